```python
import jax, jax.numpy as jnp
from jax import lax
import numpy as np

D_MODEL = 1024
BATCH = 4
SEQ = 4096
DEPTH = 2
DEC_BATCH = 128
DEC_SEQ = 4
PAST_LEN = 8192
PAGE_SIZE = 128

WINDOW = 128
SWA_BLOCK = WINDOW
HD_A = 64
HQ_A = 8
HKV_A = 2
G_A = HQ_A // HKV_A
H_B = 4
DK_B = 64
DV_B = 128
GATE_RANK = 16
GATE_TAU = 16.0
GLA_CHUNK = 64
CONV_W = 3
D_FF = ((8 * D_MODEL // 3 + 127) // 128) * 128
N_EVEN = (DEPTH + 1) // 2
N_ODD = DEPTH // 2
EPS = 1e-6
SPLIT_EVEN = (HQ_A * HD_A, HKV_A * HD_A, HKV_A * HD_A, H_B * DK_B, H_B * DK_B, H_B * DV_B, H_B * DV_B, GATE_RANK)
D_IN_EVEN = sum(SPLIT_EVEN)
D_MIX_EVEN = HQ_A * HD_A + H_B * DV_B

kernel_name = "hybrid_swa_gla_shortconv_convffn_step"


def rmsnorm(x, g):
    x32 = x.astype(jnp.float32)
    y = x32 * lax.rsqrt(jnp.mean(x32 * x32, axis=-1, keepdims=True) + EPS) * g.astype(jnp.float32)
    return y.astype(x.dtype)


def alibi_slopes(n):
    return jnp.asarray(2.0 ** (-8.0 * np.arange(1, n + 1) / n), dtype=jnp.float32)


def causal_dwconv(x, past, w):
    L = x.shape[1]
    xp = jnp.concatenate([past.astype(x.dtype), x], axis=1)
    y = w[0] * xp[:, 0:L]
    for i in range(1, CONV_W):
        y = y + w[i] * xp[:, i:i + L]
    return y, xp[:, L:]


def sink_attention(q, k, v, dist, valid, slopes, sinks):
    s = jnp.einsum('nbqkgd,nbskd->nbkgqs', q, k).astype(jnp.float32) * (HD_A ** -0.5)
    s = s - slopes.reshape(HKV_A, G_A, 1, 1) * dist.astype(jnp.float32)
    s = jnp.where(valid[None, :, None, None], s, -jnp.inf)
    sink = jnp.broadcast_to(sinks.astype(jnp.float32).reshape(HKV_A, G_A, 1, 1), s.shape[:-1] + (1,))
    p = jax.nn.softmax(jnp.concatenate([s, sink], axis=-1), axis=-1)[..., :-1]
    return jnp.einsum('nbkgqs,nbskd->nbqkgd', p.astype(v.dtype), v)


def swa_prompt(q, k, v, slopes, sinks):
    N, S, _ = q.shape
    nb = S // SWA_BLOCK
    qb = q.reshape(N, nb, SWA_BLOCK, HKV_A, G_A, HD_A)

    def with_prev(t):
        tb = t.reshape(N, nb, SWA_BLOCK, HKV_A, HD_A)
        prev = jnp.pad(tb, ((0, 0), (1, 0), (0, 0), (0, 0), (0, 0)))[:, :-1]
        return jnp.concatenate([prev, tb], axis=2)

    qi = jnp.arange(SWA_BLOCK)[:, None]
    sj = jnp.arange(2 * SWA_BLOCK)[None, :]
    dist = SWA_BLOCK + qi - sj
    blk = jnp.arange(nb)[:, None, None]
    valid = (dist >= 0) & (dist <= WINDOW) & ((blk - 1) * SWA_BLOCK + sj >= 0)
    o = sink_attention(qb, with_prev(k), with_prev(v), dist, valid, slopes, sinks)
    return o.reshape(N, S, HQ_A * HD_A)


def swa_decode(q, k, v, buf_k, buf_v, slopes, sinks):
    N, L, _ = q.shape
    kk = jnp.concatenate([buf_k.astype(k.dtype), k], axis=1)
    vv = jnp.concatenate([buf_v.astype(v.dtype), v], axis=1)
    dist = WINDOW + jnp.arange(L)[:, None] - jnp.arange(WINDOW + L)[None, :]
    valid = ((dist >= 0) & (dist <= WINDOW))[None]
    o = sink_attention(q.reshape(N, 1, L, HKV_A, G_A, HD_A), kk[:, None], vv[:, None], dist, valid, slopes, sinks)
    return o.reshape(N, L, HQ_A * HD_A), kk[:, -WINDOW:], vv[:, -WINDOW:]


def gla_chunked(q, k, v, g, S0):
    N, L, H, dk = q.shape
    dv = v.shape[-1]
    nc = L // GLA_CHUNK
    to_chunks = lambda t: t.reshape(N, nc, GLA_CHUNK, H, t.shape[-1]).transpose(1, 0, 2, 3, 4)
    mask = jnp.tril(jnp.ones((GLA_CHUNK, GLA_CHUNK), dtype=bool))[None, :, :, None, None]

    def step(S, inp):
        qc, kc, vc, gc = inp
        b = jnp.cumsum(gc, axis=1)
        o_inter = jnp.einsum('nchd,nhde->nche', qc * jnp.exp(b), S)
        decay = jnp.exp(jnp.where(mask, b[:, :, None] - b[:, None, :], -jnp.inf))
        A = jnp.einsum('nihd,njhd,nijhd->nhij', qc, kc, decay)
        o_intra = jnp.einsum('nhij,njhe->nihe', A, vc)
        bC = b[:, -1]
        S = jnp.exp(bC)[..., None] * S + jnp.einsum('nchd,nche->nhde', kc * jnp.exp(bC[:, None] - b), vc)
        return S, o_inter + o_intra

    S, o = lax.scan(step, S0, (to_chunks(q), to_chunks(k), to_chunks(v), to_chunks(g)))
    return o.transpose(1, 0, 2, 3, 4).reshape(N, L, H, dv), S


def gla_recurrent(q, k, v, g, S0):
    def step(S, inp):
        qt, kt, vt, gt = inp
        S = jnp.exp(gt)[..., None] * S + kt[..., None] * vt[..., None, :]
        return S, jnp.einsum('nhd,nhde->nhe', qt, S)

    S, o = lax.scan(step, S0, tuple(jnp.swapaxes(t, 0, 1) for t in (q, k, v, g)))
    return jnp.swapaxes(o, 0, 1), S


def even_mixer(h, w_in, w_gate_up, b_gate, sinks, gla_norm, w_out, past):
    f32 = jnp.float32
    N, L, _ = h.shape
    idx = list(np.cumsum(SPLIT_EVEN)[:-1])
    qa, ka, va, qb, kb, vb, rb, glr = jnp.split(h @ w_in, idx, axis=-1)
    slopes = alibi_slopes(HQ_A)
    ka = ka.reshape(N, L, HKV_A, HD_A)
    va = va.reshape(N, L, HKV_A, HD_A)
    qb = qb.reshape(N, L, H_B, DK_B).astype(f32) * (DK_B ** -0.5)
    kb = kb.reshape(N, L, H_B, DK_B).astype(f32)
    vb = vb.reshape(N, L, H_B, DV_B).astype(f32)
    gb = (jax.nn.log_sigmoid((glr @ w_gate_up + b_gate).astype(f32)) / GATE_TAU).reshape(N, L, H_B, DK_B)
    if past is None:
        oa = swa_prompt(qa, ka, va, slopes, sinks)
        new_k, new_v = ka[:, -WINDOW:], va[:, -WINDOW:]
        ob, S = gla_chunked(qb, kb, vb, gb, jnp.zeros((N, H_B, DK_B, DV_B), f32))
    else:
        buf_k, buf_v, S0 = past
        oa, new_k, new_v = swa_decode(qa, ka, va, buf_k, buf_v, slopes, sinks)
        ob, S = gla_recurrent(qb, kb, vb, gb, S0.astype(f32))
    ob = ob * lax.rsqrt(jnp.mean(ob * ob, axis=-1, keepdims=True) + EPS)
    ob = ob.reshape(N, L, H_B * DV_B) * gla_norm.astype(f32) * jax.nn.silu(rb.astype(f32))
    y = jnp.concatenate([oa.astype(h.dtype), ob.astype(h.dtype)], axis=-1) @ w_out
    return y, new_k, new_v, S


def odd_mixer(h, w_in, conv_w, w_out, past):
    bg, cg, u = jnp.split(h @ w_in, 3, axis=-1)
    z, new = causal_dwconv(cg * u, past, conv_w)
    return (bg * z) @ w_out, new


def conv_ffn(h, w_up, conv_w, conv_b, w_down, past):
    u = h @ w_up
    uc, new = causal_dwconv(u, past, conv_w)
    gate, val = jnp.split(uc + conv_b, 2, axis=-1)
    return (jax.nn.gelu(gate, approximate=True) * val) @ w_down, new


def trunk(x, past, weights):
    (norm_mix_pre, norm_mix_post, norm_ffn_pre, norm_ffn_post, w_in_even, w_gate_up, b_gate, attn_sinks,
     gla_norm, w_out_even, w_in_odd, conv_w_odd, w_out_odd, ffn_up, ffn_conv_w, ffn_conv_b, ffn_down) = weights
    N = x.shape[0]
    ks, vs, gs, cs, fs = [], [], [], [], []
    for l in range(DEPTH):
        h = rmsnorm(x, norm_mix_pre[l])
        if l % 2 == 0:
            e = l // 2
            p = None if past is None else (past[0][e], past[1][e], past[2][e])
            m, nk, nv, S = even_mixer(h, w_in_even[e], w_gate_up[e], b_gate[e], attn_sinks[e], gla_norm[e],
                                      w_out_even[e], p)
            ks.append(nk)
            vs.append(nv)
            gs.append(S)
        else:
            o = l // 2
            p = jnp.zeros((N, CONV_W - 1, D_MODEL), x.dtype) if past is None else past[3][o]
            m, nc = odd_mixer(h, w_in_odd[o], conv_w_odd[o], w_out_odd[o], p)
            cs.append(nc)
        x = x + rmsnorm(m, norm_mix_post[l])
        h = rmsnorm(x, norm_ffn_pre[l])
        p = jnp.zeros((N, CONV_W - 1, 2 * D_FF), x.dtype) if past is None else past[4][l]
        f, nf = conv_ffn(h, ffn_up[l], ffn_conv_w[l], ffn_conv_b[l], ffn_down[l], p)
        fs.append(nf)
        x = x + rmsnorm(f, norm_ffn_post[l])
    return x, jnp.stack(ks), jnp.stack(vs), jnp.stack(gs), jnp.stack(cs), jnp.stack(fs)


def setup_inputs(seed: int = 0) -> dict:
    key = jax.random.key(seed)
    ks = jax.random.split(key, 24)
    f32 = jnp.float32
    D = D_MODEL
    F2 = 2 * D_FF

    def nrm(k, shape, scale):
        return jax.random.normal(k, shape, f32) * scale

    return {
        "x_prompt": nrm(ks[0], (BATCH, SEQ, D), 1.0),
        "x_sample": nrm(ks[1], (DEC_BATCH, DEC_SEQ, D), 1.0),
        "cache_swa_k": nrm(ks[2], (N_EVEN, DEC_BATCH, WINDOW, HKV_A, HD_A), 1.0),
        "cache_swa_v": nrm(ks[3], (N_EVEN, DEC_BATCH, WINDOW, HKV_A, HD_A), 1.0),
        "state_gla": nrm(ks[4], (N_EVEN, DEC_BATCH, H_B, DK_B, DV_B), 0.5),
        "state_conv": nrm(ks[5], (N_ODD, DEC_BATCH, CONV_W - 1, D), 1.0),
        "state_ffn": nrm(ks[6], (DEPTH, DEC_BATCH, CONV_W - 1, F2), 1.0),
        "norm_mix_pre": 1.0 + nrm(ks[7], (DEPTH, D), 0.05),
        "norm_mix_post": 1.0 + nrm(ks[8], (DEPTH, D), 0.05),
        "norm_ffn_pre": 1.0 + nrm(ks[9], (DEPTH, D), 0.05),
        "norm_ffn_post": 1.0 + nrm(ks[10], (DEPTH, D), 0.05),
        "w_in_even": nrm(ks[11], (N_EVEN, D, D_IN_EVEN), D ** -0.5),
        "w_gate_up": nrm(ks[12], (N_EVEN, GATE_RANK, H_B * DK_B), GATE_RANK ** -0.5),
        "b_gate": nrm(ks[13], (N_EVEN, H_B * DK_B), 0.1),
        "attn_sinks": nrm(ks[14], (N_EVEN, HQ_A), 0.5),
        "gla_norm": 1.0 + nrm(ks[15], (N_EVEN, H_B * DV_B), 0.05),
        "w_out_even": nrm(ks[16], (N_EVEN, D_MIX_EVEN, D), D_MIX_EVEN ** -0.5),
        "w_in_odd": nrm(ks[17], (N_ODD, D, 3 * D), D ** -0.5),
        "conv_w_odd": nrm(ks[18], (N_ODD, CONV_W, D), CONV_W ** -0.5),
        "w_out_odd": nrm(ks[19], (N_ODD, D, D), D ** -0.5),
        "ffn_up": nrm(ks[20], (DEPTH, D, F2), D ** -0.5),
        "ffn_conv_w": nrm(ks[21], (DEPTH, CONV_W, F2), CONV_W ** -0.5),
        "ffn_conv_b": nrm(ks[22], (DEPTH, F2), 0.02),
        "ffn_down": nrm(ks[23], (DEPTH, D_FF, D), D_FF ** -0.5),
    }


def reference(x_prompt, x_sample, cache_swa_k, cache_swa_v, state_gla, state_conv, state_ffn,
              norm_mix_pre, norm_mix_post, norm_ffn_pre, norm_ffn_post, w_in_even, w_gate_up, b_gate,
              attn_sinks, gla_norm, w_out_even, w_in_odd, conv_w_odd, w_out_odd, ffn_up, ffn_conv_w,
              ffn_conv_b, ffn_down):
    weights = (norm_mix_pre, norm_mix_post, norm_ffn_pre, norm_ffn_post, w_in_even, w_gate_up, b_gate,
               attn_sinks, gla_norm, w_out_even, w_in_odd, conv_w_odd, w_out_odd, ffn_up, ffn_conv_w,
               ffn_conv_b, ffn_down)
    y_prompt, swa_k_p, swa_v_p, gla_p, conv_p, ffn_p = trunk(x_prompt, None, weights)
    y_sample, swa_k_s, swa_v_s, gla_s, conv_s, ffn_s = trunk(
        x_sample, (cache_swa_k, cache_swa_v, state_gla, state_conv, state_ffn), weights)
    return (y_prompt, y_sample, swa_k_p, swa_v_p, gla_p, conv_p, ffn_p, swa_k_s, swa_v_s, gla_s, conv_s, ffn_s)
```

```python
import functools

import numpy as np
import jax
import jax.numpy as jnp
from jax import lax
from jax.experimental import pallas as pl
from jax.experimental.pallas import tpu as pltpu

F32 = jnp.float32
BF16 = jnp.bfloat16

D_MODEL = 1024
WINDOW = 128
HD_A = 64
HQ_A = 8
HKV_A = 2
G_A = HQ_A // HKV_A
H_B = 4
DK_B = 64
DV_B = 128
GATE_RANK = 16
GATE_TAU = 16.0
GLA_CHUNK = 64
D_FF = ((8 * D_MODEL // 3 + 127) // 128) * 128
F2 = 2 * D_FF
EPS = 1e-6

QA_W = HQ_A * HD_A
KV_W = 2 * HKV_A * HD_A
QB_W = H_B * DK_B
VB_W = H_B * DV_B
MAIN_W = QA_W + KV_W + 2 * QB_W + 2 * VB_W

LANES = 128
SUBLANES = 8
VMEM_LIMIT = 56 * 1024 * 1024

TM_TOK = 512
TM_GLA = 256
TQ_SWA = 512
GLA_GROUP = 128
GLA_SUB = 8
FFN_TF = 256
SEQ_BLOCK = 16


def _cparams(*sem):
    return pltpu.CompilerParams(dimension_semantics=sem, vmem_limit_bytes=VMEM_LIMIT)


def _rms(x, g):
    return x * lax.rsqrt(jnp.mean(x * x, axis=-1, keepdims=True) + EPS) * g


def _dot(a, b):
    return jnp.dot(a, b, preferred_element_type=F32)


def _dot_nt(a, b):
    return lax.dot_general(a, b, (((1,), (1,)), ((), ())), preferred_element_type=F32)


def _dot_tn(a, b):
    return lax.dot_general(a, b, (((0,), (0,)), ((), ())), preferred_element_type=F32)


def _split3(x):
    hi = x.astype(BF16)
    r1 = x - hi.astype(F32)
    mid = r1.astype(BF16)
    lo = (r1 - mid.astype(F32)).astype(BF16)
    return hi, mid, lo


def _inproj_even_kernel(x_ref, gpre_ref, w_ref, wg_ref, wgu_ref, bg_ref,
                        qa_ref, kv_ref, qb_ref, kb_ref, vb_ref, rb_ref, gb_ref):
    h = _rms(x_ref[...], gpre_ref[...]).astype(BF16)

    def mm(lo, width):
        return _dot(h, w_ref[:, lo:lo + width])

    lo = 0
    qa_ref[...] = (mm(lo, QA_W) * (HD_A ** -0.5)).astype(BF16)
    lo += QA_W
    kv_ref[...] = mm(lo, KV_W)
    lo += KV_W
    qb_ref[...] = mm(lo, QB_W) * (DK_B ** -0.5)
    lo += QB_W
    kb_ref[...] = mm(lo, QB_W)
    lo += QB_W
    vb_ref[...] = mm(lo, VB_W).astype(BF16)
    lo += VB_W
    rb_ref[...] = mm(lo, VB_W)
    glr = _dot(h, wg_ref[...]).astype(BF16)
    z = _dot(glr, wgu_ref[...]) + bg_ref[...]
    gb_ref[...] = jax.nn.log_sigmoid(z) * (1.0 / GATE_TAU)


def _inproj_even(x2d, gpre, w_main, w_g, w_gu, b_g):
    T = x2d.shape[0]
    tm = min(TM_TOK, T)
    row = lambda w: pl.BlockSpec((tm, w), lambda i: (i, 0))
    full = lambda a: pl.BlockSpec(a.shape, lambda i: (0,) * a.ndim)
    out_shape = (
        jax.ShapeDtypeStruct((T, QA_W), BF16), jax.ShapeDtypeStruct((T, KV_W), F32),
        jax.ShapeDtypeStruct((T, QB_W), F32), jax.ShapeDtypeStruct((T, QB_W), F32),
        jax.ShapeDtypeStruct((T, VB_W), BF16), jax.ShapeDtypeStruct((T, VB_W), F32),
        jax.ShapeDtypeStruct((T, QB_W), F32))
    return pl.pallas_call(
        _inproj_even_kernel, grid=(T // tm,),
        in_specs=[row(D_MODEL), full(gpre), full(w_main), full(w_g), full(w_gu), full(b_g)],
        out_specs=(row(QA_W), row(KV_W), row(QB_W), row(QB_W), row(VB_W), row(VB_W), row(QB_W)),
        out_shape=out_shape, compiler_params=_cparams("parallel"), name="inproj_even",
    )(x2d, gpre, w_main, w_g, w_gu, b_g)


def _proj_resid_kernel(y_ref, x_ref, w_ref, g_ref, o_ref):
    o_ref[...] = x_ref[...] + _rms(_dot(y_ref[...], w_ref[...]), g_ref[...])


def _proj_resid(y2d, x2d, w, gpost):
    T = x2d.shape[0]
    tm = min(TM_TOK, T)
    row = lambda w_: pl.BlockSpec((tm, w_), lambda i: (i, 0))
    full = lambda a: pl.BlockSpec(a.shape, lambda i: (0,) * a.ndim)
    return pl.pallas_call(
        _proj_resid_kernel, grid=(T // tm,),
        in_specs=[row(y2d.shape[1]), row(D_MODEL), full(w), full(gpost)],
        out_specs=row(D_MODEL), out_shape=jax.ShapeDtypeStruct((T, D_MODEL), F32),
        compiler_params=_cparams("parallel"), name="proj_resid",
    )(y2d, x2d, w, gpost)


def _conv3_from_buf(buf, cw, cur, tm, past, rs):
    y = cw[0:1, :] * buf[past - 2 * rs:past - 2 * rs + tm, :]
    y = y + cw[1:2, :] * buf[past - rs:past - rs + tm, :]
    return y + cw[2:3, :] * cur


def _odd_in_kernel(x_ref, past_ref, gpre_ref, win_ref, cw_ref, y_ref, st_ref, carry, buf,
                   *, tm, past, rs, tc):
    @pl.when(pl.program_id(1) == 0)
    def _():
        carry[...] = past_ref[0]

    h = _rms(x_ref[0], gpre_ref[...]).astype(BF16)
    for c in range(D_MODEL // tc):
        lo = c * tc
        bg = _dot(h, win_ref[:, lo:lo + tc])
        cu = _dot(h, win_ref[:, D_MODEL + lo:D_MODEL + lo + tc]) * \
            _dot(h, win_ref[:, 2 * D_MODEL + lo:2 * D_MODEL + lo + tc])
        buf[0:past, :] = carry[:, lo:lo + tc]
        buf[past:past + tm, :] = cu
        carry[:, lo:lo + tc] = buf[tm:tm + past, :]
        z = _conv3_from_buf(buf, cw_ref[:, lo:lo + tc], cu, tm, past, rs)
        y_ref[0, :, lo:lo + tc] = (bg * z).astype(BF16)
    st_ref[0] = carry[...]


def _odd_in(x3d, past0, gpre, w_in, conv_w, *, tm, rs):
    nb, rows, _ = x3d.shape
    past = past0.shape[1]
    tc = 512
    kern = functools.partial(_odd_in_kernel, tm=tm, past=past, rs=rs, tc=tc)
    full = lambda a: pl.BlockSpec(a.shape, lambda b, j: (0,) * a.ndim)
    return pl.pallas_call(
        kern, grid=(nb, rows // tm),
        in_specs=[pl.BlockSpec((1, tm, D_MODEL), lambda b, j: (b, j, 0)),
                  pl.BlockSpec((1, past, D_MODEL), lambda b, j: (b, 0, 0)),
                  full(gpre), full(w_in), full(conv_w)],
        out_specs=(pl.BlockSpec((1, tm, D_MODEL), lambda b, j: (b, j, 0)),
                   pl.BlockSpec((1, past, D_MODEL), lambda b, j: (b, 0, 0))),
        out_shape=(jax.ShapeDtypeStruct((nb, rows, D_MODEL), BF16),
                   jax.ShapeDtypeStruct((nb, past, D_MODEL), F32)),
        scratch_shapes=[pltpu.VMEM((past, D_MODEL), F32), pltpu.VMEM((past + tm, tc), F32)],
        compiler_params=_cparams("parallel", "arbitrary"), name="odd_in",
    )(x3d, past0, gpre, w_in, conv_w)


def _ffn_kernel(x_ref, past_ref, gpre_ref, wup_ref, cw_ref, cb_ref, wdn_ref, gpost_ref,
                o_ref, st_ref, carry, buf_g, buf_v, acc, *, tm, past, rs, tf):
    @pl.when(pl.program_id(1) == 0)
    def _():
        carry[...] = past_ref[0]

    x = x_ref[0]
    h = _rms(x, gpre_ref[...]).astype(BF16)
    for c in range(D_FF // tf):
        halves = []
        for lo, buf in ((c * tf, buf_g), (D_FF + c * tf, buf_v)):
            u = _dot(h, wup_ref[:, lo:lo + tf])
            buf[0:past, :] = carry[:, lo:lo + tf]
            buf[past:past + tm, :] = u
            carry[:, lo:lo + tf] = buf[tm:tm + past, :]
            halves.append(_conv3_from_buf(buf, cw_ref[:, lo:lo + tf], u, tm, past, rs)
                          + cb_ref[:, lo:lo + tf])
        act = (jax.nn.gelu(halves[0], approximate=True) * halves[1]).astype(BF16)
        part = _dot(act, wdn_ref[c * tf:(c + 1) * tf, :])
        if c == 0:
            acc[...] = part
        else:
            acc[...] += part
    o_ref[0] = x + _rms(acc[...], gpost_ref[...])
    st_ref[0] = carry[...]


def _ffn(x3d, past0, gpre, w_up, conv_w, conv_b, w_dn, gpost, *, tm, rs):
    nb, rows, _ = x3d.shape
    past = past0.shape[1]
    kern = functools.partial(_ffn_kernel, tm=tm, past=past, rs=rs, tf=FFN_TF)
    full = lambda a: pl.BlockSpec(a.shape, lambda b, j: (0,) * a.ndim)
    return pl.pallas_call(
        kern, grid=(nb, rows // tm),
        in_specs=[pl.BlockSpec((1, tm, D_MODEL), lambda b, j: (b, j, 0)),
                  pl.BlockSpec((1, past, F2), lambda b, j: (b, 0, 0)),
                  full(gpre), full(w_up), full(conv_w), full(conv_b), full(w_dn), full(gpost)],
        out_specs=(pl.BlockSpec((1, tm, D_MODEL), lambda b, j: (b, j, 0)),
                   pl.BlockSpec((1, past, F2), lambda b, j: (b, 0, 0))),
        out_shape=(jax.ShapeDtypeStruct((nb, rows, D_MODEL), F32),
                   jax.ShapeDtypeStruct((nb, past, F2), F32)),
        scratch_shapes=[pltpu.VMEM((past, F2), F32), pltpu.VMEM((past + tm, FFN_TF), F32),
                        pltpu.VMEM((past + tm, FFN_TF), F32), pltpu.VMEM((tm, D_MODEL), F32)],
        compiler_params=_cparams("parallel", "arbitrary"), name="conv_ffn",
    )(x3d, past0, gpre, w_up, conv_w, conv_b, w_dn, gpost)


def _swa_bias_prompt():
    qi = np.arange(WINDOW)[:, None]
    sj = np.arange(2 * WINDOW)[None, :]
    dist = WINDOW + qi - sj
    valid = (dist >= 0) & (dist <= WINDOW)
    slopes = 2.0 ** (-8.0 * np.arange(1, HQ_A + 1) / HQ_A)
    bias = np.where(valid[None], -slopes[:, None, None] * dist[None].astype(np.float64), -np.inf)
    return jnp.asarray(bias, dtype=F32)


def _swa_prompt_kernel(sink_ref, q_ref, kvc_ref, kvp_ref, bias_ref, o_ref, *, tq):
    i = pl.program_id(1)
    kvc = kvc_ref[0]
    kvp = kvp_ref[0]
    kcat = jnp.concatenate([kvp[:, 0:LANES], kvc[:, 0:LANES]], axis=0).astype(BF16)
    vcat = jnp.concatenate([kvp[:, LANES:], kvc[:, LANES:]], axis=0).astype(BF16)
    lane = lax.broadcasted_iota(jnp.int32, (WINDOW, LANES), 1)
    col = lax.broadcasted_iota(jnp.int32, (WINDOW, 2 * WINDOW), 1)
    for j in range(tq // WINDOW):
        keys = kcat[j * WINDOW:(j + 2) * WINDOW, :]
        vals = vcat[j * WINDOW:(j + 2) * WINDOW, :]
        for g in range(G_A):
            q2 = q_ref[0, j * WINDOW:(j + 1) * WINDOW, g * LANES:(g + 1) * LANES]
            outs = []
            for kv in range(HKV_A):
                hq = kv * G_A + g
                in_head = (lane < HD_A) if kv == 0 else (lane >= HD_A)
                qm = jnp.where(in_head, q2, jnp.zeros_like(q2))
                s = _dot_nt(qm, keys) + bias_ref[hq]
                if j == 0:
                    s = jnp.where(jnp.logical_and(i == 0, col < WINDOW), -jnp.inf, s)
                sink = sink_ref[hq]
                m = jnp.maximum(jnp.max(s, axis=-1, keepdims=True), sink)
                p = jnp.exp(s - m)
                den = jnp.sum(p, axis=-1, keepdims=True) + jnp.exp(sink - m)
                outs.append(_dot(p.astype(BF16), vals) / den)
            o2 = jnp.where(lane < HD_A, outs[0], outs[1])
            o_ref[0, j * WINDOW:(j + 1) * WINDOW, g * LANES:(g + 1) * LANES] = o2.astype(BF16)


def _swa_prompt(q3d, kv3d, sinks):
    nb, S, _ = q3d.shape
    tq = min(TQ_SWA, S)
    bias = _swa_bias_prompt()
    kern = functools.partial(_swa_prompt_kernel, tq=tq)
    blocks_per_tile = tq // WINDOW
    return pl.pallas_call(
        kern, grid=(nb, S // tq),
        in_specs=[pl.BlockSpec(memory_space=pltpu.SMEM),
                  pl.BlockSpec((1, tq, QA_W), lambda b, i: (b, i, 0)),
                  pl.BlockSpec((1, tq, KV_W), lambda b, i: (b, i, 0)),
                  pl.BlockSpec((1, WINDOW, KV_W),
                               lambda b, i: (b, jnp.maximum(i * blocks_per_tile - 1, 0), 0)),
                  pl.BlockSpec(bias.shape, lambda b, i: (0, 0, 0))],
        out_specs=pl.BlockSpec((1, tq, QA_W), lambda b, i: (b, i, 0)),
        out_shape=jax.ShapeDtypeStruct((nb, S, QA_W), BF16),
        compiler_params=_cparams("parallel", "arbitrary"), name="swa_prompt",
    )(sinks, q3d, kv3d, kv3d, bias)


def _gla_constants():
    r = np.arange(GLA_GROUP)
    tri = ((r[:, None] >= r[None, :]) & (r[:, None] // GLA_CHUNK == r[None, :] // GLA_CHUNK))
    ones_bd = (np.arange(QB_W)[:, None] // DK_B) == (np.arange(VB_W)[None, :] // DV_B)
    masks = []
    for s in (8, 16, 32):
        i, j = r[:, None], r[None, :]
        masks.append((i // (2 * s) == j // (2 * s)) & (i % (2 * s) >= s) & (j % (2 * s) < s))
    return (jnp.asarray(tri, dtype=BF16), jnp.asarray(ones_bd, dtype=BF16),
            jnp.asarray(np.stack(masks), dtype=F32))


def _bcast_rows(x, period, row):
    t, c = x.shape
    x3 = x.reshape(t // period, period, c)
    return jnp.broadcast_to(x3[:, row:row + 1, :], (t // period, period, c)).reshape(t, c)


def _head_norm_gate(o, r, gn):
    parts = []
    for h in range(H_B):
        oh = o[:, h * DV_B:(h + 1) * DV_B]
        parts.append(oh * lax.rsqrt(jnp.mean(oh * oh, axis=-1, keepdims=True) + EPS))
    return jnp.concatenate(parts, axis=-1) * gn * jax.nn.silu(r)


def _gla_prompt_kernel(q_ref, k_ref, v_ref, g_ref, r_ref, gn_ref, tri_ref, ones_ref, lvl_ref,
                       o_ref, st_ref, state, b_scr, o_scr, *, tm):
    @pl.when(pl.program_id(1) == 0)
    def _():
        state[...] = jnp.zeros_like(state)

    q = q_ref[0]
    k = k_ref[0]
    v = v_ref[0]
    lane = lax.broadcasted_iota(jnp.int32, (1, LANES), 1)
    in_even = lane < DK_B

    ghi, gmid, glo = _split3(g_ref[0])
    tri = tri_ref[...]
    for gi in range(tm // GLA_GROUP):
        rows = slice(gi * GLA_GROUP, (gi + 1) * GLA_GROUP)
        b_scr[rows, :] = _dot(tri, ghi[rows]) + _dot(tri, gmid[rows]) + _dot(tri, glo[rows])
    b = b_scr[...]

    vf = v.astype(F32)
    sub = lax.broadcasted_iota(jnp.int32, (tm, 1), 0) % GLA_SUB
    odiag = jnp.zeros((tm, VB_W), F32)
    for jj in range(GLA_SUB):
        kj = _bcast_rows(k, GLA_SUB, jj)
        bj = _bcast_rows(b, GLA_SUB, jj)
        vj = _bcast_rows(vf, GLA_SUB, jj)
        p = (q * kj * jnp.exp(jnp.minimum(b - bj, 0.0))).astype(BF16)
        a = _dot(p, ones_ref[...])
        odiag = odiag + jnp.where(sub >= jj, a, 0.0) * vj
    o_scr[...] = odiag

    for gi in range(tm // GLA_GROUP):
        rows = slice(gi * GLA_GROUP, (gi + 1) * GLA_GROUP)
        qg, kg, bgrp = q[rows], k[rows], b[rows]
        a_heads = [jnp.zeros((GLA_GROUP, GLA_GROUP), F32) for _ in range(H_B)]
        for li, s in enumerate((8, 16, 32)):
            ref_b = _bcast_rows(bgrp, 2 * s, s - 1)
            qe = (qg * jnp.exp(jnp.minimum(bgrp - ref_b, 0.0))).astype(BF16)
            ke = (kg * jnp.exp(jnp.minimum(ref_b - bgrp, 0.0))).astype(BF16)
            keep = lvl_ref[li] > 0.5
            for h in range(H_B):
                pr = slice((h // 2) * LANES, (h // 2 + 1) * LANES)
                sel = in_even if h % 2 == 0 else jnp.logical_not(in_even)
                kem = jnp.where(sel, ke[:, pr], jnp.zeros_like(ke[:, pr]))
                a_heads[h] = a_heads[h] + jnp.where(keep, _dot_nt(qe[:, pr], kem), 0.0)
        for h in range(H_B):
            cols = slice(h * DV_B, (h + 1) * DV_B)
            o_scr[rows, cols] += _dot(a_heads[h].astype(BF16), v[rows, cols])

    for c in range(tm // GLA_CHUNK):
        rows = slice(c * GLA_CHUNK, (c + 1) * GLA_CHUNK)
        bc = b[rows]
        b_end = bc[GLA_CHUNK - 1:GLA_CHUNK, :]
        qe = (q[rows] * jnp.exp(bc)).astype(BF16)
        ke = (k[rows] * jnp.exp(b_end - bc)).astype(BF16)
        st = state[...]
        st_bf = st.astype(BF16)
        upd = []
        for pi in range(H_B // 2):
            pr = slice(pi * LANES, (pi + 1) * LANES)
            acc = jnp.zeros((DV_B, LANES), F32)
            for h in (2 * pi, 2 * pi + 1):
                sel = in_even if h % 2 == 0 else jnp.logical_not(in_even)
                cols = slice(h * DV_B, (h + 1) * DV_B)
                qm = jnp.where(sel, qe[:, pr], jnp.zeros_like(qe[:, pr]))
                o_scr[rows, cols] += _dot_nt(qm, st_bf[:, pr])
                kem = jnp.where(sel, ke[:, pr], jnp.zeros_like(ke[:, pr]))
                acc = acc + _dot_tn(v[rows, cols], kem)
            upd.append(acc)
        state[...] = st * jnp.exp(b_end) + jnp.concatenate(upd, axis=-1)

    o_ref[0] = _head_norm_gate(o_scr[...], r_ref[0], gn_ref[...]).astype(BF16)
    st_ref[0] = state[...]


def _gla_prompt(q3d, k3d, v3d, g3d, r3d, gn):
    nb, S, _ = q3d.shape
    tm = min(TM_GLA, S)
    tri, ones_bd, lvl = _gla_constants()
    kern = functools.partial(_gla_prompt_kernel, tm=tm)
    tok = lambda w: pl.BlockSpec((1, tm, w), lambda b, j: (b, j, 0))
    full = lambda a: pl.BlockSpec(a.shape, lambda b, j: (0,) * a.ndim)
    return pl.pallas_call(
        kern, grid=(nb, S // tm),
        in_specs=[tok(QB_W), tok(QB_W), tok(VB_W), tok(QB_W), tok(VB_W),
                  full(gn), full(tri), full(ones_bd), full(lvl)],
        out_specs=(tok(VB_W), pl.BlockSpec((1, DV_B, QB_W), lambda b, j: (b, 0, 0))),
        out_shape=(jax.ShapeDtypeStruct((nb, S, VB_W), BF16),
                   jax.ShapeDtypeStruct((nb, DV_B, QB_W), F32)),
        scratch_shapes=[pltpu.VMEM((DV_B, QB_W), F32), pltpu.VMEM((tm, QB_W), F32),
                        pltpu.VMEM((tm, VB_W), F32)],
        compiler_params=_cparams("parallel", "arbitrary"), name="gla_prompt",
    )(q3d, k3d, v3d, g3d, r3d, gn, tri, ones_bd, lvl)


def _swa_bias_decode(L):
    rows = HQ_A * L
    kpad = WINDOW + SUBLANES
    hq = np.arange(rows)[:, None] // L
    t = np.arange(rows)[:, None] % L
    s = np.arange(kpad)[None, :]
    dist = WINDOW + t - s
    valid = (dist >= 0) & (dist <= WINDOW) & (s < WINDOW + L)
    slopes = 2.0 ** (-8.0 * (hq + 1) / HQ_A)
    return jnp.asarray(np.where(valid, -slopes * dist, -np.inf), dtype=F32)


def _swa_decode_kernel(q_ref, kvn_ref, ck_ref, cv_ref, bias_ref, sink_ref,
                       o_ref, nk_ref, nv_ref, qs, os_, kc, vc, kn, vn, *, ns, L, sb):
    i = pl.program_id(0)
    rows_q = HQ_A * L
    lane = lax.broadcasted_iota(jnp.int32, (1, LANES), 1)

    @pl.when(i == 0)
    def _():
        q = q_ref[...].astype(F32)
        for kv in range(HKV_A):
            sel = (lane < HD_A) if kv == 0 else (lane >= HD_A)
            for g in range(G_A):
                base = (kv * G_A + g) * L * ns
                qs[base:base + L * ns, :] = jnp.where(sel, q[:, g * LANES:(g + 1) * LANES], 0.0)
        kc[...] = jnp.zeros_like(kc)
        vc[...] = jnp.zeros_like(vc)
        kn[...] = kvn_ref[:, 0:LANES]
        vn[...] = kvn_ref[:, LANES:]

    bias = bias_ref[...]
    sink = sink_ref[...]

    def body(s, carry):
        seq = i * sb + s
        kc[0:WINDOW, :] = ck_ref[s]
        vc[0:WINDOW, :] = cv_ref[s]
        kc[WINDOW:WINDOW + L, :] = kn[pl.ds(seq, L, stride=ns), :]
        vc[WINDOW:WINDOW + L, :] = vn[pl.ds(seq, L, stride=ns), :]
        nk_ref[s] = kc[L:L + WINDOW, :]
        nv_ref[s] = vc[L:L + WINDOW, :]
        lhs = qs[pl.ds(seq, rows_q, stride=ns), :].astype(BF16)
        sc = _dot_nt(lhs, kc[...].astype(BF16)) + bias
        m = jnp.maximum(jnp.max(sc, axis=-1, keepdims=True), sink)
        p = jnp.exp(sc - m)
        den = jnp.sum(p, axis=-1, keepdims=True) + jnp.exp(sink - m)
        res = _dot(p.astype(BF16), vc[...].astype(BF16)) / den
        half = rows_q // 2
        os_[pl.ds(seq, half, stride=ns), :] = jnp.where(lane < HD_A, res[0:half], res[half:])
        return carry

    lax.fori_loop(0, sb, body, 0)

    @pl.when(i == pl.num_programs(0) - 1)
    def _():
        for g in range(G_A):
            o_ref[:, g * LANES:(g + 1) * LANES] = os_[g * L * ns:(g + 1) * L * ns, :].astype(BF16)


def _swa_decode(q_tm, kvn_tm, cache_k, cache_v, sinks, *, ns, L):
    sb = SEQ_BLOCK
    bias = _swa_bias_decode(L)
    sink_col = jnp.broadcast_to(jnp.repeat(sinks.astype(F32), L)[:, None], (HQ_A * L, 1))
    kern = functools.partial(_swa_decode_kernel, ns=ns, L=L, sb=sb)
    full = lambda a: pl.BlockSpec(a.shape, lambda i: (0,) * a.ndim)
    cache = pl.BlockSpec((sb, WINDOW, LANES), lambda i: (i, 0, 0))
    kpad = WINDOW + SUBLANES
    return pl.pallas_call(
        kern, grid=(ns // sb,),
        in_specs=[full(q_tm), full(kvn_tm), cache, cache, full(bias), full(sink_col)],
        out_specs=(pl.BlockSpec((L * ns, QA_W), lambda i: (0, 0)), cache, cache),
        out_shape=(jax.ShapeDtypeStruct((L * ns, QA_W), BF16),
                   jax.ShapeDtypeStruct(cache_k.shape, F32),
                   jax.ShapeDtypeStruct(cache_v.shape, F32)),
        scratch_shapes=[pltpu.VMEM((HQ_A * L * ns, LANES), F32),
                        pltpu.VMEM((G_A * L * ns, LANES), F32),
                        pltpu.VMEM((kpad, LANES), F32), pltpu.VMEM((kpad, LANES), F32),
                        pltpu.VMEM((L * ns, LANES), F32), pltpu.VMEM((L * ns, LANES), F32)],
        compiler_params=_cparams("arbitrary"), name="swa_decode",
    )(q_tm, kvn_tm, cache_k, cache_v, bias, sink_col)


def _gla_decode_kernel(q_ref, k_ref, v_ref, g_ref, r_ref, gn_ref, ones_ref, s0_ref,
                       o_ref, s1_ref, qe2, ke2, v2, dec3, oi2, od, *, ns, L, sb):
    i = pl.program_id(0)
    lane = lax.broadcasted_iota(jnp.int32, (1, LANES), 1)
    npair = H_B // 2

    @pl.when(i == 0)
    def _():
        slab = lambda a, t: a[t * ns:(t + 1) * ns, :]
        q, k, g = q_ref[...], k_ref[...], g_ref[...]
        vf = v_ref[...].astype(F32)
        b = [slab(g, 0)]
        for t in range(1, L):
            b.append(b[-1] + slab(g, t))
        for t in range(L):
            acc = jnp.zeros((ns, VB_W), F32)
            for jj in range(t + 1):
                p = (slab(q, t) * slab(k, jj) * jnp.exp(b[t] - b[jj])).astype(BF16)
                acc = acc + _dot(p, ones_ref[...]) * slab(vf, jj)
            od[t * ns:(t + 1) * ns, :] = acc
            qe = slab(q, t) * jnp.exp(b[t])
            ke = slab(k, t) * jnp.exp(b[L - 1] - b[t])
            for par in range(2):
                sel = (lane < DK_B) if par == 0 else (lane >= DK_B)
                base = (par * L + t) * ns
                for pi in range(npair):
                    pr = slice(pi * LANES, (pi + 1) * LANES)
                    qe2[pi, base:base + ns, :] = jnp.where(sel, qe[:, pr], 0.0)
                    ke2[pi, base:base + ns, :] = jnp.where(sel, ke[:, pr], 0.0)
                    h = 2 * pi + par
                    v2[pi, base:base + ns, :] = slab(vf, t)[:, h * DV_B:(h + 1) * DV_B]
        dec3[...] = jnp.zeros_like(dec3)
        hi, mid, lo = _split3(jnp.exp(b[L - 1]))
        for pi in range(npair):
            pr = slice(pi * LANES, (pi + 1) * LANES)
            dec3[pi, 0:ns, :] = hi[:, pr].astype(F32)
            dec3[pi, ns:2 * ns, :] = mid[:, pr].astype(F32)
            dec3[pi, 2 * ns:3 * ns, :] = lo[:, pr].astype(F32)

    ones8 = jnp.ones((2 * L, LANES), BF16)

    def body(s, carry):
        seq = i * sb + s
        st = s0_ref[s]
        for pi in range(npair):
            take = lambda ref: ref[pi, pl.ds(seq, 2 * L, stride=ns), :].astype(BF16)
            st_p = st[pi * LANES:(pi + 1) * LANES, :]
            oi2[pi, pl.ds(seq, 2 * L, stride=ns), :] = _dot(take(qe2), st_p.astype(BF16))
            dcol = _dot_tn(take(dec3), ones8)
            upd = _dot_tn(take(ke2), take(v2))
            s1_ref[s, pi * LANES:(pi + 1) * LANES, :] = dcol * st_p + upd
        return carry

    lax.fori_loop(0, sb, body, 0)

    @pl.when(i == pl.num_programs(0) - 1)
    def _():
        for t in range(L):
            parts = []
            for h in range(H_B):
                base = ((h % 2) * L + t) * ns
                parts.append(oi2[h // 2, base:base + ns, :])
            o = jnp.concatenate(parts, axis=-1) + od[t * ns:(t + 1) * ns, :]
            o_ref[t * ns:(t + 1) * ns, :] = _head_norm_gate(
                o, r_ref[t * ns:(t + 1) * ns, :], gn_ref[...]).astype(BF16)


def _gla_decode(q_tm, k_tm, v_tm, g_tm, r_tm, gn, s0, *, ns, L):
    sb = SEQ_BLOCK
    assert 2 * L == SUBLANES
    _, ones_bd, _ = _gla_constants()
    kern = functools.partial(_gla_decode_kernel, ns=ns, L=L, sb=sb)
    full = lambda a: pl.BlockSpec(a.shape, lambda i: (0,) * a.ndim)
    st = pl.BlockSpec((sb, QB_W, DV_B), lambda i: (i, 0, 0))
    rows2 = 2 * L * ns
    return pl.pallas_call(
        kern, grid=(ns // sb,),
        in_specs=[full(q_tm), full(k_tm), full(v_tm), full(g_tm), full(r_tm), full(gn),
                  full(ones_bd), st],
        out_specs=(pl.BlockSpec((L * ns, VB_W), lambda i: (0, 0)), st),
        out_shape=(jax.ShapeDtypeStruct((L * ns, VB_W), BF16),
                   jax.ShapeDtypeStruct(s0.shape, F32)),
        scratch_shapes=[pltpu.VMEM((H_B // 2, rows2, LANES), F32) for _ in range(5)]
        + [pltpu.VMEM((L * ns, VB_W), F32)],
        compiler_params=_cparams("arbitrary"), name="gla_decode",
    )(q_tm, k_tm, v_tm, g_tm, r_tm, gn, ones_bd, s0)


def _qa_perm():
    return np.asarray([(kv * G_A + g) * HD_A + d
                       for g in range(G_A) for kv in range(HKV_A) for d in range(HD_A)])


def _prep_even(w_in, w_gate_up, b_gate, w_out):
    perm = _qa_perm()
    w_main = jnp.concatenate([w_in[:, :QA_W][:, perm], w_in[:, QA_W:MAIN_W]], axis=1).astype(BF16)
    w_g = jnp.pad(w_in[:, MAIN_W:], ((0, 0), (0, LANES - GATE_RANK))).astype(BF16)
    w_gu = jnp.pad(w_gate_up, ((0, LANES - GATE_RANK), (0, 0))).astype(BF16)
    w_o = jnp.concatenate([w_out[:QA_W][perm], w_out[QA_W:]], axis=0).astype(BF16)
    return w_main, w_g, w_gu, b_gate.reshape(1, -1), w_o


def _row(v):
    return v.reshape(1, -1)


def kernel(x_prompt, x_sample, cache_swa_k, cache_swa_v, state_gla, state_conv, state_ffn,
           norm_mix_pre, norm_mix_post, norm_ffn_pre, norm_ffn_post, w_in_even, w_gate_up, b_gate,
           attn_sinks, gla_norm, w_out_even, w_in_odd, conv_w_odd, w_out_odd, ffn_up, ffn_conv_w,
           ffn_conv_b, ffn_down):
    nb, S, _ = x_prompt.shape
    ns, L, _ = x_sample.shape
    depth = norm_mix_pre.shape[0]

    xp = x_prompt
    xs = x_sample.transpose(1, 0, 2).reshape(1, L * ns, D_MODEL)
    past_p = max(SUBLANES, 2)
    past_s = 2 * ns
    tm_p = min(TM_TOK, S)

    ks_p, vs_p, gs_p, cs_p, fs_p = [], [], [], [], []
    ks_s, vs_s, gs_s, cs_s, fs_s = [], [], [], [], []

    for l in range(depth):
        gpre, gpost = _row(norm_mix_pre[l]), _row(norm_mix_post[l])
        if l % 2 == 0:
            e = l // 2
            w_main, w_g, w_gu, b_g, w_o = _prep_even(w_in_even[e], w_gate_up[e], b_gate[e],
                                                     w_out_even[e])
            gn = _row(gla_norm[e])
            qa, kv, qb, kb, vb, rb, gb = _inproj_even(xp.reshape(nb * S, D_MODEL), gpre,
                                                      w_main, w_g, w_gu, b_g)
            r3 = lambda a: a.reshape(nb, S, a.shape[-1])
            oa = _swa_prompt(r3(qa), r3(kv), attn_sinks[e])
            ob, st_t = _gla_prompt(r3(qb), r3(kb), r3(vb), r3(gb), r3(rb), gn)
            y = jnp.concatenate([oa, ob], axis=-1).reshape(nb * S, QA_W + VB_W)
            xp = _proj_resid(y, xp.reshape(nb * S, D_MODEL), w_o, gpost).reshape(nb, S, D_MODEL)
            kv_last = r3(kv)[:, S - WINDOW:, :]
            ks_p.append(kv_last[..., :LANES].reshape(nb, WINDOW, HKV_A, HD_A))
            vs_p.append(kv_last[..., LANES:].reshape(nb, WINDOW, HKV_A, HD_A))
            gs_p.append(st_t.transpose(0, 2, 1).reshape(nb, H_B, DK_B, DV_B))
            qa, kv, qb, kb, vb, rb, gb = _inproj_even(xs.reshape(L * ns, D_MODEL), gpre,
                                                      w_main, w_g, w_gu, b_g)
            oa, nk, nv = _swa_decode(qa, kv, cache_swa_k[e].reshape(ns, WINDOW, LANES),
                                     cache_swa_v[e].reshape(ns, WINDOW, LANES), attn_sinks[e],
                                     ns=ns, L=L)
            ob, s1 = _gla_decode(qb, kb, vb, gb, rb, gn, state_gla[e].reshape(ns, QB_W, DV_B),
                                 ns=ns, L=L)
            y = jnp.concatenate([oa, ob], axis=-1)
            xs = _proj_resid(y, xs.reshape(L * ns, D_MODEL), w_o, gpost).reshape(1, L * ns, D_MODEL)
            ks_s.append(nk.reshape(ns, WINDOW, HKV_A, HD_A))
            vs_s.append(nv.reshape(ns, WINDOW, HKV_A, HD_A))
            gs_s.append(s1.reshape(ns, H_B, DK_B, DV_B))
        else:
            o = l // 2
            w_in = w_in_odd[o].astype(BF16)
            w_o = w_out_odd[o].astype(BF16)
            y, st = _odd_in(xp, jnp.zeros((nb, past_p, D_MODEL), F32), gpre, w_in, conv_w_odd[o],
                            tm=tm_p, rs=1)
            xp = _proj_resid(y.reshape(nb * S, D_MODEL), xp.reshape(nb * S, D_MODEL), w_o,
                             gpost).reshape(nb, S, D_MODEL)
            cs_p.append(st[:, past_p - 2:, :])
            past = state_conv[o].transpose(1, 0, 2).reshape(1, past_s, D_MODEL)
            y, st = _odd_in(xs, past, gpre, w_in, conv_w_odd[o], tm=L * ns, rs=ns)
            xs = _proj_resid(y.reshape(L * ns, D_MODEL), xs.reshape(L * ns, D_MODEL), w_o,
                             gpost).reshape(1, L * ns, D_MODEL)
            cs_s.append(st.reshape(2, ns, D_MODEL).transpose(1, 0, 2))

        gpre, gpost = _row(norm_ffn_pre[l]), _row(norm_ffn_post[l])
        w_up = ffn_up[l].astype(BF16)
        w_dn = ffn_down[l].astype(BF16)
        cb = _row(ffn_conv_b[l])
        xp, st = _ffn(xp, jnp.zeros((nb, past_p, F2), F32), gpre, w_up, ffn_conv_w[l], cb, w_dn,
                      gpost, tm=tm_p, rs=1)
        fs_p.append(st[:, past_p - 2:, :])
        past = state_ffn[l].transpose(1, 0, 2).reshape(1, past_s, F2)
        xs, st = _ffn(xs, past, gpre, w_up, ffn_conv_w[l], cb, w_dn, gpost, tm=L * ns, rs=ns)
        fs_s.append(st.reshape(2, ns, F2).transpose(1, 0, 2))

    y_sample = xs.reshape(L, ns, D_MODEL).transpose(1, 0, 2)
    return (xp, y_sample, jnp.stack(ks_p), jnp.stack(vs_p), jnp.stack(gs_p), jnp.stack(cs_p),
            jnp.stack(fs_p), jnp.stack(ks_s), jnp.stack(vs_s), jnp.stack(gs_s), jnp.stack(cs_s),
            jnp.stack(fs_s))
```

```python
import functools

import numpy as np
import jax
import jax.numpy as jnp
from jax import lax
from jax.experimental import pallas as pl
from jax.experimental.pallas import tpu as pltpu

F32 = jnp.float32
BF16 = jnp.bfloat16

D_MODEL = 1024
WINDOW = 128
HD_A = 64
HQ_A = 8
HKV_A = 2
G_A = HQ_A // HKV_A
H_B = 4
DK_B = 64
DV_B = 128
GATE_RANK = 16
GATE_TAU = 16.0
GLA_CHUNK = 64
D_FF = ((8 * D_MODEL // 3 + 127) // 128) * 128
F2 = 2 * D_FF
EPS = 1e-6

QA_W = HQ_A * HD_A
KV_W = 2 * HKV_A * HD_A
QB_W = H_B * DK_B
VB_W = H_B * DV_B
MAIN_W = QA_W + KV_W + 2 * QB_W + 2 * VB_W

LANES = 128
SUBLANES = 8
VMEM_LIMIT = 56 * 1024 * 1024

TM_TOK = 512
TM_GLA = 256
TQ_SWA = 512
GLA_GROUP = 128
GLA_SUB = 8
FFN_TF = 256
SEQ_BLOCK = 16


def _cparams(*sem):
    return pltpu.CompilerParams(dimension_semantics=sem, vmem_limit_bytes=VMEM_LIMIT)


def _rms(x, g):
    return x * lax.rsqrt(jnp.mean(x * x, axis=-1, keepdims=True) + EPS) * g


def _dot(a, b):
    return jnp.dot(a, b, preferred_element_type=F32)


def _dot_nt(a, b):
    return lax.dot_general(a, b, (((1,), (1,)), ((), ())), preferred_element_type=F32)


def _dot_tn(a, b):
    return lax.dot_general(a, b, (((0,), (0,)), ((), ())), preferred_element_type=F32)


def _split3(x):
    hi = x.astype(BF16)
    r1 = x - hi.astype(F32)
    mid = r1.astype(BF16)
    lo = (r1 - mid.astype(F32)).astype(BF16)
    return hi, mid, lo


def _inproj_even_kernel(x_ref, gpre_ref, w_ref, wg_ref, wgu_ref, bg_ref,
                        qa_ref, kv_ref, qb_ref, kb_ref, vb_ref, rb_ref, gb_ref):
    h = _rms(x_ref[...], gpre_ref[...]).astype(BF16)

    def mm(lo, width):
        return _dot(h, w_ref[:, lo:lo + width])

    lo = 0
    qa_ref[...] = (mm(lo, QA_W) * (HD_A ** -0.5)).astype(BF16)
    lo += QA_W
    kv_ref[...] = mm(lo, KV_W)
    lo += KV_W
    qb_ref[...] = mm(lo, QB_W) * (DK_B ** -0.5)
    lo += QB_W
    kb_ref[...] = mm(lo, QB_W)
    lo += QB_W
    vb_ref[...] = mm(lo, VB_W).astype(BF16)
    lo += VB_W
    rb_ref[...] = mm(lo, VB_W)
    glr = _dot(h, wg_ref[...]).astype(BF16)
    z = _dot(glr, wgu_ref[...]) + bg_ref[...]
    gb_ref[...] = jax.nn.log_sigmoid(z) * (1.0 / GATE_TAU)


def _inproj_even(x2d, gpre, w_main, w_g, w_gu, b_g):
    T = x2d.shape[0]
    tm = min(TM_TOK, T)
    row = lambda w: pl.BlockSpec((tm, w), lambda i: (i, 0))
    full = lambda a: pl.BlockSpec(a.shape, lambda i: (0,) * a.ndim)
    out_shape = (
        jax.ShapeDtypeStruct((T, QA_W), BF16), jax.ShapeDtypeStruct((T, KV_W), F32),
        jax.ShapeDtypeStruct((T, QB_W), F32), jax.ShapeDtypeStruct((T, QB_W), F32),
        jax.ShapeDtypeStruct((T, VB_W), BF16), jax.ShapeDtypeStruct((T, VB_W), F32),
        jax.ShapeDtypeStruct((T, QB_W), F32))
    return pl.pallas_call(
        _inproj_even_kernel, grid=(T // tm,),
        in_specs=[row(D_MODEL), full(gpre), full(w_main), full(w_g), full(w_gu), full(b_g)],
        out_specs=(row(QA_W), row(KV_W), row(QB_W), row(QB_W), row(VB_W), row(VB_W), row(QB_W)),
        out_shape=out_shape, compiler_params=_cparams("parallel"), name="inproj_even",
    )(x2d, gpre, w_main, w_g, w_gu, b_g)


def _proj_resid_kernel(y_ref, x_ref, w_ref, g_ref, o_ref):
    o_ref[...] = x_ref[...] + _rms(_dot(y_ref[...], w_ref[...]), g_ref[...])


def _proj_resid(y2d, x2d, w, gpost):
    T = x2d.shape[0]
    tm = min(TM_TOK, T)
    row = lambda w_: pl.BlockSpec((tm, w_), lambda i: (i, 0))
    full = lambda a: pl.BlockSpec(a.shape, lambda i: (0,) * a.ndim)
    return pl.pallas_call(
        _proj_resid_kernel, grid=(T // tm,),
        in_specs=[row(y2d.shape[1]), row(D_MODEL), full(w), full(gpost)],
        out_specs=row(D_MODEL), out_shape=jax.ShapeDtypeStruct((T, D_MODEL), F32),
        compiler_params=_cparams("parallel"), name="proj_resid",
    )(y2d, x2d, w, gpost)


def _conv3_from_buf(buf, cw, cur, tm, past, rs):
    y = cw[0:1, :] * buf[past - 2 * rs:past - 2 * rs + tm, :]
    y = y + cw[1:2, :] * buf[past - rs:past - rs + tm, :]
    return y + cw[2:3, :] * cur


def _odd_in_kernel(x_ref, past_ref, gpre_ref, win_ref, cw_ref, y_ref, st_ref, carry, buf,
                   *, tm, past, rs, tc):
    @pl.when(pl.program_id(1) == 0)
    def _():
        carry[...] = past_ref[0]

    h = _rms(x_ref[0], gpre_ref[...]).astype(BF16)
    for c in range(D_MODEL // tc):
        lo = c * tc
        bg = _dot(h, win_ref[:, lo:lo + tc])
        cu = _dot(h, win_ref[:, D_MODEL + lo:D_MODEL + lo + tc]) * \
            _dot(h, win_ref[:, 2 * D_MODEL + lo:2 * D_MODEL + lo + tc])
        buf[0:past, :] = carry[:, lo:lo + tc]
        buf[past:past + tm, :] = cu
        carry[:, lo:lo + tc] = buf[tm:tm + past, :]
        z = _conv3_from_buf(buf, cw_ref[:, lo:lo + tc], cu, tm, past, rs)
        y_ref[0, :, lo:lo + tc] = (bg * z).astype(BF16)
    st_ref[0] = carry[...]


def _odd_in(x3d, past0, gpre, w_in, conv_w, *, tm, rs):
    nb, rows, _ = x3d.shape
    past = past0.shape[1]
    tc = 512
    kern = functools.partial(_odd_in_kernel, tm=tm, past=past, rs=rs, tc=tc)
    full = lambda a: pl.BlockSpec(a.shape, lambda b, j: (0,) * a.ndim)
    return pl.pallas_call(
        kern, grid=(nb, rows // tm),
        in_specs=[pl.BlockSpec((1, tm, D_MODEL), lambda b, j: (b, j, 0)),
                  pl.BlockSpec((1, past, D_MODEL), lambda b, j: (b, 0, 0)),
                  full(gpre), full(w_in), full(conv_w)],
        out_specs=(pl.BlockSpec((1, tm, D_MODEL), lambda b, j: (b, j, 0)),
                   pl.BlockSpec((1, past, D_MODEL), lambda b, j: (b, 0, 0))),
        out_shape=(jax.ShapeDtypeStruct((nb, rows, D_MODEL), BF16),
                   jax.ShapeDtypeStruct((nb, past, D_MODEL), F32)),
        scratch_shapes=[pltpu.VMEM((past, D_MODEL), F32), pltpu.VMEM((past + tm, tc), F32)],
        compiler_params=_cparams("parallel", "arbitrary"), name="odd_in",
    )(x3d, past0, gpre, w_in, conv_w)


def _ffn_kernel(x_ref, past_ref, gpre_ref, wup_ref, cw_ref, cb_ref, wdn_ref, gpost_ref,
                o_ref, st_ref, carry, buf_g, buf_v, acc, h_scr, act, *, tm, past, rs, tf):
    @pl.when(pl.program_id(1) == 0)
    def _():
        carry[...] = past_ref[0]

    h_scr[...] = _rms(x_ref[0], gpre_ref[...]).astype(BF16)
    nf = D_FF // tf

    def up(c):
        for lo, buf in ((c * tf, buf_g), (D_FF + c * tf, buf_v)):
            buf[c % 2, 0:past, :] = carry[:, lo:lo + tf]
            buf[c % 2, past:past + tm, :] = _dot(h_scr[...], wup_ref[:, lo:lo + tf])
            carry[:, lo:lo + tf] = buf[c % 2, tm:tm + past, :]

    def elem(c):
        halves = []
        for lo, buf in ((c * tf, buf_g), (D_FF + c * tf, buf_v)):
            b = buf.at[c % 2]
            halves.append(_conv3_from_buf(b, cw_ref[:, lo:lo + tf], b[past:past + tm, :],
                                          tm, past, rs) + cb_ref[:, lo:lo + tf])
        act[c % 2] = (jax.nn.gelu(halves[0], approximate=True) * halves[1]).astype(BF16)

    def down(c):
        part = _dot(act[c % 2], wdn_ref[c * tf:(c + 1) * tf, :])
        if c == 0:
            acc[...] = part
        else:
            acc[...] += part

    for c in range(nf + 2):
        if c < nf:
            up(c)
        if 1 <= c <= nf:
            elem(c - 1)
        if c >= 2:
            down(c - 2)
    o_ref[0] = x_ref[0] + _rms(acc[...], gpost_ref[...])
    st_ref[0] = carry[...]


def _ffn(x3d, past0, gpre, w_up, conv_w, conv_b, w_dn, gpost, *, tm, rs):
    nb, rows, _ = x3d.shape
    past = past0.shape[1]
    kern = functools.partial(_ffn_kernel, tm=tm, past=past, rs=rs, tf=FFN_TF)
    full = lambda a: pl.BlockSpec(a.shape, lambda b, j: (0,) * a.ndim)
    return pl.pallas_call(
        kern, grid=(nb, rows // tm),
        in_specs=[pl.BlockSpec((1, tm, D_MODEL), lambda b, j: (b, j, 0)),
                  pl.BlockSpec((1, past, F2), lambda b, j: (b, 0, 0)),
                  full(gpre), full(w_up), full(conv_w), full(conv_b), full(w_dn), full(gpost)],
        out_specs=(pl.BlockSpec((1, tm, D_MODEL), lambda b, j: (b, j, 0)),
                   pl.BlockSpec((1, past, F2), lambda b, j: (b, 0, 0))),
        out_shape=(jax.ShapeDtypeStruct((nb, rows, D_MODEL), F32),
                   jax.ShapeDtypeStruct((nb, past, F2), F32)),
        scratch_shapes=[pltpu.VMEM((past, F2), F32), pltpu.VMEM((2, past + tm, FFN_TF), F32),
                        pltpu.VMEM((2, past + tm, FFN_TF), F32), pltpu.VMEM((tm, D_MODEL), F32),
                        pltpu.VMEM((tm, D_MODEL), BF16), pltpu.VMEM((2, tm, FFN_TF), BF16)],
        compiler_params=_cparams("parallel", "arbitrary"), name="conv_ffn",
    )(x3d, past0, gpre, w_up, conv_w, conv_b, w_dn, gpost)


def _swa_bias_prompt():
    qi = np.arange(WINDOW)[:, None]
    sj = np.arange(2 * WINDOW)[None, :]
    dist = WINDOW + qi - sj
    valid = (dist >= 0) & (dist <= WINDOW)
    slopes = 2.0 ** (-8.0 * np.arange(1, HQ_A + 1) / HQ_A)
    bias = np.where(valid[None], -slopes[:, None, None] * dist[None].astype(np.float64), -np.inf)
    return jnp.asarray(bias, dtype=F32)


def _swa_prompt_kernel(sink_ref, q_ref, kvc_ref, kvp_ref, bias_ref, o_ref, *, tq):
    i = pl.program_id(1)
    kvc = kvc_ref[0]
    kvp = kvp_ref[0]
    kcat = jnp.concatenate([kvp[:, 0:LANES], kvc[:, 0:LANES]], axis=0).astype(BF16)
    vcat = jnp.concatenate([kvp[:, LANES:], kvc[:, LANES:]], axis=0).astype(BF16)
    lane = lax.broadcasted_iota(jnp.int32, (WINDOW, LANES), 1)
    col = lax.broadcasted_iota(jnp.int32, (WINDOW, 2 * WINDOW), 1)
    for j in range(tq // WINDOW):
        keys = kcat[j * WINDOW:(j + 2) * WINDOW, :]
        vals = vcat[j * WINDOW:(j + 2) * WINDOW, :]
        for g in range(G_A):
            q2 = q_ref[0, j * WINDOW:(j + 1) * WINDOW, g * LANES:(g + 1) * LANES]
            outs = []
            for kv in range(HKV_A):
                hq = kv * G_A + g
                in_head = (lane < HD_A) if kv == 0 else (lane >= HD_A)
                qm = jnp.where(in_head, q2, jnp.zeros_like(q2))
                s = _dot_nt(qm, keys) + bias_ref[hq]
                if j == 0:
                    s = jnp.where(jnp.logical_and(i == 0, col < WINDOW), -jnp.inf, s)
                sink = sink_ref[hq]
                m = jnp.maximum(jnp.max(s, axis=-1, keepdims=True), sink)
                p = jnp.exp(s - m)
                den = jnp.sum(p, axis=-1, keepdims=True) + jnp.exp(sink - m)
                outs.append(_dot(p.astype(BF16), vals) / den)
            o2 = jnp.where(lane < HD_A, outs[0], outs[1])
            o_ref[0, j * WINDOW:(j + 1) * WINDOW, g * LANES:(g + 1) * LANES] = o2.astype(BF16)


def _swa_prompt(q3d, kv3d, sinks):
    nb, S, _ = q3d.shape
    tq = min(TQ_SWA, S)
    bias = _swa_bias_prompt()
    kern = functools.partial(_swa_prompt_kernel, tq=tq)
    blocks_per_tile = tq // WINDOW
    return pl.pallas_call(
        kern, grid=(nb, S // tq),
        in_specs=[pl.BlockSpec(memory_space=pltpu.SMEM),
                  pl.BlockSpec((1, tq, QA_W), lambda b, i: (b, i, 0)),
                  pl.BlockSpec((1, tq, KV_W), lambda b, i: (b, i, 0)),
                  pl.BlockSpec((1, WINDOW, KV_W),
                               lambda b, i: (b, jnp.maximum(i * blocks_per_tile - 1, 0), 0)),
                  pl.BlockSpec(bias.shape, lambda b, i: (0, 0, 0))],
        out_specs=pl.BlockSpec((1, tq, QA_W), lambda b, i: (b, i, 0)),
        out_shape=jax.ShapeDtypeStruct((nb, S, QA_W), BF16),
        compiler_params=_cparams("parallel", "arbitrary"), name="swa_prompt",
    )(sinks, q3d, kv3d, kv3d, bias)


def _gla_constants():
    r = np.arange(GLA_GROUP)
    tri = ((r[:, None] >= r[None, :]) & (r[:, None] // GLA_CHUNK == r[None, :] // GLA_CHUNK))
    ones_bd = (np.arange(QB_W)[:, None] // DK_B) == (np.arange(VB_W)[None, :] // DV_B)
    masks = []
    for s in (8, 16, 32):
        i, j = r[:, None], r[None, :]
        masks.append((i // (2 * s) == j // (2 * s)) & (i % (2 * s) >= s) & (j % (2 * s) < s))
    return (jnp.asarray(tri, dtype=BF16), jnp.asarray(ones_bd, dtype=BF16),
            jnp.asarray(np.stack(masks), dtype=F32))


def _bcast_rows(x, period, row):
    t, c = x.shape
    x3 = x.reshape(t // period, period, c)
    return jnp.broadcast_to(x3[:, row:row + 1, :], (t // period, period, c)).reshape(t, c)


def _head_norm_gate(o, r, gn):
    parts = []
    for h in range(H_B):
        oh = o[:, h * DV_B:(h + 1) * DV_B]
        parts.append(oh * lax.rsqrt(jnp.mean(oh * oh, axis=-1, keepdims=True) + EPS))
    return jnp.concatenate(parts, axis=-1) * gn * jax.nn.silu(r)


def _gla_prompt_kernel(q_ref, k_ref, v_ref, g_ref, r_ref, gn_ref, tri_ref, ones_ref, lvl_ref,
                       o_ref, st_ref, state, b_scr, o_scr, *, tm):
    @pl.when(pl.program_id(1) == 0)
    def _():
        state[...] = jnp.zeros_like(state)

    q = q_ref[0]
    k = k_ref[0]
    v = v_ref[0]
    lane = lax.broadcasted_iota(jnp.int32, (1, LANES), 1)
    in_even = lane < DK_B

    ghi, gmid, glo = _split3(g_ref[0])
    tri = tri_ref[...]
    for gi in range(tm // GLA_GROUP):
        rows = slice(gi * GLA_GROUP, (gi + 1) * GLA_GROUP)
        b_scr[rows, :] = _dot(tri, ghi[rows]) + _dot(tri, gmid[rows]) + _dot(tri, glo[rows])
    b = b_scr[...]

    vf = v.astype(F32)
    sub = lax.broadcasted_iota(jnp.int32, (tm, 1), 0) % GLA_SUB
    odiag = jnp.zeros((tm, VB_W), F32)
    for jj in range(GLA_SUB):
        kj = _bcast_rows(k, GLA_SUB, jj)
        bj = _bcast_rows(b, GLA_SUB, jj)
        vj = _bcast_rows(vf, GLA_SUB, jj)
        p = (q * kj * jnp.exp(jnp.minimum(b - bj, 0.0))).astype(BF16)
        a = _dot(p, ones_ref[...])
        odiag = odiag + jnp.where(sub >= jj, a, 0.0) * vj
    o_scr[...] = odiag

    for gi in range(tm // GLA_GROUP):
        rows = slice(gi * GLA_GROUP, (gi + 1) * GLA_GROUP)
        qg, kg, bgrp = q[rows], k[rows], b[rows]
        a_heads = [jnp.zeros((GLA_GROUP, GLA_GROUP), F32) for _ in range(H_B)]
        for li, s in enumerate((8, 16, 32)):
            ref_b = _bcast_rows(bgrp, 2 * s, s - 1)
            qe = (qg * jnp.exp(jnp.minimum(bgrp - ref_b, 0.0))).astype(BF16)
            ke = (kg * jnp.exp(jnp.minimum(ref_b - bgrp, 0.0))).astype(BF16)
            keep = lvl_ref[li] > 0.5
            for h in range(H_B):
                pr = slice((h // 2) * LANES, (h // 2 + 1) * LANES)
                sel = in_even if h % 2 == 0 else jnp.logical_not(in_even)
                kem = jnp.where(sel, ke[:, pr], jnp.zeros_like(ke[:, pr]))
                a_heads[h] = a_heads[h] + jnp.where(keep, _dot_nt(qe[:, pr], kem), 0.0)
        for h in range(H_B):
            cols = slice(h * DV_B, (h + 1) * DV_B)
            o_scr[rows, cols] += _dot(a_heads[h].astype(BF16), v[rows, cols])

    for c in range(tm // GLA_CHUNK):
        rows = slice(c * GLA_CHUNK, (c + 1) * GLA_CHUNK)
        bc = b[rows]
        b_end = bc[GLA_CHUNK - 1:GLA_CHUNK, :]
        qe = (q[rows] * jnp.exp(bc)).astype(BF16)
        ke = (k[rows] * jnp.exp(b_end - bc)).astype(BF16)
        st = state[...]
        st_bf = st.astype(BF16)
        upd = []
        for pi in range(H_B // 2):
            pr = slice(pi * LANES, (pi + 1) * LANES)
            acc = jnp.zeros((DV_B, LANES), F32)
            for h in (2 * pi, 2 * pi + 1):
                sel = in_even if h % 2 == 0 else jnp.logical_not(in_even)
                cols = slice(h * DV_B, (h + 1) * DV_B)
                qm = jnp.where(sel, qe[:, pr], jnp.zeros_like(qe[:, pr]))
                o_scr[rows, cols] += _dot_nt(qm, st_bf[:, pr])
                kem = jnp.where(sel, ke[:, pr], jnp.zeros_like(ke[:, pr]))
                acc = acc + _dot_tn(v[rows, cols], kem)
            upd.append(acc)
        state[...] = st * jnp.exp(b_end) + jnp.concatenate(upd, axis=-1)

    o_ref[0] = _head_norm_gate(o_scr[...], r_ref[0], gn_ref[...]).astype(BF16)
    st_ref[0] = state[...]


def _gla_prompt(q3d, k3d, v3d, g3d, r3d, gn):
    nb, S, _ = q3d.shape
    tm = min(TM_GLA, S)
    tri, ones_bd, lvl = _gla_constants()
    kern = functools.partial(_gla_prompt_kernel, tm=tm)
    tok = lambda w: pl.BlockSpec((1, tm, w), lambda b, j: (b, j, 0))
    full = lambda a: pl.BlockSpec(a.shape, lambda b, j: (0,) * a.ndim)
    return pl.pallas_call(
        kern, grid=(nb, S // tm),
        in_specs=[tok(QB_W), tok(QB_W), tok(VB_W), tok(QB_W), tok(VB_W),
                  full(gn), full(tri), full(ones_bd), full(lvl)],
        out_specs=(tok(VB_W), pl.BlockSpec((1, DV_B, QB_W), lambda b, j: (b, 0, 0))),
        out_shape=(jax.ShapeDtypeStruct((nb, S, VB_W), BF16),
                   jax.ShapeDtypeStruct((nb, DV_B, QB_W), F32)),
        scratch_shapes=[pltpu.VMEM((DV_B, QB_W), F32), pltpu.VMEM((tm, QB_W), F32),
                        pltpu.VMEM((tm, VB_W), F32)],
        compiler_params=_cparams("parallel", "arbitrary"), name="gla_prompt",
    )(q3d, k3d, v3d, g3d, r3d, gn, tri, ones_bd, lvl)


def _swa_bias_decode(L):
    rows = HQ_A * L
    kpad = WINDOW + SUBLANES
    hq = np.arange(rows)[:, None] // L
    t = np.arange(rows)[:, None] % L
    s = np.arange(kpad)[None, :]
    dist = WINDOW + t - s
    valid = (dist >= 0) & (dist <= WINDOW) & (s < WINDOW + L)
    slopes = 2.0 ** (-8.0 * (hq + 1) / HQ_A)
    return jnp.asarray(np.where(valid, -slopes * dist, -np.inf), dtype=F32)


def _swa_decode_kernel(q_ref, kvn_ref, ck_ref, cv_ref, bias_ref, sink_ref,
                       o_ref, nk_ref, nv_ref, qs, os_, kc, vc, kn, vn, *, ns, L, sb):
    i = pl.program_id(0)
    rows_q = HQ_A * L
    lane = lax.broadcasted_iota(jnp.int32, (1, LANES), 1)

    @pl.when(i == 0)
    def _():
        q = q_ref[...].astype(F32)
        for kv in range(HKV_A):
            sel = (lane < HD_A) if kv == 0 else (lane >= HD_A)
            for g in range(G_A):
                base = (kv * G_A + g) * L * ns
                qs[base:base + L * ns, :] = jnp.where(sel, q[:, g * LANES:(g + 1) * LANES], 0.0)
        kc[...] = jnp.zeros_like(kc)
        vc[...] = jnp.zeros_like(vc)
        kn[...] = kvn_ref[:, 0:LANES]
        vn[...] = kvn_ref[:, LANES:]

    bias = bias_ref[...]
    sink = sink_ref[...]

    def body(s, carry):
        seq = i * sb + s
        kc[0:WINDOW, :] = ck_ref[s]
        vc[0:WINDOW, :] = cv_ref[s]
        kc[WINDOW:WINDOW + L, :] = kn[pl.ds(seq, L, stride=ns), :]
        vc[WINDOW:WINDOW + L, :] = vn[pl.ds(seq, L, stride=ns), :]
        nk_ref[s] = kc[L:L + WINDOW, :]
        nv_ref[s] = vc[L:L + WINDOW, :]
        lhs = qs[pl.ds(seq, rows_q, stride=ns), :].astype(BF16)
        sc = _dot_nt(lhs, kc[...].astype(BF16)) + bias
        m = jnp.maximum(jnp.max(sc, axis=-1, keepdims=True), sink)
        p = jnp.exp(sc - m)
        den = jnp.sum(p, axis=-1, keepdims=True) + jnp.exp(sink - m)
        res = _dot(p.astype(BF16), vc[...].astype(BF16)) / den
        half = rows_q // 2
        os_[pl.ds(seq, half, stride=ns), :] = jnp.where(lane < HD_A, res[0:half], res[half:])
        return carry

    lax.fori_loop(0, sb, body, 0)

    @pl.when(i == pl.num_programs(0) - 1)
    def _():
        for g in range(G_A):
            o_ref[:, g * LANES:(g + 1) * LANES] = os_[g * L * ns:(g + 1) * L * ns, :].astype(BF16)


def _swa_decode(q_tm, kvn_tm, cache_k, cache_v, sinks, *, ns, L):
    sb = SEQ_BLOCK
    bias = _swa_bias_decode(L)
    sink_col = jnp.broadcast_to(jnp.repeat(sinks.astype(F32), L)[:, None], (HQ_A * L, 1))
    kern = functools.partial(_swa_decode_kernel, ns=ns, L=L, sb=sb)
    full = lambda a: pl.BlockSpec(a.shape, lambda i: (0,) * a.ndim)
    cache = pl.BlockSpec((sb, WINDOW, LANES), lambda i: (i, 0, 0))
    kpad = WINDOW + SUBLANES
    return pl.pallas_call(
        kern, grid=(ns // sb,),
        in_specs=[full(q_tm), full(kvn_tm), cache, cache, full(bias), full(sink_col)],
        out_specs=(pl.BlockSpec((L * ns, QA_W), lambda i: (0, 0)), cache, cache),
        out_shape=(jax.ShapeDtypeStruct((L * ns, QA_W), BF16),
                   jax.ShapeDtypeStruct(cache_k.shape, F32),
                   jax.ShapeDtypeStruct(cache_v.shape, F32)),
        scratch_shapes=[pltpu.VMEM((HQ_A * L * ns, LANES), F32),
                        pltpu.VMEM((G_A * L * ns, LANES), F32),
                        pltpu.VMEM((kpad, LANES), F32), pltpu.VMEM((kpad, LANES), F32),
                        pltpu.VMEM((L * ns, LANES), F32), pltpu.VMEM((L * ns, LANES), F32)],
        compiler_params=_cparams("arbitrary"), name="swa_decode",
    )(q_tm, kvn_tm, cache_k, cache_v, bias, sink_col)


def _gla_decode_kernel(q_ref, k_ref, v_ref, g_ref, r_ref, gn_ref, ones_ref, s0_ref,
                       o_ref, s1_ref, qe2, ke2, v2, dec3, oi2, od, *, ns, L, sb):
    i = pl.program_id(0)
    lane = lax.broadcasted_iota(jnp.int32, (1, LANES), 1)
    npair = H_B // 2

    @pl.when(i == 0)
    def _():
        slab = lambda a, t: a[t * ns:(t + 1) * ns, :]
        q, k, g = q_ref[...], k_ref[...], g_ref[...]
        vf = v_ref[...].astype(F32)
        b = [slab(g, 0)]
        for t in range(1, L):
            b.append(b[-1] + slab(g, t))
        for t in range(L):
            acc = jnp.zeros((ns, VB_W), F32)
            for jj in range(t + 1):
                p = (slab(q, t) * slab(k, jj) * jnp.exp(b[t] - b[jj])).astype(BF16)
                acc = acc + _dot(p, ones_ref[...]) * slab(vf, jj)
            od[t * ns:(t + 1) * ns, :] = acc
            qe = slab(q, t) * jnp.exp(b[t])
            ke = slab(k, t) * jnp.exp(b[L - 1] - b[t])
            for par in range(2):
                sel = (lane < DK_B) if par == 0 else (lane >= DK_B)
                base = (par * L + t) * ns
                for pi in range(npair):
                    pr = slice(pi * LANES, (pi + 1) * LANES)
                    qe2[pi, base:base + ns, :] = jnp.where(sel, qe[:, pr], 0.0)
                    ke2[pi, base:base + ns, :] = jnp.where(sel, ke[:, pr], 0.0)
                    h = 2 * pi + par
                    v2[pi, base:base + ns, :] = slab(vf, t)[:, h * DV_B:(h + 1) * DV_B]
        dec3[...] = jnp.zeros_like(dec3)
        hi, mid, lo = _split3(jnp.exp(b[L - 1]))
        for pi in range(npair):
            pr = slice(pi * LANES, (pi + 1) * LANES)
            dec3[pi, 0:ns, :] = hi[:, pr].astype(F32)
            dec3[pi, ns:2 * ns, :] = mid[:, pr].astype(F32)
            dec3[pi, 2 * ns:3 * ns, :] = lo[:, pr].astype(F32)

    ones8 = jnp.ones((2 * L, LANES), BF16)

    def body(s, carry):
        seq = i * sb + s
        st = s0_ref[s]
        for pi in range(npair):
            take = lambda ref: ref[pi, pl.ds(seq, 2 * L, stride=ns), :].astype(BF16)
            st_p = st[pi * LANES:(pi + 1) * LANES, :]
            oi2[pi, pl.ds(seq, 2 * L, stride=ns), :] = _dot(take(qe2), st_p.astype(BF16))
            dcol = _dot_tn(take(dec3), ones8)
            upd = _dot_tn(take(ke2), take(v2))
            s1_ref[s, pi * LANES:(pi + 1) * LANES, :] = dcol * st_p + upd
        return carry

    lax.fori_loop(0, sb, body, 0)

    @pl.when(i == pl.num_programs(0) - 1)
    def _():
        for t in range(L):
            parts = []
            for h in range(H_B):
                base = ((h % 2) * L + t) * ns
                parts.append(oi2[h // 2, base:base + ns, :])
            o = jnp.concatenate(parts, axis=-1) + od[t * ns:(t + 1) * ns, :]
            o_ref[t * ns:(t + 1) * ns, :] = _head_norm_gate(
                o, r_ref[t * ns:(t + 1) * ns, :], gn_ref[...]).astype(BF16)


def _gla_decode(q_tm, k_tm, v_tm, g_tm, r_tm, gn, s0, *, ns, L):
    sb = SEQ_BLOCK
    assert 2 * L == SUBLANES
    _, ones_bd, _ = _gla_constants()
    kern = functools.partial(_gla_decode_kernel, ns=ns, L=L, sb=sb)
    full = lambda a: pl.BlockSpec(a.shape, lambda i: (0,) * a.ndim)
    st = pl.BlockSpec((sb, QB_W, DV_B), lambda i: (i, 0, 0))
    rows2 = 2 * L * ns
    return pl.pallas_call(
        kern, grid=(ns // sb,),
        in_specs=[full(q_tm), full(k_tm), full(v_tm), full(g_tm), full(r_tm), full(gn),
                  full(ones_bd), st],
        out_specs=(pl.BlockSpec((L * ns, VB_W), lambda i: (0, 0)), st),
        out_shape=(jax.ShapeDtypeStruct((L * ns, VB_W), BF16),
                   jax.ShapeDtypeStruct(s0.shape, F32)),
        scratch_shapes=[pltpu.VMEM((H_B // 2, rows2, LANES), F32) for _ in range(5)]
        + [pltpu.VMEM((L * ns, VB_W), F32)],
        compiler_params=_cparams("arbitrary"), name="gla_decode",
    )(q_tm, k_tm, v_tm, g_tm, r_tm, gn, ones_bd, s0)


def _qa_perm():
    return np.asarray([(kv * G_A + g) * HD_A + d
                       for g in range(G_A) for kv in range(HKV_A) for d in range(HD_A)])


def _prep_even(w_in, w_gate_up, b_gate, w_out):
    perm = _qa_perm()
    w_main = jnp.concatenate([w_in[:, :QA_W][:, perm], w_in[:, QA_W:MAIN_W]], axis=1).astype(BF16)
    w_g = jnp.pad(w_in[:, MAIN_W:], ((0, 0), (0, LANES - GATE_RANK))).astype(BF16)
    w_gu = jnp.pad(w_gate_up, ((0, LANES - GATE_RANK), (0, 0))).astype(BF16)
    w_o = jnp.concatenate([w_out[:QA_W][perm], w_out[QA_W:]], axis=0).astype(BF16)
    return w_main, w_g, w_gu, b_gate.reshape(1, -1), w_o


def _row(v):
    return v.reshape(1, -1)


def kernel(x_prompt, x_sample, cache_swa_k, cache_swa_v, state_gla, state_conv, state_ffn,
           norm_mix_pre, norm_mix_post, norm_ffn_pre, norm_ffn_post, w_in_even, w_gate_up, b_gate,
           attn_sinks, gla_norm, w_out_even, w_in_odd, conv_w_odd, w_out_odd, ffn_up, ffn_conv_w,
           ffn_conv_b, ffn_down):
    nb, S, _ = x_prompt.shape
    ns, L, _ = x_sample.shape
    depth = norm_mix_pre.shape[0]

    xp = x_prompt
    xs = x_sample.transpose(1, 0, 2).reshape(1, L * ns, D_MODEL)
    past_p = max(SUBLANES, 2)
    past_s = 2 * ns
    tm_p = min(TM_TOK, S)

    ks_p, vs_p, gs_p, cs_p, fs_p = [], [], [], [], []
    ks_s, vs_s, gs_s, cs_s, fs_s = [], [], [], [], []

    for l in range(depth):
        gpre, gpost = _row(norm_mix_pre[l]), _row(norm_mix_post[l])
        if l % 2 == 0:
            e = l // 2
            w_main, w_g, w_gu, b_g, w_o = _prep_even(w_in_even[e], w_gate_up[e], b_gate[e],
                                                     w_out_even[e])
            gn = _row(gla_norm[e])
            qa, kv, qb, kb, vb, rb, gb = _inproj_even(xp.reshape(nb * S, D_MODEL), gpre,
                                                      w_main, w_g, w_gu, b_g)
            r3 = lambda a: a.reshape(nb, S, a.shape[-1])
            oa = _swa_prompt(r3(qa), r3(kv), attn_sinks[e])
            ob, st_t = _gla_prompt(r3(qb), r3(kb), r3(vb), r3(gb), r3(rb), gn)
            y = jnp.concatenate([oa, ob], axis=-1).reshape(nb * S, QA_W + VB_W)
            xp = _proj_resid(y, xp.reshape(nb * S, D_MODEL), w_o, gpost).reshape(nb, S, D_MODEL)
            kv_last = r3(kv)[:, S - WINDOW:, :]
            ks_p.append(kv_last[..., :LANES].reshape(nb, WINDOW, HKV_A, HD_A))
            vs_p.append(kv_last[..., LANES:].reshape(nb, WINDOW, HKV_A, HD_A))
            gs_p.append(st_t.transpose(0, 2, 1).reshape(nb, H_B, DK_B, DV_B))
            qa, kv, qb, kb, vb, rb, gb = _inproj_even(xs.reshape(L * ns, D_MODEL), gpre,
                                                      w_main, w_g, w_gu, b_g)
            oa, nk, nv = _swa_decode(qa, kv, cache_swa_k[e].reshape(ns, WINDOW, LANES),
                                     cache_swa_v[e].reshape(ns, WINDOW, LANES), attn_sinks[e],
                                     ns=ns, L=L)
            ob, s1 = _gla_decode(qb, kb, vb, gb, rb, gn, state_gla[e].reshape(ns, QB_W, DV_B),
                                 ns=ns, L=L)
            y = jnp.concatenate([oa, ob], axis=-1)
            xs = _proj_resid(y, xs.reshape(L * ns, D_MODEL), w_o, gpost).reshape(1, L * ns, D_MODEL)
            ks_s.append(nk.reshape(ns, WINDOW, HKV_A, HD_A))
            vs_s.append(nv.reshape(ns, WINDOW, HKV_A, HD_A))
            gs_s.append(s1.reshape(ns, H_B, DK_B, DV_B))
        else:
            o = l // 2
            w_in = w_in_odd[o].astype(BF16)
            w_o = w_out_odd[o].astype(BF16)
            y, st = _odd_in(xp, jnp.zeros((nb, past_p, D_MODEL), F32), gpre, w_in, conv_w_odd[o],
                            tm=tm_p, rs=1)
            xp = _proj_resid(y.reshape(nb * S, D_MODEL), xp.reshape(nb * S, D_MODEL), w_o,
                             gpost).reshape(nb, S, D_MODEL)
            cs_p.append(st[:, past_p - 2:, :])
            past = state_conv[o].transpose(1, 0, 2).reshape(1, past_s, D_MODEL)
            y, st = _odd_in(xs, past, gpre, w_in, conv_w_odd[o], tm=L * ns, rs=ns)
            xs = _proj_resid(y.reshape(L * ns, D_MODEL), xs.reshape(L * ns, D_MODEL), w_o,
                             gpost).reshape(1, L * ns, D_MODEL)
            cs_s.append(st.reshape(2, ns, D_MODEL).transpose(1, 0, 2))

        gpre, gpost = _row(norm_ffn_pre[l]), _row(norm_ffn_post[l])
        w_up = ffn_up[l].astype(BF16)
        w_dn = ffn_down[l].astype(BF16)
        cb = _row(ffn_conv_b[l])
        xp, st = _ffn(xp, jnp.zeros((nb, past_p, F2), F32), gpre, w_up, ffn_conv_w[l], cb, w_dn,
                      gpost, tm=tm_p, rs=1)
        fs_p.append(st[:, past_p - 2:, :])
        past = state_ffn[l].transpose(1, 0, 2).reshape(1, past_s, F2)
        xs, st = _ffn(xs, past, gpre, w_up, ffn_conv_w[l], cb, w_dn, gpost, tm=L * ns, rs=ns)
        fs_s.append(st.reshape(2, ns, F2).transpose(1, 0, 2))

    y_sample = xs.reshape(L, ns, D_MODEL).transpose(1, 0, 2)
    return (xp, y_sample, jnp.stack(ks_p), jnp.stack(vs_p), jnp.stack(gs_p), jnp.stack(cs_p),
            jnp.stack(fs_p), jnp.stack(ks_s), jnp.stack(vs_s), jnp.stack(gs_s), jnp.stack(cs_s),
            jnp.stack(fs_s))
```

```python
import functools

import numpy as np
import jax
import jax.numpy as jnp
from jax import lax
from jax.experimental import pallas as pl
from jax.experimental.pallas import tpu as pltpu

F32 = jnp.float32
BF16 = jnp.bfloat16

D_MODEL = 1024
WINDOW = 128
HD_A = 64
HQ_A = 8
HKV_A = 2
G_A = HQ_A // HKV_A
H_B = 4
DK_B = 64
DV_B = 128
GATE_RANK = 16
GATE_TAU = 16.0
GLA_CHUNK = 64
D_FF = ((8 * D_MODEL // 3 + 127) // 128) * 128
F2 = 2 * D_FF
EPS = 1e-6
GELU_C = float(np.sqrt(2.0 / np.pi))
GELU_A = 0.044715

QA_W = HQ_A * HD_A
KV_W = 2 * HKV_A * HD_A
QB_W = H_B * DK_B
VB_W = H_B * DV_B
MAIN_W = QA_W + KV_W + 2 * QB_W + 2 * VB_W

LANES = 128
SUBLANES = 8
VMEM_LIMIT = 56 * 1024 * 1024

TM_TOK = 512
TM_GLA = 512
TQ_SWA = 512
GLA_GROUP = 128
FFN_TF = 256
FFN_ELEM_DTYPE = BF16
SEQ_BLOCK = 16


def _cparams(*sem):
    return pltpu.CompilerParams(dimension_semantics=sem, vmem_limit_bytes=VMEM_LIMIT)


def _rms(x, g):
    return x * lax.rsqrt(jnp.mean(x * x, axis=-1, keepdims=True) + EPS) * g


def _dot(a, b):
    return jnp.dot(a, b, preferred_element_type=F32)


def _dot_nt(a, b):
    return lax.dot_general(a, b, (((1,), (1,)), ((), ())), preferred_element_type=F32)


def _dot_tn(a, b):
    return lax.dot_general(a, b, (((0,), (0,)), ((), ())), preferred_element_type=F32)


def _split3(x):
    hi = x.astype(BF16)
    r1 = x - hi.astype(F32)
    mid = r1.astype(BF16)
    lo = (r1 - mid.astype(F32)).astype(BF16)
    return hi, mid, lo


def _inproj_even_kernel(x_ref, gpre_ref, w_ref, wg_ref, wgu_ref, bg_ref,
                        qa_ref, kv_ref, qb_ref, kb_ref, vb_ref, rb_ref, gb_ref):
    h = _rms(x_ref[...], gpre_ref[...]).astype(BF16)

    def mm(lo, width):
        return _dot(h, w_ref[:, lo:lo + width])

    lo = 0
    qa_ref[...] = (mm(lo, QA_W) * (HD_A ** -0.5)).astype(BF16)
    lo += QA_W
    kv_ref[...] = mm(lo, KV_W)
    lo += KV_W
    qb_ref[...] = mm(lo, QB_W) * (DK_B ** -0.5)
    lo += QB_W
    kb_ref[...] = mm(lo, QB_W)
    lo += QB_W
    vb_ref[...] = mm(lo, VB_W).astype(BF16)
    lo += VB_W
    rb_ref[...] = mm(lo, VB_W)
    glr = _dot(h, wg_ref[...]).astype(BF16)
    z = _dot(glr, wgu_ref[...]) + bg_ref[...]
    gb_ref[...] = jax.nn.log_sigmoid(z) * (1.0 / GATE_TAU)


def _inproj_even(x2d, gpre, w_main, w_g, w_gu, b_g):
    T = x2d.shape[0]
    tm = min(TM_TOK, T)
    row = lambda w: pl.BlockSpec((tm, w), lambda i: (i, 0))
    full = lambda a: pl.BlockSpec(a.shape, lambda i: (0,) * a.ndim)
    out_shape = (
        jax.ShapeDtypeStruct((T, QA_W), BF16), jax.ShapeDtypeStruct((T, KV_W), F32),
        jax.ShapeDtypeStruct((T, QB_W), F32), jax.ShapeDtypeStruct((T, QB_W), F32),
        jax.ShapeDtypeStruct((T, VB_W), BF16), jax.ShapeDtypeStruct((T, VB_W), F32),
        jax.ShapeDtypeStruct((T, QB_W), F32))
    return pl.pallas_call(
        _inproj_even_kernel, grid=(T // tm,),
        in_specs=[row(D_MODEL), full(gpre), full(w_main), full(w_g), full(w_gu), full(b_g)],
        out_specs=(row(QA_W), row(KV_W), row(QB_W), row(QB_W), row(VB_W), row(VB_W), row(QB_W)),
        out_shape=out_shape, compiler_params=_cparams("parallel"), name="inproj_even",
    )(x2d, gpre, w_main, w_g, w_gu, b_g)


def _proj_resid_kernel(y_ref, x_ref, w_ref, g_ref, o_ref):
    o_ref[...] = x_ref[...] + _rms(_dot(y_ref[...], w_ref[...]), g_ref[...])


def _proj_resid(y2d, x2d, w, gpost):
    T = x2d.shape[0]
    tm = min(TM_TOK, T)
    row = lambda w_: pl.BlockSpec((tm, w_), lambda i: (i, 0))
    full = lambda a: pl.BlockSpec(a.shape, lambda i: (0,) * a.ndim)
    return pl.pallas_call(
        _proj_resid_kernel, grid=(T // tm,),
        in_specs=[row(y2d.shape[1]), row(D_MODEL), full(w), full(gpost)],
        out_specs=row(D_MODEL), out_shape=jax.ShapeDtypeStruct((T, D_MODEL), F32),
        compiler_params=_cparams("parallel"), name="proj_resid",
    )(y2d, x2d, w, gpost)


def _conv3_from_buf(buf, cw, cur, tm, past, rs, dtype=F32):
    cw = cw.astype(dtype)
    y = cw[0:1, :] * buf[past - 2 * rs:past - 2 * rs + tm, :].astype(dtype)
    y = y + cw[1:2, :] * buf[past - rs:past - rs + tm, :].astype(dtype)
    return y + cw[2:3, :] * cur.astype(dtype)


def _odd_in_kernel(x_ref, past_ref, gpre_ref, win_ref, cw_ref, y_ref, st_ref, carry, buf,
                   *, tm, past, rs, tc):
    @pl.when(pl.program_id(1) == 0)
    def _():
        carry[...] = past_ref[0]

    h = _rms(x_ref[0], gpre_ref[...]).astype(BF16)
    for c in range(D_MODEL // tc):
        lo = c * tc
        bg = _dot(h, win_ref[:, lo:lo + tc])
        cu = _dot(h, win_ref[:, D_MODEL + lo:D_MODEL + lo + tc]) * \
            _dot(h, win_ref[:, 2 * D_MODEL + lo:2 * D_MODEL + lo + tc])
        buf[0:past, :] = carry[:, lo:lo + tc]
        buf[past:past + tm, :] = cu
        carry[:, lo:lo + tc] = buf[tm:tm + past, :]
        z = _conv3_from_buf(buf, cw_ref[:, lo:lo + tc], cu, tm, past, rs)
        y_ref[0, :, lo:lo + tc] = (bg * z).astype(BF16)
    st_ref[0] = carry[...]


def _odd_in(x3d, past0, gpre, w_in, conv_w, *, tm, rs):
    nb, rows, _ = x3d.shape
    past = past0.shape[1]
    tc = 512
    kern = functools.partial(_odd_in_kernel, tm=tm, past=past, rs=rs, tc=tc)
    full = lambda a: pl.BlockSpec(a.shape, lambda b, j: (0,) * a.ndim)
    return pl.pallas_call(
        kern, grid=(nb, rows // tm),
        in_specs=[pl.BlockSpec((1, tm, D_MODEL), lambda b, j: (b, j, 0)),
                  pl.BlockSpec((1, past, D_MODEL), lambda b, j: (b, 0, 0)),
                  full(gpre), full(w_in), full(conv_w)],
        out_specs=(pl.BlockSpec((1, tm, D_MODEL), lambda b, j: (b, j, 0)),
                   pl.BlockSpec((1, past, D_MODEL), lambda b, j: (b, 0, 0))),
        out_shape=(jax.ShapeDtypeStruct((nb, rows, D_MODEL), BF16),
                   jax.ShapeDtypeStruct((nb, past, D_MODEL), F32)),
        scratch_shapes=[pltpu.VMEM((past, D_MODEL), F32), pltpu.VMEM((past + tm, tc), F32)],
        compiler_params=_cparams("parallel", "arbitrary"), name="odd_in",
    )(x3d, past0, gpre, w_in, conv_w)


def _ffn_kernel(x_ref, past_ref, gpre_ref, wup_ref, cw_ref, cb_ref, wdn_ref, gpost_ref, gk_ref,
                o_ref, st_ref, carry, buf_g, buf_v, acc, h_scr, act, *, tm, past, rs, tf):
    @pl.when(pl.program_id(1) == 0)
    def _():
        carry[...] = past_ref[0]

    h_scr[...] = _rms(x_ref[0], gpre_ref[...]).astype(BF16)
    nf = D_FF // tf

    def up(c):
        for lo, buf in ((c * tf, buf_g), (D_FF + c * tf, buf_v)):
            buf[c % 2, 0:past, :] = carry[:, lo:lo + tf]
            buf[c % 2, past:past + tm, :] = _dot(h_scr[...], wup_ref[:, lo:lo + tf])
            carry[:, lo:lo + tf] = buf[c % 2, tm:tm + past, :]

    gelu_c = gk_ref[0:1, 0:1].astype(FFN_ELEM_DTYPE)
    gelu_ca = gk_ref[0:1, 1:2].astype(FFN_ELEM_DTYPE)

    def elem(c):
        halves = []
        for lo, buf, scale in ((c * tf, buf_g, 1.0), (D_FF + c * tf, buf_v, 0.5)):
            b = buf.at[c % 2]
            halves.append(_conv3_from_buf(b, cw_ref[:, lo:lo + tf] * scale, b[past:past + tm, :],
                                          tm, past, rs, FFN_ELEM_DTYPE)
                          + (cb_ref[:, lo:lo + tf] * scale).astype(FFN_ELEM_DTYPE))
        x, half_v = halves
        t = jnp.tanh(x * (x * x * gelu_ca + gelu_c))
        act[c % 2] = ((x * t + x) * half_v).astype(BF16)

    def down(c):
        part = _dot(act[c % 2], wdn_ref[c * tf:(c + 1) * tf, :])
        if c == 0:
            acc[...] = part
        else:
            acc[...] += part

    for c in range(nf + 2):
        if c < nf:
            up(c)
        if 1 <= c <= nf:
            elem(c - 1)
        if c >= 2:
            down(c - 2)
    o_ref[0] = x_ref[0] + _rms(acc[...], gpost_ref[...])
    st_ref[0] = carry[...]


def _ffn(x3d, past0, gpre, w_up, conv_w, conv_b, w_dn, gpost, *, tm, rs):
    nb, rows, _ = x3d.shape
    past = past0.shape[1]
    kern = functools.partial(_ffn_kernel, tm=tm, past=past, rs=rs, tf=FFN_TF)
    gelu_k = jnp.zeros((1, LANES), F32).at[0, 0].set(GELU_C).at[0, 1].set(GELU_C * GELU_A)
    full = lambda a: pl.BlockSpec(a.shape, lambda b, j: (0,) * a.ndim)
    return pl.pallas_call(
        kern, grid=(nb, rows // tm),
        in_specs=[pl.BlockSpec((1, tm, D_MODEL), lambda b, j: (b, j, 0)),
                  pl.BlockSpec((1, past, F2), lambda b, j: (b, 0, 0)),
                  full(gpre), full(w_up), full(conv_w), full(conv_b), full(w_dn), full(gpost),
                  full(gelu_k)],
        out_specs=(pl.BlockSpec((1, tm, D_MODEL), lambda b, j: (b, j, 0)),
                   pl.BlockSpec((1, past, F2), lambda b, j: (b, 0, 0))),
        out_shape=(jax.ShapeDtypeStruct((nb, rows, D_MODEL), F32),
                   jax.ShapeDtypeStruct((nb, past, F2), F32)),
        scratch_shapes=[pltpu.VMEM((past, F2), F32), pltpu.VMEM((2, past + tm, FFN_TF), F32),
                        pltpu.VMEM((2, past + tm, FFN_TF), F32), pltpu.VMEM((tm, D_MODEL), F32),
                        pltpu.VMEM((tm, D_MODEL), BF16), pltpu.VMEM((2, tm, FFN_TF), BF16)],
        compiler_params=_cparams("parallel", "arbitrary"), name="conv_ffn",
    )(x3d, past0, gpre, w_up, conv_w, conv_b, w_dn, gpost, gelu_k)


def _swa_bias_prompt():
    qi = np.arange(WINDOW)[:, None]
    sj = np.arange(2 * WINDOW)[None, :]
    dist = WINDOW + qi - sj
    valid = (dist >= 0) & (dist <= WINDOW)
    slopes = 2.0 ** (-8.0 * np.arange(1, HQ_A + 1) / HQ_A)
    bias = np.where(valid[None], -slopes[:, None, None] * dist[None].astype(np.float64), -np.inf)
    return jnp.asarray(bias, dtype=F32)


def _swa_prompt_kernel(sink_ref, q_ref, kvc_ref, kvp_ref, bias_ref, o_ref, *, tq):
    i = pl.program_id(1)
    kvc = kvc_ref[0]
    kvp = kvp_ref[0]
    kcat = jnp.concatenate([kvp[:, 0:LANES], kvc[:, 0:LANES]], axis=0).astype(BF16)
    vcat = jnp.concatenate([kvp[:, LANES:], kvc[:, LANES:]], axis=0).astype(BF16)
    lane = lax.broadcasted_iota(jnp.int32, (WINDOW, LANES), 1)
    col = lax.broadcasted_iota(jnp.int32, (WINDOW, 2 * WINDOW), 1)
    for j in range(tq // WINDOW):
        keys = kcat[j * WINDOW:(j + 2) * WINDOW, :]
        vals = vcat[j * WINDOW:(j + 2) * WINDOW, :]
        for g in range(G_A):
            q2 = q_ref[0, j * WINDOW:(j + 1) * WINDOW, g * LANES:(g + 1) * LANES]
            outs = []
            for kv in range(HKV_A):
                hq = kv * G_A + g
                in_head = (lane < HD_A) if kv == 0 else (lane >= HD_A)
                qm = jnp.where(in_head, q2, jnp.zeros_like(q2))
                s = _dot_nt(qm, keys) + bias_ref[hq]
                if j == 0:
                    s = jnp.where(jnp.logical_and(i == 0, col < WINDOW), -jnp.inf, s)
                sink = sink_ref[hq]
                m = jnp.maximum(jnp.max(s, axis=-1, keepdims=True), sink)
                p = jnp.exp(s - m)
                den = jnp.sum(p, axis=-1, keepdims=True) + jnp.exp(sink - m)
                outs.append(_dot(p.astype(BF16), vals) / den)
            o2 = jnp.where(lane < HD_A, outs[0], outs[1])
            o_ref[0, j * WINDOW:(j + 1) * WINDOW, g * LANES:(g + 1) * LANES] = o2.astype(BF16)


def _swa_prompt(q3d, kv3d, sinks):
    nb, S, _ = q3d.shape
    tq = min(TQ_SWA, S)
    bias = _swa_bias_prompt()
    kern = functools.partial(_swa_prompt_kernel, tq=tq)
    blocks_per_tile = tq // WINDOW
    return pl.pallas_call(
        kern, grid=(nb, S // tq),
        in_specs=[pl.BlockSpec(memory_space=pltpu.SMEM),
                  pl.BlockSpec((1, tq, QA_W), lambda b, i: (b, i, 0)),
                  pl.BlockSpec((1, tq, KV_W), lambda b, i: (b, i, 0)),
                  pl.BlockSpec((1, WINDOW, KV_W),
                               lambda b, i: (b, jnp.maximum(i * blocks_per_tile - 1, 0), 0)),
                  pl.BlockSpec(bias.shape, lambda b, i: (0, 0, 0))],
        out_specs=pl.BlockSpec((1, tq, QA_W), lambda b, i: (b, i, 0)),
        out_shape=jax.ShapeDtypeStruct((nb, S, QA_W), BF16),
        compiler_params=_cparams("parallel", "arbitrary"), name="swa_prompt",
    )(sinks, q3d, kv3d, kv3d, bias)


GLA_LEVELS = (0, 1, 2, 4, 8, 16, 32)
LOG2E = 1.4426950408889634


def _gla_constants():
    r = np.arange(GLA_GROUP)
    i, j = r[:, None], r[None, :]
    ranges = [(i >= j) & (i // GLA_CHUNK == j // GLA_CHUNK)]
    masks = [i == j]
    for s in GLA_LEVELS[1:]:
        mid = (i // (2 * s)) * (2 * s) + s
        upper = (i % (2 * s) >= s) & (j >= mid) & (j <= i)
        lower = (i % (2 * s) < s) & (j > i) & (j < mid)
        ranges.append(upper | lower)
        masks.append((i // (2 * s) == j // (2 * s)) & (i % (2 * s) >= s) & (j % (2 * s) < s))
    ones_bd = (np.arange(QB_W)[:, None] // DK_B) == (np.arange(VB_W)[None, :] // DV_B)
    return (jnp.asarray(np.concatenate(ranges, axis=0), dtype=BF16),
            jnp.asarray(ones_bd, dtype=BF16), jnp.asarray(np.stack(masks), dtype=F32))


def _head_norm_gate(o, r, gn):
    parts = []
    for h in range(H_B):
        oh = o[:, h * DV_B:(h + 1) * DV_B]
        parts.append(oh * lax.rsqrt(jnp.mean(oh * oh, axis=-1, keepdims=True) + EPS))
    return jnp.concatenate(parts, axis=-1) * gn * jax.nn.silu(r)


def _gla_prompt_kernel(q_ref, k_ref, v_ref, g_ref, r_ref, gn_ref, rng_ref, lvl_ref,
                       o_ref, st_ref, state, b_scr, o_scr, qk_scr, *, tm):
    @pl.when(pl.program_id(1) == 0)
    def _():
        state[...] = jnp.zeros_like(state)

    q = q_ref[0]
    k = k_ref[0]
    v = v_ref[0]
    lane = lax.broadcasted_iota(jnp.int32, (1, LANES), 1)
    head_lanes = (lane < DK_B, lane >= DK_B)
    head_cols = lax.broadcasted_iota(jnp.int32, (1, QB_W), 1) // DK_B

    g2 = g_ref[0] * LOG2E
    ghi = g2.astype(BF16)
    glo = (g2 - ghi.astype(F32)).astype(BF16)
    nlev = len(GLA_LEVELS)
    for gi in range(tm // GLA_GROUP):
        rows = slice(gi * GLA_GROUP, (gi + 1) * GLA_GROUP)
        d_all = _dot(rng_ref[...], ghi[rows]) + _dot(rng_ref[...], glo[rows])
        b_scr[rows, :] = d_all[0:GLA_GROUP]
        qk_scr[0, 0, rows, :] = q[rows].astype(BF16)
        qk_scr[0, 1, rows, :] = k[rows].astype(BF16)
        for li in range(1, nlev):
            e = jnp.exp2(d_all[li * GLA_GROUP:(li + 1) * GLA_GROUP])
            qk_scr[li, 0, rows, :] = (q[rows] * e).astype(BF16)
            qk_scr[li, 1, rows, :] = (k[rows] * e).astype(BF16)
    b = b_scr[...]

    for gi in range(tm // GLA_GROUP):
        rows = slice(gi * GLA_GROUP, (gi + 1) * GLA_GROUP)
        for pi in range(H_B // 2):
            pr = slice(pi * LANES, (pi + 1) * LANES)
            a = None
            for li in range(nlev):
                ke = qk_scr[li, 1, rows, pr]
                kem = jnp.concatenate([jnp.where(sel, ke, jnp.zeros_like(ke)) for sel in head_lanes],
                                      axis=0)
                term = _dot_nt(qk_scr[li, 0, rows, pr], kem)
                keep = lvl_ref[li] > 0.5
                keep = jnp.concatenate([keep, keep], axis=1)
                a = jnp.where(keep, term, 0.0 if a is None else a)
            a = a.astype(BF16)
            for par in range(2):
                cols = slice((2 * pi + par) * DV_B, (2 * pi + par + 1) * DV_B)
                o_scr[rows, cols] = _dot(a[:, par * GLA_GROUP:(par + 1) * GLA_GROUP], v[rows, cols])

    for c in range(tm // GLA_CHUNK):
        rows = slice(c * GLA_CHUNK, (c + 1) * GLA_CHUNK)
        bc = b[rows]
        b_end = bc[GLA_CHUNK - 1:GLA_CHUNK, :]
        qe = (q[rows] * jnp.exp2(bc)).astype(BF16)
        ke = (k[rows] * jnp.exp2(b_end - bc)).astype(BF16)
        st = state[...]
        qm = jnp.concatenate([jnp.where(head_cols == h, qe, jnp.zeros_like(qe))
                              for h in range(H_B)], axis=0)
        km = jnp.concatenate([jnp.where(head_cols == h, ke, jnp.zeros_like(ke))
                              for h in range(H_B)], axis=0)
        vm = jnp.concatenate([v[rows, h * DV_B:(h + 1) * DV_B] for h in range(H_B)], axis=0)
        oi = _dot_nt(qm, st.astype(BF16))
        for h in range(H_B):
            o_scr[rows, h * DV_B:(h + 1) * DV_B] += oi[h * GLA_CHUNK:(h + 1) * GLA_CHUNK]
        state[...] = st * jnp.exp2(b_end) + _dot_tn(vm, km)

    o_ref[0] = _head_norm_gate(o_scr[...], r_ref[0], gn_ref[...]).astype(BF16)
    st_ref[0] = state[...]


def _gla_prompt(q3d, k3d, v3d, g3d, r3d, gn):
    nb, S, _ = q3d.shape
    tm = min(TM_GLA, S)
    rng, _, lvl = _gla_constants()
    kern = functools.partial(_gla_prompt_kernel, tm=tm)
    tok = lambda w: pl.BlockSpec((1, tm, w), lambda b, j: (b, j, 0))
    full = lambda a: pl.BlockSpec(a.shape, lambda b, j: (0,) * a.ndim)
    return pl.pallas_call(
        kern, grid=(nb, S // tm),
        in_specs=[tok(QB_W), tok(QB_W), tok(VB_W), tok(QB_W), tok(VB_W),
                  full(gn), full(rng), full(lvl)],
        out_specs=(tok(VB_W), pl.BlockSpec((1, DV_B, QB_W), lambda b, j: (b, 0, 0))),
        out_shape=(jax.ShapeDtypeStruct((nb, S, VB_W), BF16),
                   jax.ShapeDtypeStruct((nb, DV_B, QB_W), F32)),
        scratch_shapes=[pltpu.VMEM((DV_B, QB_W), F32), pltpu.VMEM((tm, QB_W), F32),
                        pltpu.VMEM((tm, VB_W), F32),
                        pltpu.VMEM((len(GLA_LEVELS), 2, tm, QB_W), BF16)],
        compiler_params=_cparams("parallel", "arbitrary"), name="gla_prompt",
    )(q3d, k3d, v3d, g3d, r3d, gn, rng, lvl)


def _swa_bias_decode(L):
    rows = HQ_A * L
    kpad = WINDOW + SUBLANES
    hq = np.arange(rows)[:, None] // L
    t = np.arange(rows)[:, None] % L
    s = np.arange(kpad)[None, :]
    dist = WINDOW + t - s
    valid = (dist >= 0) & (dist <= WINDOW) & (s < WINDOW + L)
    slopes = 2.0 ** (-8.0 * (hq + 1) / HQ_A)
    return jnp.asarray(np.where(valid, -slopes * dist, -np.inf), dtype=F32)


def _swa_decode_kernel(q_ref, kvn_ref, ck_ref, cv_ref, bias_ref, sink_ref,
                       o_ref, nk_ref, nv_ref, qs, os_, kc, vc, kn, vn, *, ns, L, sb):
    i = pl.program_id(0)
    rows_q = HQ_A * L
    lane = lax.broadcasted_iota(jnp.int32, (1, LANES), 1)

    @pl.when(i == 0)
    def _():
        q = q_ref[...].astype(F32)
        for kv in range(HKV_A):
            sel = (lane < HD_A) if kv == 0 else (lane >= HD_A)
            for g in range(G_A):
                base = (kv * G_A + g) * L * ns
                qs[base:base + L * ns, :] = jnp.where(sel, q[:, g * LANES:(g + 1) * LANES], 0.0)
        kc[...] = jnp.zeros_like(kc)
        vc[...] = jnp.zeros_like(vc)
        kn[...] = kvn_ref[:, 0:LANES]
        vn[...] = kvn_ref[:, LANES:]

    bias = bias_ref[...]
    sink = sink_ref[...]

    def body(s, carry):
        seq = i * sb + s
        kc[0:WINDOW, :] = ck_ref[s]
        vc[0:WINDOW, :] = cv_ref[s]
        kc[WINDOW:WINDOW + L, :] = kn[pl.ds(seq, L, stride=ns), :]
        vc[WINDOW:WINDOW + L, :] = vn[pl.ds(seq, L, stride=ns), :]
        nk_ref[s] = kc[L:L + WINDOW, :]
        nv_ref[s] = vc[L:L + WINDOW, :]
        lhs = qs[pl.ds(seq, rows_q, stride=ns), :].astype(BF16)
        sc = _dot_nt(lhs, kc[...].astype(BF16)) + bias
        m = jnp.maximum(jnp.max(sc, axis=-1, keepdims=True), sink)
        p = jnp.exp(sc - m)
        den = jnp.sum(p, axis=-1, keepdims=True) + jnp.exp(sink - m)
        res = _dot(p.astype(BF16), vc[...].astype(BF16)) / den
        half = rows_q // 2
        os_[pl.ds(seq, half, stride=ns), :] = jnp.where(lane < HD_A, res[0:half], res[half:])
        return carry

    lax.fori_loop(0, sb, body, 0)

    @pl.when(i == pl.num_programs(0) - 1)
    def _():
        for g in range(G_A):
            o_ref[:, g * LANES:(g + 1) * LANES] = os_[g * L * ns:(g + 1) * L * ns, :].astype(BF16)


def _swa_decode(q_tm, kvn_tm, cache_k, cache_v, sinks, *, ns, L):
    sb = SEQ_BLOCK
    bias = _swa_bias_decode(L)
    sink_col = jnp.broadcast_to(jnp.repeat(sinks.astype(F32), L)[:, None], (HQ_A * L, 1))
    kern = functools.partial(_swa_decode_kernel, ns=ns, L=L, sb=sb)
    full = lambda a: pl.BlockSpec(a.shape, lambda i: (0,) * a.ndim)
    cache = pl.BlockSpec((sb, WINDOW, LANES), lambda i: (i, 0, 0))
    kpad = WINDOW + SUBLANES
    return pl.pallas_call(
        kern, grid=(ns // sb,),
        in_specs=[full(q_tm), full(kvn_tm), cache, cache, full(bias), full(sink_col)],
        out_specs=(pl.BlockSpec((L * ns, QA_W), lambda i: (0, 0)), cache, cache),
        out_shape=(jax.ShapeDtypeStruct((L * ns, QA_W), BF16),
                   jax.ShapeDtypeStruct(cache_k.shape, F32),
                   jax.ShapeDtypeStruct(cache_v.shape, F32)),
        scratch_shapes=[pltpu.VMEM((HQ_A * L * ns, LANES), F32),
                        pltpu.VMEM((G_A * L * ns, LANES), F32),
                        pltpu.VMEM((kpad, LANES), F32), pltpu.VMEM((kpad, LANES), F32),
                        pltpu.VMEM((L * ns, LANES), F32), pltpu.VMEM((L * ns, LANES), F32)],
        compiler_params=_cparams("arbitrary"), name="swa_decode",
    )(q_tm, kvn_tm, cache_k, cache_v, bias, sink_col)


def _gla_decode_kernel(q_ref, k_ref, v_ref, g_ref, r_ref, gn_ref, ones_ref, s0_ref,
                       o_ref, s1_ref, qe2, ke2, v2, dec3, oi2, od, *, ns, L, sb):
    i = pl.program_id(0)
    lane = lax.broadcasted_iota(jnp.int32, (1, LANES), 1)
    npair = H_B // 2

    @pl.when(i == 0)
    def _():
        slab = lambda a, t: a[t * ns:(t + 1) * ns, :]
        q, k, g = q_ref[...], k_ref[...], g_ref[...]
        vf = v_ref[...].astype(F32)
        b = [slab(g, 0)]
        for t in range(1, L):
            b.append(b[-1] + slab(g, t))
        for t in range(L):
            acc = jnp.zeros((ns, VB_W), F32)
            for jj in range(t + 1):
                p = (slab(q, t) * slab(k, jj) * jnp.exp(b[t] - b[jj])).astype(BF16)
                acc = acc + _dot(p, ones_ref[...]) * slab(vf, jj)
            od[t * ns:(t + 1) * ns, :] = acc
            qe = slab(q, t) * jnp.exp(b[t])
            ke = slab(k, t) * jnp.exp(b[L - 1] - b[t])
            for par in range(2):
                sel = (lane < DK_B) if par == 0 else (lane >= DK_B)
                base = (par * L + t) * ns
                for pi in range(npair):
                    pr = slice(pi * LANES, (pi + 1) * LANES)
                    qe2[pi, base:base + ns, :] = jnp.where(sel, qe[:, pr], 0.0)
                    ke2[pi, base:base + ns, :] = jnp.where(sel, ke[:, pr], 0.0)
                    h = 2 * pi + par
                    v2[pi, base:base + ns, :] = slab(vf, t)[:, h * DV_B:(h + 1) * DV_B]
        dec3[...] = jnp.zeros_like(dec3)
        hi, mid, lo = _split3(jnp.exp(b[L - 1]))
        for pi in range(npair):
            pr = slice(pi * LANES, (pi + 1) * LANES)
            dec3[pi, 0:ns, :] = hi[:, pr].astype(F32)
            dec3[pi, ns:2 * ns, :] = mid[:, pr].astype(F32)
            dec3[pi, 2 * ns:3 * ns, :] = lo[:, pr].astype(F32)

    ones8 = jnp.ones((2 * L, LANES), BF16)

    def body(s, carry):
        seq = i * sb + s
        st = s0_ref[s]
        for pi in range(npair):
            take = lambda ref: ref[pi, pl.ds(seq, 2 * L, stride=ns), :].astype(BF16)
            st_p = st[pi * LANES:(pi + 1) * LANES, :]
            oi2[pi, pl.ds(seq, 2 * L, stride=ns), :] = _dot(take(qe2), st_p.astype(BF16))
            dcol = _dot_tn(take(dec3), ones8)
            upd = _dot_tn(take(ke2), take(v2))
            s1_ref[s, pi * LANES:(pi + 1) * LANES, :] = dcol * st_p + upd
        return carry

    lax.fori_loop(0, sb, body, 0)

    @pl.when(i == pl.num_programs(0) - 1)
    def _():
        for t in range(L):
            parts = []
            for h in range(H_B):
                base = ((h % 2) * L + t) * ns
                parts.append(oi2[h // 2, base:base + ns, :])
            o = jnp.concatenate(parts, axis=-1) + od[t * ns:(t + 1) * ns, :]
            o_ref[t * ns:(t + 1) * ns, :] = _head_norm_gate(
                o, r_ref[t * ns:(t + 1) * ns, :], gn_ref[...]).astype(BF16)


def _gla_decode(q_tm, k_tm, v_tm, g_tm, r_tm, gn, s0, *, ns, L):
    sb = SEQ_BLOCK
    assert 2 * L == SUBLANES
    _, ones_bd, _ = _gla_constants()
    kern = functools.partial(_gla_decode_kernel, ns=ns, L=L, sb=sb)
    full = lambda a: pl.BlockSpec(a.shape, lambda i: (0,) * a.ndim)
    st = pl.BlockSpec((sb, QB_W, DV_B), lambda i: (i, 0, 0))
    rows2 = 2 * L * ns
    return pl.pallas_call(
        kern, grid=(ns // sb,),
        in_specs=[full(q_tm), full(k_tm), full(v_tm), full(g_tm), full(r_tm), full(gn),
                  full(ones_bd), st],
        out_specs=(pl.BlockSpec((L * ns, VB_W), lambda i: (0, 0)), st),
        out_shape=(jax.ShapeDtypeStruct((L * ns, VB_W), BF16),
                   jax.ShapeDtypeStruct(s0.shape, F32)),
        scratch_shapes=[pltpu.VMEM((H_B // 2, rows2, LANES), F32) for _ in range(5)]
        + [pltpu.VMEM((L * ns, VB_W), F32)],
        compiler_params=_cparams("arbitrary"), name="gla_decode",
    )(q_tm, k_tm, v_tm, g_tm, r_tm, gn, ones_bd, s0)


def _qa_perm():
    return np.asarray([(kv * G_A + g) * HD_A + d
                       for g in range(G_A) for kv in range(HKV_A) for d in range(HD_A)])


def _prep_even(w_in, w_gate_up, b_gate, w_out):
    perm = _qa_perm()
    w_main = jnp.concatenate([w_in[:, :QA_W][:, perm], w_in[:, QA_W:MAIN_W]], axis=1).astype(BF16)
    w_g = jnp.pad(w_in[:, MAIN_W:], ((0, 0), (0, LANES - GATE_RANK))).astype(BF16)
    w_gu = jnp.pad(w_gate_up, ((0, LANES - GATE_RANK), (0, 0))).astype(BF16)
    w_o = jnp.concatenate([w_out[:QA_W][perm], w_out[QA_W:]], axis=0).astype(BF16)
    return w_main, w_g, w_gu, b_gate.reshape(1, -1), w_o


def _row(v):
    return v.reshape(1, -1)


def kernel(x_prompt, x_sample, cache_swa_k, cache_swa_v, state_gla, state_conv, state_ffn,
           norm_mix_pre, norm_mix_post, norm_ffn_pre, norm_ffn_post, w_in_even, w_gate_up, b_gate,
           attn_sinks, gla_norm, w_out_even, w_in_odd, conv_w_odd, w_out_odd, ffn_up, ffn_conv_w,
           ffn_conv_b, ffn_down):
    nb, S, _ = x_prompt.shape
    ns, L, _ = x_sample.shape
    depth = norm_mix_pre.shape[0]

    xp = x_prompt
    xs = x_sample.transpose(1, 0, 2).reshape(1, L * ns, D_MODEL)
    past_p = max(SUBLANES, 2)
    past_s = 2 * ns
    tm_p = min(TM_TOK, S)

    ks_p, vs_p, gs_p, cs_p, fs_p = [], [], [], [], []
    ks_s, vs_s, gs_s, cs_s, fs_s = [], [], [], [], []

    for l in range(depth):
        gpre, gpost = _row(norm_mix_pre[l]), _row(norm_mix_post[l])
        if l % 2 == 0:
            e = l // 2
            w_main, w_g, w_gu, b_g, w_o = _prep_even(w_in_even[e], w_gate_up[e], b_gate[e],
                                                     w_out_even[e])
            gn = _row(gla_norm[e])
            qa, kv, qb, kb, vb, rb, gb = _inproj_even(xp.reshape(nb * S, D_MODEL), gpre,
                                                      w_main, w_g, w_gu, b_g)
            r3 = lambda a: a.reshape(nb, S, a.shape[-1])
            oa = _swa_prompt(r3(qa), r3(kv), attn_sinks[e])
            ob, st_t = _gla_prompt(r3(qb), r3(kb), r3(vb), r3(gb), r3(rb), gn)
            y = jnp.concatenate([oa, ob], axis=-1).reshape(nb * S, QA_W + VB_W)
            xp = _proj_resid(y, xp.reshape(nb * S, D_MODEL), w_o, gpost).reshape(nb, S, D_MODEL)
            kv_last = r3(kv)[:, S - WINDOW:, :]
            ks_p.append(kv_last[..., :LANES].reshape(nb, WINDOW, HKV_A, HD_A))
            vs_p.append(kv_last[..., LANES:].reshape(nb, WINDOW, HKV_A, HD_A))
            gs_p.append(st_t.transpose(0, 2, 1).reshape(nb, H_B, DK_B, DV_B))
            qa, kv, qb, kb, vb, rb, gb = _inproj_even(xs.reshape(L * ns, D_MODEL), gpre,
                                                      w_main, w_g, w_gu, b_g)
            oa, nk, nv = _swa_decode(qa, kv, cache_swa_k[e].reshape(ns, WINDOW, LANES),
                                     cache_swa_v[e].reshape(ns, WINDOW, LANES), attn_sinks[e],
                                     ns=ns, L=L)
            ob, s1 = _gla_decode(qb, kb, vb, gb, rb, gn, state_gla[e].reshape(ns, QB_W, DV_B),
                                 ns=ns, L=L)
            y = jnp.concatenate([oa, ob], axis=-1)
            xs = _proj_resid(y, xs.reshape(L * ns, D_MODEL), w_o, gpost).reshape(1, L * ns, D_MODEL)
            ks_s.append(nk.reshape(ns, WINDOW, HKV_A, HD_A))
            vs_s.append(nv.reshape(ns, WINDOW, HKV_A, HD_A))
            gs_s.append(s1.reshape(ns, H_B, DK_B, DV_B))
        else:
            o = l // 2
            w_in = w_in_odd[o].astype(BF16)
            w_o = w_out_odd[o].astype(BF16)
            y, st = _odd_in(xp, jnp.zeros((nb, past_p, D_MODEL), F32), gpre, w_in, conv_w_odd[o],
                            tm=tm_p, rs=1)
            xp = _proj_resid(y.reshape(nb * S, D_MODEL), xp.reshape(nb * S, D_MODEL), w_o,
                             gpost).reshape(nb, S, D_MODEL)
            cs_p.append(st[:, past_p - 2:, :])
            past = state_conv[o].transpose(1, 0, 2).reshape(1, past_s, D_MODEL)
            y, st = _odd_in(xs, past, gpre, w_in, conv_w_odd[o], tm=L * ns, rs=ns)
            xs = _proj_resid(y.reshape(L * ns, D_MODEL), xs.reshape(L * ns, D_MODEL), w_o,
                             gpost).reshape(1, L * ns, D_MODEL)
            cs_s.append(st.reshape(2, ns, D_MODEL).transpose(1, 0, 2))

        gpre, gpost = _row(norm_ffn_pre[l]), _row(norm_ffn_post[l])
        w_up = ffn_up[l].astype(BF16)
        w_dn = ffn_down[l].astype(BF16)
        cb = _row(ffn_conv_b[l])
        xp, st = _ffn(xp, jnp.zeros((nb, past_p, F2), F32), gpre, w_up, ffn_conv_w[l], cb, w_dn,
                      gpost, tm=tm_p, rs=1)
        fs_p.append(st[:, past_p - 2:, :])
        past = state_ffn[l].transpose(1, 0, 2).reshape(1, past_s, F2)
        xs, st = _ffn(xs, past, gpre, w_up, ffn_conv_w[l], cb, w_dn, gpost, tm=L * ns, rs=ns)
        fs_s.append(st.reshape(2, ns, F2).transpose(1, 0, 2))

    y_sample = xs.reshape(L, ns, D_MODEL).transpose(1, 0, 2)
    return (xp, y_sample, jnp.stack(ks_p), jnp.stack(vs_p), jnp.stack(gs_p), jnp.stack(cs_p),
            jnp.stack(fs_p), jnp.stack(ks_s), jnp.stack(vs_s), jnp.stack(gs_s), jnp.stack(cs_s),
            jnp.stack(fs_s))
```

```python
import functools

import numpy as np
import jax
import jax.numpy as jnp
from jax import lax
from jax.experimental import pallas as pl
from jax.experimental.pallas import tpu as pltpu

F32 = jnp.float32
BF16 = jnp.bfloat16

D_MODEL = 1024
WINDOW = 128
HD_A = 64
HQ_A = 8
HKV_A = 2
G_A = HQ_A // HKV_A
H_B = 4
DK_B = 64
DV_B = 128
GATE_RANK = 16
GATE_TAU = 16.0
GLA_CHUNK = 64
D_FF = ((8 * D_MODEL // 3 + 127) // 128) * 128
F2 = 2 * D_FF
EPS = 1e-6
GELU_C = float(np.sqrt(2.0 / np.pi))
GELU_A = 0.044715

QA_W = HQ_A * HD_A
KV_W = 2 * HKV_A * HD_A
QB_W = H_B * DK_B
VB_W = H_B * DV_B
MAIN_W = QA_W + KV_W + 2 * QB_W + 2 * VB_W

LANES = 128
SUBLANES = 8
VMEM_LIMIT = 56 * 1024 * 1024

TM_TOK = 512
TM_GLA = 512
TQ_SWA = 512
GLA_GROUP = 128
FFN_TF = 256
FFN_ELEM_DTYPE = BF16
SEQ_BLOCK = 16


def _cparams(*sem):
    return pltpu.CompilerParams(dimension_semantics=sem, vmem_limit_bytes=VMEM_LIMIT)


def _rms(x, g):
    return x * lax.rsqrt(jnp.mean(x * x, axis=-1, keepdims=True) + EPS) * g


def _dot(a, b):
    return jnp.dot(a, b, preferred_element_type=F32)


def _dot_nt(a, b):
    return lax.dot_general(a, b, (((1,), (1,)), ((), ())), preferred_element_type=F32)


def _dot_tn(a, b):
    return lax.dot_general(a, b, (((0,), (0,)), ((), ())), preferred_element_type=F32)


def _split3(x):
    hi = x.astype(BF16)
    r1 = x - hi.astype(F32)
    mid = r1.astype(BF16)
    lo = (r1 - mid.astype(F32)).astype(BF16)
    return hi, mid, lo


def _inproj_even_kernel(x_ref, gpre_ref, w_ref, wg_ref, wgu_ref, bg_ref,
                        qa_ref, kv_ref, qb_ref, kb_ref, vb_ref, rb_ref, gb_ref):
    h = _rms(x_ref[...], gpre_ref[...]).astype(BF16)

    def mm(lo, width):
        return _dot(h, w_ref[:, lo:lo + width])

    lo = 0
    qa_ref[...] = (mm(lo, QA_W) * (HD_A ** -0.5)).astype(BF16)
    lo += QA_W
    kv_ref[...] = mm(lo, KV_W)
    lo += KV_W
    qb_ref[...] = mm(lo, QB_W) * (DK_B ** -0.5)
    lo += QB_W
    kb_ref[...] = mm(lo, QB_W)
    lo += QB_W
    vb_ref[...] = mm(lo, VB_W).astype(BF16)
    lo += VB_W
    rb_ref[...] = mm(lo, VB_W)
    glr = _dot(h, wg_ref[...]).astype(BF16)
    z = _dot(glr, wgu_ref[...]) + bg_ref[...]
    gb_ref[...] = jax.nn.log_sigmoid(z) * (1.0 / GATE_TAU)


def _inproj_even(x2d, gpre, w_main, w_g, w_gu, b_g):
    T = x2d.shape[0]
    tm = min(TM_TOK, T)
    row = lambda w: pl.BlockSpec((tm, w), lambda i: (i, 0))
    full = lambda a: pl.BlockSpec(a.shape, lambda i: (0,) * a.ndim)
    out_shape = (
        jax.ShapeDtypeStruct((T, QA_W), BF16), jax.ShapeDtypeStruct((T, KV_W), F32),
        jax.ShapeDtypeStruct((T, QB_W), F32), jax.ShapeDtypeStruct((T, QB_W), F32),
        jax.ShapeDtypeStruct((T, VB_W), BF16), jax.ShapeDtypeStruct((T, VB_W), F32),
        jax.ShapeDtypeStruct((T, QB_W), F32))
    return pl.pallas_call(
        _inproj_even_kernel, grid=(T // tm,),
        in_specs=[row(D_MODEL), full(gpre), full(w_main), full(w_g), full(w_gu), full(b_g)],
        out_specs=(row(QA_W), row(KV_W), row(QB_W), row(QB_W), row(VB_W), row(VB_W), row(QB_W)),
        out_shape=out_shape, compiler_params=_cparams("parallel"), name="inproj_even",
    )(x2d, gpre, w_main, w_g, w_gu, b_g)


def _conv3_from_buf(buf, cw, cur, tm, past, rs, dtype=F32):
    cw = cw.astype(dtype)
    y = cw[0:1, :] * buf[past - 2 * rs:past - 2 * rs + tm, :].astype(dtype)
    y = y + cw[1:2, :] * buf[past - rs:past - rs + tm, :].astype(dtype)
    return y + cw[2:3, :] * cur.astype(dtype)


def _odd_in_kernel(x_ref, past_ref, gpre_ref, win_ref, cw_ref, y_ref, st_ref, carry, buf,
                   *, tm, past, rs, tc):
    @pl.when(pl.program_id(1) == 0)
    def _():
        carry[...] = past_ref[0]

    h = _rms(x_ref[0], gpre_ref[...]).astype(BF16)
    for c in range(D_MODEL // tc):
        lo = c * tc
        bg = _dot(h, win_ref[:, lo:lo + tc])
        cu = _dot(h, win_ref[:, D_MODEL + lo:D_MODEL + lo + tc]) * \
            _dot(h, win_ref[:, 2 * D_MODEL + lo:2 * D_MODEL + lo + tc])
        buf[0:past, :] = carry[:, lo:lo + tc]
        buf[past:past + tm, :] = cu
        carry[:, lo:lo + tc] = buf[tm:tm + past, :]
        z = _conv3_from_buf(buf, cw_ref[:, lo:lo + tc], cu, tm, past, rs)
        y_ref[0, :, lo:lo + tc] = (bg * z).astype(BF16)
    st_ref[0] = carry[...]


def _odd_in(x3d, past0, gpre, w_in, conv_w, *, tm, rs):
    nb, rows, _ = x3d.shape
    past = past0.shape[1]
    tc = 512
    kern = functools.partial(_odd_in_kernel, tm=tm, past=past, rs=rs, tc=tc)
    full = lambda a: pl.BlockSpec(a.shape, lambda b, j: (0,) * a.ndim)
    return pl.pallas_call(
        kern, grid=(nb, rows // tm),
        in_specs=[pl.BlockSpec((1, tm, D_MODEL), lambda b, j: (b, j, 0)),
                  pl.BlockSpec((1, past, D_MODEL), lambda b, j: (b, 0, 0)),
                  full(gpre), full(w_in), full(conv_w)],
        out_specs=(pl.BlockSpec((1, tm, D_MODEL), lambda b, j: (b, j, 0)),
                   pl.BlockSpec((1, past, D_MODEL), lambda b, j: (b, 0, 0))),
        out_shape=(jax.ShapeDtypeStruct((nb, rows, D_MODEL), BF16),
                   jax.ShapeDtypeStruct((nb, past, D_MODEL), F32)),
        scratch_shapes=[pltpu.VMEM((past, D_MODEL), F32), pltpu.VMEM((past + tm, tc), F32)],
        compiler_params=_cparams("parallel", "arbitrary"), name="odd_in",
    )(x3d, past0, gpre, w_in, conv_w)


def _ffn_kernel(x_ref, ya_ref, yb_ref, wo_ref, gmix_ref, past_ref, gpre_ref, wup_ref, cw_ref,
                cb_ref, wdn_ref, gpost_ref, gk_ref, o_ref, st_ref,
                carry, buf_g, buf_v, acc, h_scr, act, x1_scr, *, tm, past, rs, tf):
    @pl.when(pl.program_id(1) == 0)
    def _():
        carry[...] = past_ref[0]

    mix = _dot(ya_ref[0], wo_ref[0]) + _dot(yb_ref[0], wo_ref[1])
    x1_scr[...] = x_ref[0] + _rms(mix, gmix_ref[...])
    h_scr[...] = _rms(x1_scr[...], gpre_ref[...]).astype(BF16)
    nf = D_FF // tf

    def up(c):
        for lo, buf in ((c * tf, buf_g), (D_FF + c * tf, buf_v)):
            buf[c % 2, 0:past, :] = carry[:, lo:lo + tf]
            buf[c % 2, past:past + tm, :] = _dot(h_scr[...], wup_ref[:, lo:lo + tf])
            carry[:, lo:lo + tf] = buf[c % 2, tm:tm + past, :]

    gelu_c = gk_ref[0:1, 0:1].astype(FFN_ELEM_DTYPE)
    gelu_ca = gk_ref[0:1, 1:2].astype(FFN_ELEM_DTYPE)

    def elem(c):
        halves = []
        for lo, buf, scale in ((c * tf, buf_g, 1.0), (D_FF + c * tf, buf_v, 0.5)):
            b = buf.at[c % 2]
            halves.append(_conv3_from_buf(b, cw_ref[:, lo:lo + tf] * scale, b[past:past + tm, :],
                                          tm, past, rs, FFN_ELEM_DTYPE)
                          + (cb_ref[:, lo:lo + tf] * scale).astype(FFN_ELEM_DTYPE))
        x, half_v = halves
        t = jnp.tanh(x * (x * x * gelu_ca + gelu_c))
        act[c % 2] = ((x * t + x) * half_v).astype(BF16)

    def down(c):
        part = _dot(act[c % 2], wdn_ref[c * tf:(c + 1) * tf, :])
        if c == 0:
            acc[...] = part
        else:
            acc[...] += part

    for c in range(nf + 2):
        if c < nf:
            up(c)
        if 1 <= c <= nf:
            elem(c - 1)
        if c >= 2:
            down(c - 2)
    o_ref[0] = x1_scr[...] + _rms(acc[...], gpost_ref[...])
    st_ref[0] = carry[...]


def _ffn(x3d, ya, yb, w_o, gmix, past0, gpre, w_up, conv_w, conv_b, w_dn, gpost, layer, *, tm, rs):
    nb, rows, _ = x3d.shape
    past = past0.shape[1]
    half = w_o.shape[0] // 2
    kern = functools.partial(_ffn_kernel, tm=tm, past=past, rs=rs, tf=FFN_TF)
    gelu_k = jnp.zeros((1, LANES), F32).at[0, 0].set(GELU_C).at[0, 1].set(GELU_C * GELU_A)
    w_o2 = w_o.reshape(2, half, D_MODEL)
    once = dict(pipeline_mode=pl.Buffered(1))
    full = lambda a: pl.BlockSpec(a.shape, lambda b, j: (0,) * a.ndim, **once)
    per_layer = lambda a: pl.BlockSpec((None,) + a.shape[1:], lambda b, j: (layer, 0, 0), **once)
    tok = lambda w, cblk=0: pl.BlockSpec((1, tm, w), lambda b, j: (b, j, cblk))
    return pl.pallas_call(
        kern, grid=(nb, rows // tm),
        in_specs=[tok(D_MODEL), tok(half, ya[1]), tok(half, yb[1]), full(w_o2), full(gmix),
                  pl.BlockSpec((1, past, F2), lambda b, j: (b, 0, 0)),
                  full(gpre), per_layer(w_up), full(conv_w), full(conv_b), per_layer(w_dn),
                  full(gpost), full(gelu_k)],
        out_specs=(tok(D_MODEL), pl.BlockSpec((1, past, F2), lambda b, j: (b, 0, 0))),
        out_shape=(jax.ShapeDtypeStruct((nb, rows, D_MODEL), F32),
                   jax.ShapeDtypeStruct((nb, past, F2), F32)),
        scratch_shapes=[pltpu.VMEM((past, F2), F32), pltpu.VMEM((2, past + tm, FFN_TF), F32),
                        pltpu.VMEM((2, past + tm, FFN_TF), F32), pltpu.VMEM((tm, D_MODEL), F32),
                        pltpu.VMEM((tm, D_MODEL), BF16), pltpu.VMEM((2, tm, FFN_TF), BF16),
                        pltpu.VMEM((tm, D_MODEL), F32)],
        compiler_params=_cparams("parallel", "arbitrary"), name="conv_ffn",
    )(x3d, ya[0], yb[0], w_o2, gmix, past0, gpre, w_up, conv_w, conv_b, w_dn, gpost, gelu_k)


def _swa_bias_prompt():
    qi = np.arange(WINDOW)[:, None]
    sj = np.arange(2 * WINDOW)[None, :]
    dist = WINDOW + qi - sj
    valid = (dist >= 0) & (dist <= WINDOW)
    slopes = 2.0 ** (-8.0 * np.arange(1, HQ_A + 1) / HQ_A)
    bias = np.where(valid[None], -slopes[:, None, None] * dist[None].astype(np.float64), -np.inf)
    return jnp.asarray(bias, dtype=F32)


def _swa_prompt_kernel(sink_ref, q_ref, kvc_ref, kvp_ref, bias_ref, o_ref, *, tq):
    i = pl.program_id(1)
    kvc = kvc_ref[0]
    kvp = kvp_ref[0]
    kcat = jnp.concatenate([kvp[:, 0:LANES], kvc[:, 0:LANES]], axis=0).astype(BF16)
    vcat = jnp.concatenate([kvp[:, LANES:], kvc[:, LANES:]], axis=0).astype(BF16)
    lane = lax.broadcasted_iota(jnp.int32, (WINDOW, LANES), 1)
    col = lax.broadcasted_iota(jnp.int32, (WINDOW, 2 * WINDOW), 1)
    for j in range(tq // WINDOW):
        keys = kcat[j * WINDOW:(j + 2) * WINDOW, :]
        vals = vcat[j * WINDOW:(j + 2) * WINDOW, :]
        for g in range(G_A):
            q2 = q_ref[0, j * WINDOW:(j + 1) * WINDOW, g * LANES:(g + 1) * LANES]
            outs = []
            for kv in range(HKV_A):
                hq = kv * G_A + g
                in_head = (lane < HD_A) if kv == 0 else (lane >= HD_A)
                qm = jnp.where(in_head, q2, jnp.zeros_like(q2))
                s = _dot_nt(qm, keys) + bias_ref[hq]
                if j == 0:
                    s = jnp.where(jnp.logical_and(i == 0, col < WINDOW), -jnp.inf, s)
                sink = sink_ref[hq]
                m = jnp.maximum(jnp.max(s, axis=-1, keepdims=True), sink)
                p = jnp.exp(s - m)
                den = jnp.sum(p, axis=-1, keepdims=True) + jnp.exp(sink - m)
                outs.append(_dot(p.astype(BF16), vals) / den)
            o2 = jnp.where(lane < HD_A, outs[0], outs[1])
            o_ref[0, j * WINDOW:(j + 1) * WINDOW, g * LANES:(g + 1) * LANES] = o2.astype(BF16)


def _swa_prompt(q3d, kv3d, sinks):
    nb, S, _ = q3d.shape
    tq = min(TQ_SWA, S)
    bias = _swa_bias_prompt()
    kern = functools.partial(_swa_prompt_kernel, tq=tq)
    blocks_per_tile = tq // WINDOW
    return pl.pallas_call(
        kern, grid=(nb, S // tq),
        in_specs=[pl.BlockSpec(memory_space=pltpu.SMEM),
                  pl.BlockSpec((1, tq, QA_W), lambda b, i: (b, i, 0)),
                  pl.BlockSpec((1, tq, KV_W), lambda b, i: (b, i, 0)),
                  pl.BlockSpec((1, WINDOW, KV_W),
                               lambda b, i: (b, jnp.maximum(i * blocks_per_tile - 1, 0), 0)),
                  pl.BlockSpec(bias.shape, lambda b, i: (0, 0, 0))],
        out_specs=pl.BlockSpec((1, tq, QA_W), lambda b, i: (b, i, 0)),
        out_shape=jax.ShapeDtypeStruct((nb, S, QA_W), BF16),
        compiler_params=_cparams("parallel", "arbitrary"), name="swa_prompt",
    )(sinks, q3d, kv3d, kv3d, bias)


GLA_LEVELS = (0, 1, 2, 4, 8, 16, 32)
LOG2E = 1.4426950408889634


def _gla_constants():
    r = np.arange(GLA_GROUP)
    i, j = r[:, None], r[None, :]
    ranges = [(i >= j) & (i // GLA_CHUNK == j // GLA_CHUNK)]
    masks = [i == j]
    for s in GLA_LEVELS[1:]:
        mid = (i // (2 * s)) * (2 * s) + s
        upper = (i % (2 * s) >= s) & (j >= mid) & (j <= i)
        lower = (i % (2 * s) < s) & (j > i) & (j < mid)
        ranges.append(upper | lower)
        masks.append((i // (2 * s) == j // (2 * s)) & (i % (2 * s) >= s) & (j % (2 * s) < s))
    ones_bd = (np.arange(QB_W)[:, None] // DK_B) == (np.arange(VB_W)[None, :] // DV_B)
    return (jnp.asarray(np.concatenate(ranges, axis=0), dtype=BF16),
            jnp.asarray(ones_bd, dtype=BF16), jnp.asarray(np.stack(masks), dtype=F32))


def _head_norm_gate(o, r, gn):
    parts = []
    for h in range(H_B):
        oh = o[:, h * DV_B:(h + 1) * DV_B]
        parts.append(oh * lax.rsqrt(jnp.mean(oh * oh, axis=-1, keepdims=True) + EPS))
    return jnp.concatenate(parts, axis=-1) * gn * jax.nn.silu(r)


def _gla_prompt_kernel(q_ref, k_ref, v_ref, g_ref, r_ref, gn_ref, rng_ref, lvl_ref,
                       o_ref, st_ref, state, b_scr, o_scr, qk_scr, *, tm):
    @pl.when(pl.program_id(1) == 0)
    def _():
        state[...] = jnp.zeros_like(state)

    q = q_ref[0]
    k = k_ref[0]
    v = v_ref[0]
    lane = lax.broadcasted_iota(jnp.int32, (1, LANES), 1)
    head_lanes = (lane < DK_B, lane >= DK_B)
    head_cols = lax.broadcasted_iota(jnp.int32, (1, QB_W), 1) // DK_B

    g2 = g_ref[0] * LOG2E
    ghi = g2.astype(BF16)
    glo = (g2 - ghi.astype(F32)).astype(BF16)
    nlev = len(GLA_LEVELS)
    for gi in range(tm // GLA_GROUP):
        rows = slice(gi * GLA_GROUP, (gi + 1) * GLA_GROUP)
        d_all = _dot(rng_ref[...], ghi[rows]) + _dot(rng_ref[...], glo[rows])
        b_scr[rows, :] = d_all[0:GLA_GROUP]
        qk_scr[0, 0, rows, :] = q[rows].astype(BF16)
        qk_scr[0, 1, rows, :] = k[rows].astype(BF16)
        for li in range(1, nlev):
            e = jnp.exp2(d_all[li * GLA_GROUP:(li + 1) * GLA_GROUP])
            qk_scr[li, 0, rows, :] = (q[rows] * e).astype(BF16)
            qk_scr[li, 1, rows, :] = (k[rows] * e).astype(BF16)
    b = b_scr[...]

    for gi in range(tm // GLA_GROUP):
        rows = slice(gi * GLA_GROUP, (gi + 1) * GLA_GROUP)
        for pi in range(H_B // 2):
            pr = slice(pi * LANES, (pi + 1) * LANES)
            a = None
            for li in range(nlev):
                ke = qk_scr[li, 1, rows, pr]
                kem = jnp.concatenate([jnp.where(sel, ke, jnp.zeros_like(ke)) for sel in head_lanes],
                                      axis=0)
                term = _dot_nt(qk_scr[li, 0, rows, pr], kem)
                keep = lvl_ref[li] > 0.5
                keep = jnp.concatenate([keep, keep], axis=1)
                a = jnp.where(keep, term, 0.0 if a is None else a)
            a = a.astype(BF16)
            for par in range(2):
                cols = slice((2 * pi + par) * DV_B, (2 * pi + par + 1) * DV_B)
                o_scr[rows, cols] = _dot(a[:, par * GLA_GROUP:(par + 1) * GLA_GROUP], v[rows, cols])

    for c in range(tm // GLA_CHUNK):
        rows = slice(c * GLA_CHUNK, (c + 1) * GLA_CHUNK)
        bc = b[rows]
        b_end = bc[GLA_CHUNK - 1:GLA_CHUNK, :]
        qe = (q[rows] * jnp.exp2(bc)).astype(BF16)
        ke = (k[rows] * jnp.exp2(b_end - bc)).astype(BF16)
        st = state[...]
        qm = jnp.concatenate([jnp.where(head_cols == h, qe, jnp.zeros_like(qe))
                              for h in range(H_B)], axis=0)
        km = jnp.concatenate([jnp.where(head_cols == h, ke, jnp.zeros_like(ke))
                              for h in range(H_B)], axis=0)
        vm = jnp.concatenate([v[rows, h * DV_B:(h + 1) * DV_B] for h in range(H_B)], axis=0)
        oi = _dot_nt(qm, st.astype(BF16))
        for h in range(H_B):
            o_scr[rows, h * DV_B:(h + 1) * DV_B] += oi[h * GLA_CHUNK:(h + 1) * GLA_CHUNK]
        state[...] = st * jnp.exp2(b_end) + _dot_tn(vm, km)

    o_ref[0] = _head_norm_gate(o_scr[...], r_ref[0], gn_ref[...]).astype(BF16)
    st_ref[0] = state[...]


def _gla_prompt(q3d, k3d, v3d, g3d, r3d, gn):
    nb, S, _ = q3d.shape
    tm = min(TM_GLA, S)
    rng, _, lvl = _gla_constants()
    kern = functools.partial(_gla_prompt_kernel, tm=tm)
    tok = lambda w: pl.BlockSpec((1, tm, w), lambda b, j: (b, j, 0))
    full = lambda a: pl.BlockSpec(a.shape, lambda b, j: (0,) * a.ndim)
    return pl.pallas_call(
        kern, grid=(nb, S // tm),
        in_specs=[tok(QB_W), tok(QB_W), tok(VB_W), tok(QB_W), tok(VB_W),
                  full(gn), full(rng), full(lvl)],
        out_specs=(tok(VB_W), pl.BlockSpec((1, DV_B, QB_W), lambda b, j: (b, 0, 0))),
        out_shape=(jax.ShapeDtypeStruct((nb, S, VB_W), BF16),
                   jax.ShapeDtypeStruct((nb, DV_B, QB_W), F32)),
        scratch_shapes=[pltpu.VMEM((DV_B, QB_W), F32), pltpu.VMEM((tm, QB_W), F32),
                        pltpu.VMEM((tm, VB_W), F32),
                        pltpu.VMEM((len(GLA_LEVELS), 2, tm, QB_W), BF16)],
        compiler_params=_cparams("parallel", "arbitrary"), name="gla_prompt",
    )(q3d, k3d, v3d, g3d, r3d, gn, rng, lvl)


def _swa_bias_decode(L):
    rows = HQ_A * L
    kpad = WINDOW + SUBLANES
    hq = np.arange(rows)[:, None] // L
    t = np.arange(rows)[:, None] % L
    s = np.arange(kpad)[None, :]
    dist = WINDOW + t - s
    valid = (dist >= 0) & (dist <= WINDOW) & (s < WINDOW + L)
    slopes = 2.0 ** (-8.0 * (hq + 1) / HQ_A)
    return jnp.asarray(np.where(valid, -slopes * dist, -np.inf), dtype=F32)


def _swa_decode_kernel(q_ref, kvn_ref, ck_ref, cv_ref, bias_ref, sink_ref,
                       o_ref, nk_ref, nv_ref, qs, os_, kc, vc, kn, vn, *, ns, L, sb):
    i = pl.program_id(0)
    rows_q = HQ_A * L
    lane = lax.broadcasted_iota(jnp.int32, (1, LANES), 1)

    @pl.when(i == 0)
    def _():
        q = q_ref[...].astype(F32)
        for kv in range(HKV_A):
            sel = (lane < HD_A) if kv == 0 else (lane >= HD_A)
            for g in range(G_A):
                base = (kv * G_A + g) * L * ns
                qs[base:base + L * ns, :] = jnp.where(sel, q[:, g * LANES:(g + 1) * LANES], 0.0)
        kc[...] = jnp.zeros_like(kc)
        vc[...] = jnp.zeros_like(vc)
        kn[...] = kvn_ref[:, 0:LANES]
        vn[...] = kvn_ref[:, LANES:]

    bias = bias_ref[...]
    sink = sink_ref[...]

    def body(s, carry):
        seq = i * sb + s
        kc[0:WINDOW, :] = ck_ref[s]
        vc[0:WINDOW, :] = cv_ref[s]
        kc[WINDOW:WINDOW + L, :] = kn[pl.ds(seq, L, stride=ns), :]
        vc[WINDOW:WINDOW + L, :] = vn[pl.ds(seq, L, stride=ns), :]
        nk_ref[s] = kc[L:L + WINDOW, :]
        nv_ref[s] = vc[L:L + WINDOW, :]
        lhs = qs[pl.ds(seq, rows_q, stride=ns), :].astype(BF16)
        sc = _dot_nt(lhs, kc[...].astype(BF16)) + bias
        m = jnp.maximum(jnp.max(sc, axis=-1, keepdims=True), sink)
        p = jnp.exp(sc - m)
        den = jnp.sum(p, axis=-1, keepdims=True) + jnp.exp(sink - m)
        res = _dot(p.astype(BF16), vc[...].astype(BF16)) / den
        half = rows_q // 2
        os_[pl.ds(seq, half, stride=ns), :] = jnp.where(lane < HD_A, res[0:half], res[half:])
        return carry

    lax.fori_loop(0, sb, body, 0)

    @pl.when(i == pl.num_programs(0) - 1)
    def _():
        for g in range(G_A):
            o_ref[:, g * LANES:(g + 1) * LANES] = os_[g * L * ns:(g + 1) * L * ns, :].astype(BF16)


def _swa_decode(q_tm, kvn_tm, cache_k, cache_v, sinks, *, ns, L):
    sb = SEQ_BLOCK
    bias = _swa_bias_decode(L)
    sink_col = jnp.broadcast_to(jnp.repeat(sinks.astype(F32), L)[:, None], (HQ_A * L, 1))
    kern = functools.partial(_swa_decode_kernel, ns=ns, L=L, sb=sb)
    full = lambda a: pl.BlockSpec(a.shape, lambda i: (0,) * a.ndim)
    cache = pl.BlockSpec((sb, WINDOW, LANES), lambda i: (i, 0, 0))
    kpad = WINDOW + SUBLANES
    return pl.pallas_call(
        kern, grid=(ns // sb,),
        in_specs=[full(q_tm), full(kvn_tm), cache, cache, full(bias), full(sink_col)],
        out_specs=(pl.BlockSpec((L * ns, QA_W), lambda i: (0, 0)), cache, cache),
        out_shape=(jax.ShapeDtypeStruct((L * ns, QA_W), BF16),
                   jax.ShapeDtypeStruct(cache_k.shape, F32),
                   jax.ShapeDtypeStruct(cache_v.shape, F32)),
        scratch_shapes=[pltpu.VMEM((HQ_A * L * ns, LANES), F32),
                        pltpu.VMEM((G_A * L * ns, LANES), F32),
                        pltpu.VMEM((kpad, LANES), F32), pltpu.VMEM((kpad, LANES), F32),
                        pltpu.VMEM((L * ns, LANES), F32), pltpu.VMEM((L * ns, LANES), F32)],
        compiler_params=_cparams("arbitrary"), name="swa_decode",
    )(q_tm, kvn_tm, cache_k, cache_v, bias, sink_col)


def _gla_decode_kernel(q_ref, k_ref, v_ref, g_ref, r_ref, gn_ref, ones_ref, s0_ref,
                       o_ref, s1_ref, qe2, ke2, v2, dec3, oi2, od, *, ns, L, sb):
    i = pl.program_id(0)
    lane = lax.broadcasted_iota(jnp.int32, (1, LANES), 1)
    npair = H_B // 2

    @pl.when(i == 0)
    def _():
        slab = lambda a, t: a[t * ns:(t + 1) * ns, :]
        q, k, g = q_ref[...], k_ref[...], g_ref[...]
        vf = v_ref[...].astype(F32)
        b = [slab(g, 0)]
        for t in range(1, L):
            b.append(b[-1] + slab(g, t))
        for t in range(L):
            acc = jnp.zeros((ns, VB_W), F32)
            for jj in range(t + 1):
                p = (slab(q, t) * slab(k, jj) * jnp.exp(b[t] - b[jj])).astype(BF16)
                acc = acc + _dot(p, ones_ref[...]) * slab(vf, jj)
            od[t * ns:(t + 1) * ns, :] = acc
            qe = slab(q, t) * jnp.exp(b[t])
            ke = slab(k, t) * jnp.exp(b[L - 1] - b[t])
            for par in range(2):
                sel = (lane < DK_B) if par == 0 else (lane >= DK_B)
                base = (par * L + t) * ns
                for pi in range(npair):
                    pr = slice(pi * LANES, (pi + 1) * LANES)
                    qe2[pi, base:base + ns, :] = jnp.where(sel, qe[:, pr], 0.0)
                    ke2[pi, base:base + ns, :] = jnp.where(sel, ke[:, pr], 0.0)
                    h = 2 * pi + par
                    v2[pi, base:base + ns, :] = slab(vf, t)[:, h * DV_B:(h + 1) * DV_B]
        dec3[...] = jnp.zeros_like(dec3)
        hi, mid, lo = _split3(jnp.exp(b[L - 1]))
        for pi in range(npair):
            pr = slice(pi * LANES, (pi + 1) * LANES)
            dec3[pi, 0:ns, :] = hi[:, pr].astype(F32)
            dec3[pi, ns:2 * ns, :] = mid[:, pr].astype(F32)
            dec3[pi, 2 * ns:3 * ns, :] = lo[:, pr].astype(F32)

    ones8 = jnp.ones((2 * L, LANES), BF16)

    def body(s, carry):
        seq = i * sb + s
        st = s0_ref[s]
        for pi in range(npair):
            take = lambda ref: ref[pi, pl.ds(seq, 2 * L, stride=ns), :].astype(BF16)
            st_p = st[pi * LANES:(pi + 1) * LANES, :]
            oi2[pi, pl.ds(seq, 2 * L, stride=ns), :] = _dot(take(qe2), st_p.astype(BF16))
            dcol = _dot_tn(take(dec3), ones8)
            upd = _dot_tn(take(ke2), take(v2))
            s1_ref[s, pi * LANES:(pi + 1) * LANES, :] = dcol * st_p + upd
        return carry

    lax.fori_loop(0, sb, body, 0)

    @pl.when(i == pl.num_programs(0) - 1)
    def _():
        for t in range(L):
            parts = []
            for h in range(H_B):
                base = ((h % 2) * L + t) * ns
                parts.append(oi2[h // 2, base:base + ns, :])
            o = jnp.concatenate(parts, axis=-1) + od[t * ns:(t + 1) * ns, :]
            o_ref[t * ns:(t + 1) * ns, :] = _head_norm_gate(
                o, r_ref[t * ns:(t + 1) * ns, :], gn_ref[...]).astype(BF16)


def _gla_decode(q_tm, k_tm, v_tm, g_tm, r_tm, gn, s0, *, ns, L):
    sb = SEQ_BLOCK
    assert 2 * L == SUBLANES
    _, ones_bd, _ = _gla_constants()
    kern = functools.partial(_gla_decode_kernel, ns=ns, L=L, sb=sb)
    full = lambda a: pl.BlockSpec(a.shape, lambda i: (0,) * a.ndim)
    st = pl.BlockSpec((sb, QB_W, DV_B), lambda i: (i, 0, 0))
    rows2 = 2 * L * ns
    return pl.pallas_call(
        kern, grid=(ns // sb,),
        in_specs=[full(q_tm), full(k_tm), full(v_tm), full(g_tm), full(r_tm), full(gn),
                  full(ones_bd), st],
        out_specs=(pl.BlockSpec((L * ns, VB_W), lambda i: (0, 0)), st),
        out_shape=(jax.ShapeDtypeStruct((L * ns, VB_W), BF16),
                   jax.ShapeDtypeStruct(s0.shape, F32)),
        scratch_shapes=[pltpu.VMEM((H_B // 2, rows2, LANES), F32) for _ in range(5)]
        + [pltpu.VMEM((L * ns, VB_W), F32)],
        compiler_params=_cparams("arbitrary"), name="gla_decode",
    )(q_tm, k_tm, v_tm, g_tm, r_tm, gn, ones_bd, s0)


def _qa_perm():
    return np.asarray([(kv * G_A + g) * HD_A + d
                       for g in range(G_A) for kv in range(HKV_A) for d in range(HD_A)])


def _prep_even(w_in, w_gate_up, b_gate, w_out):
    perm = _qa_perm()
    w_main = jnp.concatenate([w_in[:, :QA_W][:, perm], w_in[:, QA_W:MAIN_W]], axis=1).astype(BF16)
    w_g = jnp.pad(w_in[:, MAIN_W:], ((0, 0), (0, LANES - GATE_RANK))).astype(BF16)
    w_gu = jnp.pad(w_gate_up, ((0, LANES - GATE_RANK), (0, 0))).astype(BF16)
    w_o = jnp.concatenate([w_out[:QA_W][perm], w_out[QA_W:]], axis=0).astype(BF16)
    return w_main, w_g, w_gu, b_gate.reshape(1, -1), w_o


def _row(v):
    return v.reshape(1, -1)


def kernel(x_prompt, x_sample, cache_swa_k, cache_swa_v, state_gla, state_conv, state_ffn,
           norm_mix_pre, norm_mix_post, norm_ffn_pre, norm_ffn_post, w_in_even, w_gate_up, b_gate,
           attn_sinks, gla_norm, w_out_even, w_in_odd, conv_w_odd, w_out_odd, ffn_up, ffn_conv_w,
           ffn_conv_b, ffn_down):
    nb, S, _ = x_prompt.shape
    ns, L, _ = x_sample.shape
    depth = norm_mix_pre.shape[0]

    xp = x_prompt
    xs = x_sample.transpose(1, 0, 2).reshape(1, L * ns, D_MODEL)
    past_p = max(SUBLANES, 2)
    past_s = 2 * ns
    tm_p = min(TM_TOK, S)

    w_up_all = ffn_up.astype(BF16)
    w_dn_all = ffn_down.astype(BF16)

    ks_p, vs_p, gs_p, cs_p, fs_p = [], [], [], [], []
    ks_s, vs_s, gs_s, cs_s, fs_s = [], [], [], [], []

    for l in range(depth):
        gpre, gpost = _row(norm_mix_pre[l]), _row(norm_mix_post[l])
        if l % 2 == 0:
            e = l // 2
            w_main, w_g, w_gu, b_g, w_o = _prep_even(w_in_even[e], w_gate_up[e], b_gate[e],
                                                     w_out_even[e])
            gn = _row(gla_norm[e])
            qa, kv, qb, kb, vb, rb, gb = _inproj_even(xp.reshape(nb * S, D_MODEL), gpre,
                                                      w_main, w_g, w_gu, b_g)
            r3 = lambda a: a.reshape(nb, S, a.shape[-1])
            oa = _swa_prompt(r3(qa), r3(kv), attn_sinks[e])
            ob, st_t = _gla_prompt(r3(qb), r3(kb), r3(vb), r3(gb), r3(rb), gn)
            mix_p = ((oa, 0), (ob, 0))
            kv_last = r3(kv)[:, S - WINDOW:, :]
            ks_p.append(kv_last[..., :LANES].reshape(nb, WINDOW, HKV_A, HD_A))
            vs_p.append(kv_last[..., LANES:].reshape(nb, WINDOW, HKV_A, HD_A))
            gs_p.append(st_t.transpose(0, 2, 1).reshape(nb, H_B, DK_B, DV_B))
            qa, kv, qb, kb, vb, rb, gb = _inproj_even(xs.reshape(L * ns, D_MODEL), gpre,
                                                      w_main, w_g, w_gu, b_g)
            oa, nk, nv = _swa_decode(qa, kv, cache_swa_k[e].reshape(ns, WINDOW, LANES),
                                     cache_swa_v[e].reshape(ns, WINDOW, LANES), attn_sinks[e],
                                     ns=ns, L=L)
            ob, s1 = _gla_decode(qb, kb, vb, gb, rb, gn, state_gla[e].reshape(ns, QB_W, DV_B),
                                 ns=ns, L=L)
            mix_s = ((oa.reshape(1, L * ns, QA_W), 0), (ob.reshape(1, L * ns, VB_W), 0))
            ks_s.append(nk.reshape(ns, WINDOW, HKV_A, HD_A))
            vs_s.append(nv.reshape(ns, WINDOW, HKV_A, HD_A))
            gs_s.append(s1.reshape(ns, H_B, DK_B, DV_B))
        else:
            o = l // 2
            w_in = w_in_odd[o].astype(BF16)
            w_o = w_out_odd[o].astype(BF16)
            y, st = _odd_in(xp, jnp.zeros((nb, past_p, D_MODEL), F32), gpre, w_in, conv_w_odd[o],
                            tm=tm_p, rs=1)
            mix_p = ((y, 0), (y, 1))
            cs_p.append(st[:, past_p - 2:, :])
            past = state_conv[o].transpose(1, 0, 2).reshape(1, past_s, D_MODEL)
            y, st = _odd_in(xs, past, gpre, w_in, conv_w_odd[o], tm=L * ns, rs=ns)
            mix_s = ((y, 0), (y, 1))
            cs_s.append(st.reshape(2, ns, D_MODEL).transpose(1, 0, 2))

        gmix = gpost
        gpre, gpost = _row(norm_ffn_pre[l]), _row(norm_ffn_post[l])
        cb = _row(ffn_conv_b[l])
        xp, st = _ffn(xp, mix_p[0], mix_p[1], w_o, gmix, jnp.zeros((nb, past_p, F2), F32), gpre,
                      w_up_all, ffn_conv_w[l], cb, w_dn_all, gpost, l, tm=tm_p, rs=1)
        fs_p.append(st[:, past_p - 2:, :])
        past = state_ffn[l].transpose(1, 0, 2).reshape(1, past_s, F2)
        xs, st = _ffn(xs, mix_s[0], mix_s[1], w_o, gmix, past, gpre, w_up_all, ffn_conv_w[l], cb,
                      w_dn_all, gpost, l, tm=L * ns, rs=ns)
        fs_s.append(st.reshape(2, ns, F2).transpose(1, 0, 2))

    y_sample = xs.reshape(L, ns, D_MODEL).transpose(1, 0, 2)
    return (xp, y_sample, jnp.stack(ks_p), jnp.stack(vs_p), jnp.stack(gs_p), jnp.stack(cs_p),
            jnp.stack(fs_p), jnp.stack(ks_s), jnp.stack(vs_s), jnp.stack(gs_s), jnp.stack(cs_s),
            jnp.stack(fs_s))
```

```python
import functools

import numpy as np
import jax
import jax.numpy as jnp
from jax import lax
from jax.experimental import pallas as pl
from jax.experimental.pallas import tpu as pltpu

F32 = jnp.float32
BF16 = jnp.bfloat16

D_MODEL = 1024
WINDOW = 128
HD_A = 64
HQ_A = 8
HKV_A = 2
G_A = HQ_A // HKV_A
H_B = 4
DK_B = 64
DV_B = 128
GATE_RANK = 16
GATE_TAU = 16.0
GLA_CHUNK = 64
D_FF = ((8 * D_MODEL // 3 + 127) // 128) * 128
F2 = 2 * D_FF
EPS = 1e-6
GELU_C = float(np.sqrt(2.0 / np.pi))
GELU_A = 0.044715

QA_W = HQ_A * HD_A
KV_W = 2 * HKV_A * HD_A
QB_W = H_B * DK_B
VB_W = H_B * DV_B
MAIN_W = QA_W + KV_W + 2 * QB_W + 2 * VB_W

LANES = 128
SUBLANES = 8
VMEM_LIMIT = 56 * 1024 * 1024

TM_TOK = 512
TM_GLA = 512
TQ_SWA = 512
GLA_GROUP = 128
FFN_TF = 256
FFN_ELEM_DTYPE = BF16
SEQ_BLOCK = 16
DEC_UNROLL = 8
DEC_KEYS = 2 * WINDOW


def _cparams(*sem):
    return pltpu.CompilerParams(dimension_semantics=sem, vmem_limit_bytes=VMEM_LIMIT)


def _rms(x, g):
    return x * lax.rsqrt(jnp.mean(x * x, axis=-1, keepdims=True) + EPS) * g


def _dot(a, b):
    return jnp.dot(a, b, preferred_element_type=F32)


def _dot_nt(a, b):
    return lax.dot_general(a, b, (((1,), (1,)), ((), ())), preferred_element_type=F32)


def _dot_tn(a, b):
    return lax.dot_general(a, b, (((0,), (0,)), ((), ())), preferred_element_type=F32)


def _split3(x):
    hi = x.astype(BF16)
    r1 = x - hi.astype(F32)
    mid = r1.astype(BF16)
    lo = (r1 - mid.astype(F32)).astype(BF16)
    return hi, mid, lo


def _inproj_even_kernel(x_ref, gpre_ref, w_ref, wg_ref, wgu_ref, bg_ref,
                        qa_ref, kv_ref, qb_ref, kb_ref, vb_ref, rb_ref, gb_ref):
    h = _rms(x_ref[...], gpre_ref[...]).astype(BF16)

    def mm(lo, width):
        return _dot(h, w_ref[:, lo:lo + width])

    lo = 0
    qa_ref[...] = (mm(lo, QA_W) * (HD_A ** -0.5)).astype(BF16)
    lo += QA_W
    kv_ref[...] = mm(lo, KV_W)
    lo += KV_W
    qb_ref[...] = mm(lo, QB_W) * (DK_B ** -0.5)
    lo += QB_W
    kb_ref[...] = mm(lo, QB_W)
    lo += QB_W
    vb_ref[...] = mm(lo, VB_W).astype(BF16)
    lo += VB_W
    rb_ref[...] = mm(lo, VB_W)
    glr = _dot(h, wg_ref[...]).astype(BF16)
    z = _dot(glr, wgu_ref[...]) + bg_ref[...]
    gb_ref[...] = jax.nn.log_sigmoid(z) * (1.0 / GATE_TAU)


def _inproj_even(x2d, gpre, w_main, w_g, w_gu, b_g):
    T = x2d.shape[0]
    tm = min(TM_TOK, T)
    row = lambda w: pl.BlockSpec((tm, w), lambda i: (i, 0))
    full = lambda a: pl.BlockSpec(a.shape, lambda i: (0,) * a.ndim)
    out_shape = (
        jax.ShapeDtypeStruct((T, QA_W), BF16), jax.ShapeDtypeStruct((T, KV_W), F32),
        jax.ShapeDtypeStruct((T, QB_W), F32), jax.ShapeDtypeStruct((T, QB_W), F32),
        jax.ShapeDtypeStruct((T, VB_W), BF16), jax.ShapeDtypeStruct((T, VB_W), F32),
        jax.ShapeDtypeStruct((T, QB_W), F32))
    return pl.pallas_call(
        _inproj_even_kernel, grid=(T // tm,),
        in_specs=[row(D_MODEL), full(gpre), full(w_main), full(w_g), full(w_gu), full(b_g)],
        out_specs=(row(QA_W), row(KV_W), row(QB_W), row(QB_W), row(VB_W), row(VB_W), row(QB_W)),
        out_shape=out_shape, compiler_params=_cparams("parallel"), name="inproj_even",
    )(x2d, gpre, w_main, w_g, w_gu, b_g)


def _conv3_from_buf(buf, cw, cur, tm, past, rs, dtype=F32):
    cw = cw.astype(dtype)
    y = cw[0:1, :] * buf[past - 2 * rs:past - 2 * rs + tm, :].astype(dtype)
    y = y + cw[1:2, :] * buf[past - rs:past - rs + tm, :].astype(dtype)
    return y + cw[2:3, :] * cur.astype(dtype)


def _odd_in_kernel(x_ref, past_ref, gpre_ref, win_ref, cw_ref, y_ref, st_ref, carry, buf,
                   *, tm, past, rs, tc):
    @pl.when(pl.program_id(1) == 0)
    def _():
        carry[...] = past_ref[0]

    h = _rms(x_ref[0], gpre_ref[...]).astype(BF16)
    for c in range(D_MODEL // tc):
        lo = c * tc
        bg = _dot(h, win_ref[:, lo:lo + tc])
        cu = _dot(h, win_ref[:, D_MODEL + lo:D_MODEL + lo + tc]) * \
            _dot(h, win_ref[:, 2 * D_MODEL + lo:2 * D_MODEL + lo + tc])
        buf[0:past, :] = carry[:, lo:lo + tc]
        buf[past:past + tm, :] = cu
        carry[:, lo:lo + tc] = buf[tm:tm + past, :]
        z = _conv3_from_buf(buf, cw_ref[:, lo:lo + tc], cu, tm, past, rs)
        y_ref[0, :, lo:lo + tc] = (bg * z).astype(BF16)
    st_ref[0] = carry[...]


def _odd_in(x3d, past0, gpre, w_in, conv_w, *, tm, rs):
    nb, rows, _ = x3d.shape
    past = past0.shape[1]
    tc = 512
    kern = functools.partial(_odd_in_kernel, tm=tm, past=past, rs=rs, tc=tc)
    full = lambda a: pl.BlockSpec(a.shape, lambda b, j: (0,) * a.ndim)
    return pl.pallas_call(
        kern, grid=(nb, rows // tm),
        in_specs=[pl.BlockSpec((1, tm, D_MODEL), lambda b, j: (b, j, 0)),
                  pl.BlockSpec((1, past, D_MODEL), lambda b, j: (b, 0, 0)),
                  full(gpre), full(w_in), full(conv_w)],
        out_specs=(pl.BlockSpec((1, tm, D_MODEL), lambda b, j: (b, j, 0)),
                   pl.BlockSpec((1, past, D_MODEL), lambda b, j: (b, 0, 0))),
        out_shape=(jax.ShapeDtypeStruct((nb, rows, D_MODEL), BF16),
                   jax.ShapeDtypeStruct((nb, past, D_MODEL), F32)),
        scratch_shapes=[pltpu.VMEM((past, D_MODEL), F32), pltpu.VMEM((past + tm, tc), F32)],
        compiler_params=_cparams("parallel", "arbitrary"), name="odd_in",
    )(x3d, past0, gpre, w_in, conv_w)


def _ffn_kernel(x_ref, ya_ref, yb_ref, wo_ref, gmix_ref, past_ref, gpre_ref, wup_ref, cw_ref,
                cb_ref, wdn_ref, gpost_ref, gk_ref, o_ref, st_ref,
                carry, buf_g, buf_v, acc, h_scr, act, x1_scr, *, tm, past, rs, tf):
    @pl.when(pl.program_id(1) == 0)
    def _():
        carry[...] = past_ref[0]

    mix = _dot(ya_ref[0], wo_ref[0]) + _dot(yb_ref[0], wo_ref[1])
    x1_scr[...] = x_ref[0] + _rms(mix, gmix_ref[...])
    h_scr[...] = _rms(x1_scr[...], gpre_ref[...]).astype(BF16)
    nf = D_FF // tf

    def up(c):
        for lo, buf in ((c * tf, buf_g), (D_FF + c * tf, buf_v)):
            buf[c % 2, 0:past, :] = carry[:, lo:lo + tf]
            buf[c % 2, past:past + tm, :] = _dot(h_scr[...], wup_ref[:, lo:lo + tf])
            carry[:, lo:lo + tf] = buf[c % 2, tm:tm + past, :]

    gelu_c = gk_ref[0:1, 0:1].astype(FFN_ELEM_DTYPE)
    gelu_ca = gk_ref[0:1, 1:2].astype(FFN_ELEM_DTYPE)

    def elem(c):
        halves = []
        for lo, buf, scale in ((c * tf, buf_g, 1.0), (D_FF + c * tf, buf_v, 0.5)):
            b = buf.at[c % 2]
            halves.append(_conv3_from_buf(b, cw_ref[:, lo:lo + tf] * scale, b[past:past + tm, :],
                                          tm, past, rs, FFN_ELEM_DTYPE)
                          + (cb_ref[:, lo:lo + tf] * scale).astype(FFN_ELEM_DTYPE))
        x, half_v = halves
        t = jnp.tanh(x * (x * x * gelu_ca + gelu_c))
        act[c % 2] = ((x * t + x) * half_v).astype(BF16)

    def down(c):
        part = _dot(act[c % 2], wdn_ref[c * tf:(c + 1) * tf, :])
        if c == 0:
            acc[...] = part
        else:
            acc[...] += part

    for c in range(nf + 2):
        if c < nf:
            up(c)
        if 1 <= c <= nf:
            elem(c - 1)
        if c >= 2:
            down(c - 2)
    o_ref[0] = x1_scr[...] + _rms(acc[...], gpost_ref[...])
    st_ref[0] = carry[...]


def _ffn(x3d, ya, yb, w_o, gmix, past0, gpre, w_up, conv_w, conv_b, w_dn, gpost, layer, *, tm, rs):
    nb, rows, _ = x3d.shape
    past = past0.shape[1]
    half = w_o.shape[0] // 2
    kern = functools.partial(_ffn_kernel, tm=tm, past=past, rs=rs, tf=FFN_TF)
    gelu_k = jnp.zeros((1, LANES), F32).at[0, 0].set(GELU_C).at[0, 1].set(GELU_C * GELU_A)
    w_o2 = w_o.reshape(2, half, D_MODEL)
    once = dict(pipeline_mode=pl.Buffered(1))
    full = lambda a: pl.BlockSpec(a.shape, lambda b, j: (0,) * a.ndim, **once)
    per_layer = lambda a: pl.BlockSpec((None,) + a.shape[1:], lambda b, j: (layer, 0, 0), **once)
    tok = lambda w, cblk=0: pl.BlockSpec((1, tm, w), lambda b, j: (b, j, cblk))
    return pl.pallas_call(
        kern, grid=(nb, rows // tm),
        in_specs=[tok(D_MODEL), tok(half, ya[1]), tok(half, yb[1]), full(w_o2), full(gmix),
                  pl.BlockSpec((1, past, F2), lambda b, j: (b, 0, 0)),
                  full(gpre), per_layer(w_up), full(conv_w), full(conv_b), per_layer(w_dn),
                  full(gpost), full(gelu_k)],
        out_specs=(tok(D_MODEL), pl.BlockSpec((1, past, F2), lambda b, j: (b, 0, 0))),
        out_shape=(jax.ShapeDtypeStruct((nb, rows, D_MODEL), F32),
                   jax.ShapeDtypeStruct((nb, past, F2), F32)),
        scratch_shapes=[pltpu.VMEM((past, F2), F32), pltpu.VMEM((2, past + tm, FFN_TF), F32),
                        pltpu.VMEM((2, past + tm, FFN_TF), F32), pltpu.VMEM((tm, D_MODEL), F32),
                        pltpu.VMEM((tm, D_MODEL), BF16), pltpu.VMEM((2, tm, FFN_TF), BF16),
                        pltpu.VMEM((tm, D_MODEL), F32)],
        compiler_params=_cparams("parallel", "arbitrary"), name="conv_ffn",
    )(x3d, ya[0], yb[0], w_o2, gmix, past0, gpre, w_up, conv_w, conv_b, w_dn, gpost, gelu_k)


def _swa_bias_prompt():
    qi = np.arange(WINDOW)[:, None]
    sj = np.arange(2 * WINDOW)[None, :]
    dist = WINDOW + qi - sj
    valid = (dist >= 0) & (dist <= WINDOW)
    slopes = 2.0 ** (-8.0 * np.arange(1, HQ_A + 1) / HQ_A)
    bias = np.where(valid[None], -slopes[:, None, None] * dist[None].astype(np.float64), -np.inf)
    return jnp.asarray(bias, dtype=F32)


def _swa_prompt_kernel(sink_ref, q_ref, kvc_ref, kvp_ref, bias_ref, o_ref, *, tq):
    i = pl.program_id(1)
    kvc = kvc_ref[0]
    kvp = kvp_ref[0]
    kcat = jnp.concatenate([kvp[:, 0:LANES], kvc[:, 0:LANES]], axis=0).astype(BF16)
    vcat = jnp.concatenate([kvp[:, LANES:], kvc[:, LANES:]], axis=0).astype(BF16)
    lane = lax.broadcasted_iota(jnp.int32, (WINDOW, LANES), 1)
    col = lax.broadcasted_iota(jnp.int32, (WINDOW, 2 * WINDOW), 1)
    for j in range(tq // WINDOW):
        keys = kcat[j * WINDOW:(j + 2) * WINDOW, :]
        vals = vcat[j * WINDOW:(j + 2) * WINDOW, :]
        for g in range(G_A):
            q2 = q_ref[0, j * WINDOW:(j + 1) * WINDOW, g * LANES:(g + 1) * LANES]
            outs = []
            for kv in range(HKV_A):
                hq = kv * G_A + g
                in_head = (lane < HD_A) if kv == 0 else (lane >= HD_A)
                qm = jnp.where(in_head, q2, jnp.zeros_like(q2))
                s = _dot_nt(qm, keys) + bias_ref[hq]
                if j == 0:
                    s = jnp.where(jnp.logical_and(i == 0, col < WINDOW), -jnp.inf, s)
                sink = sink_ref[hq]
                m = jnp.maximum(jnp.max(s, axis=-1, keepdims=True), sink)
                p = jnp.exp(s - m)
                den = jnp.sum(p, axis=-1, keepdims=True) + jnp.exp(sink - m)
                outs.append(_dot(p.astype(BF16), vals) / den)
            o2 = jnp.where(lane < HD_A, outs[0], outs[1])
            o_ref[0, j * WINDOW:(j + 1) * WINDOW, g * LANES:(g + 1) * LANES] = o2.astype(BF16)


def _swa_prompt(q3d, kv3d, sinks):
    nb, S, _ = q3d.shape
    tq = min(TQ_SWA, S)
    bias = _swa_bias_prompt()
    kern = functools.partial(_swa_prompt_kernel, tq=tq)
    blocks_per_tile = tq // WINDOW
    return pl.pallas_call(
        kern, grid=(nb, S // tq),
        in_specs=[pl.BlockSpec(memory_space=pltpu.SMEM),
                  pl.BlockSpec((1, tq, QA_W), lambda b, i: (b, i, 0)),
                  pl.BlockSpec((1, tq, KV_W), lambda b, i: (b, i, 0)),
                  pl.BlockSpec((1, WINDOW, KV_W),
                               lambda b, i: (b, jnp.maximum(i * blocks_per_tile - 1, 0), 0)),
                  pl.BlockSpec(bias.shape, lambda b, i: (0, 0, 0))],
        out_specs=pl.BlockSpec((1, tq, QA_W), lambda b, i: (b, i, 0)),
        out_shape=jax.ShapeDtypeStruct((nb, S, QA_W), BF16),
        compiler_params=_cparams("parallel", "arbitrary"), name="swa_prompt",
    )(sinks, q3d, kv3d, kv3d, bias)


GLA_LEVELS = (0, 1, 2, 4, 8, 16, 32)
LOG2E = 1.4426950408889634


def _gla_constants():
    r = np.arange(GLA_GROUP)
    i, j = r[:, None], r[None, :]
    ranges = [(i >= j) & (i // GLA_CHUNK == j // GLA_CHUNK)]
    masks = [i == j]
    for s in GLA_LEVELS[1:]:
        mid = (i // (2 * s)) * (2 * s) + s
        upper = (i % (2 * s) >= s) & (j >= mid) & (j <= i)
        lower = (i % (2 * s) < s) & (j > i) & (j < mid)
        ranges.append(upper | lower)
        masks.append((i // (2 * s) == j // (2 * s)) & (i % (2 * s) >= s) & (j % (2 * s) < s))
    ones_bd = (np.arange(QB_W)[:, None] // DK_B) == (np.arange(VB_W)[None, :] // DV_B)
    return (jnp.asarray(np.concatenate(ranges, axis=0), dtype=BF16),
            jnp.asarray(ones_bd, dtype=BF16), jnp.asarray(np.stack(masks), dtype=F32))


def _head_norm_gate(o, r, gn):
    parts = []
    for h in range(H_B):
        oh = o[:, h * DV_B:(h + 1) * DV_B]
        parts.append(oh * lax.rsqrt(jnp.mean(oh * oh, axis=-1, keepdims=True) + EPS))
    return jnp.concatenate(parts, axis=-1) * gn * jax.nn.silu(r)


def _gla_prompt_kernel(q_ref, k_ref, v_ref, g_ref, r_ref, gn_ref, rng_ref, lvl_ref,
                       o_ref, st_ref, state, b_scr, o_scr, qk_scr, *, tm):
    @pl.when(pl.program_id(1) == 0)
    def _():
        state[...] = jnp.zeros_like(state)

    q = q_ref[0]
    k = k_ref[0]
    v = v_ref[0]
    lane = lax.broadcasted_iota(jnp.int32, (1, LANES), 1)
    head_lanes = (lane < DK_B, lane >= DK_B)
    head_cols = lax.broadcasted_iota(jnp.int32, (1, QB_W), 1) // DK_B

    g2 = g_ref[0] * LOG2E
    ghi = g2.astype(BF16)
    glo = (g2 - ghi.astype(F32)).astype(BF16)
    nlev = len(GLA_LEVELS)
    for gi in range(tm // GLA_GROUP):
        rows = slice(gi * GLA_GROUP, (gi + 1) * GLA_GROUP)
        d_all = _dot(rng_ref[...], ghi[rows]) + _dot(rng_ref[...], glo[rows])
        b_scr[rows, :] = d_all[0:GLA_GROUP]
        qk_scr[0, 0, rows, :] = q[rows].astype(BF16)
        qk_scr[0, 1, rows, :] = k[rows].astype(BF16)
        for li in range(1, nlev):
            e = jnp.exp2(d_all[li * GLA_GROUP:(li + 1) * GLA_GROUP])
            qk_scr[li, 0, rows, :] = (q[rows] * e).astype(BF16)
            qk_scr[li, 1, rows, :] = (k[rows] * e).astype(BF16)
    b = b_scr[...]

    for gi in range(tm // GLA_GROUP):
        rows = slice(gi * GLA_GROUP, (gi + 1) * GLA_GROUP)
        for pi in range(H_B // 2):
            pr = slice(pi * LANES, (pi + 1) * LANES)
            a = None
            for li in range(nlev):
                ke = qk_scr[li, 1, rows, pr]
                kem = jnp.concatenate([jnp.where(sel, ke, jnp.zeros_like(ke)) for sel in head_lanes],
                                      axis=0)
                term = _dot_nt(qk_scr[li, 0, rows, pr], kem)
                keep = lvl_ref[li] > 0.5
                keep = jnp.concatenate([keep, keep], axis=1)
                a = jnp.where(keep, term, 0.0 if a is None else a)
            a = a.astype(BF16)
            for par in range(2):
                cols = slice((2 * pi + par) * DV_B, (2 * pi + par + 1) * DV_B)
                o_scr[rows, cols] = _dot(a[:, par * GLA_GROUP:(par + 1) * GLA_GROUP], v[rows, cols])

    for c in range(tm // GLA_CHUNK):
        rows = slice(c * GLA_CHUNK, (c + 1) * GLA_CHUNK)
        bc = b[rows]
        b_end = bc[GLA_CHUNK - 1:GLA_CHUNK, :]
        qe = (q[rows] * jnp.exp2(bc)).astype(BF16)
        ke = (k[rows] * jnp.exp2(b_end - bc)).astype(BF16)
        st = state[...]
        qm = jnp.concatenate([jnp.where(head_cols == h, qe, jnp.zeros_like(qe))
                              for h in range(H_B)], axis=0)
        km = jnp.concatenate([jnp.where(head_cols == h, ke, jnp.zeros_like(ke))
                              for h in range(H_B)], axis=0)
        vm = jnp.concatenate([v[rows, h * DV_B:(h + 1) * DV_B] for h in range(H_B)], axis=0)
        oi = _dot_nt(qm, st.astype(BF16))
        for h in range(H_B):
            o_scr[rows, h * DV_B:(h + 1) * DV_B] += oi[h * GLA_CHUNK:(h + 1) * GLA_CHUNK]
        state[...] = st * jnp.exp2(b_end) + _dot_tn(vm, km)

    o_ref[0] = _head_norm_gate(o_scr[...], r_ref[0], gn_ref[...]).astype(BF16)
    st_ref[0] = state[...]


def _gla_prompt(q3d, k3d, v3d, g3d, r3d, gn):
    nb, S, _ = q3d.shape
    tm = min(TM_GLA, S)
    rng, _, lvl = _gla_constants()
    kern = functools.partial(_gla_prompt_kernel, tm=tm)
    tok = lambda w: pl.BlockSpec((1, tm, w), lambda b, j: (b, j, 0))
    full = lambda a: pl.BlockSpec(a.shape, lambda b, j: (0,) * a.ndim)
    return pl.pallas_call(
        kern, grid=(nb, S // tm),
        in_specs=[tok(QB_W), tok(QB_W), tok(VB_W), tok(QB_W), tok(VB_W),
                  full(gn), full(rng), full(lvl)],
        out_specs=(tok(VB_W), pl.BlockSpec((1, DV_B, QB_W), lambda b, j: (b, 0, 0))),
        out_shape=(jax.ShapeDtypeStruct((nb, S, VB_W), BF16),
                   jax.ShapeDtypeStruct((nb, DV_B, QB_W), F32)),
        scratch_shapes=[pltpu.VMEM((DV_B, QB_W), F32), pltpu.VMEM((tm, QB_W), F32),
                        pltpu.VMEM((tm, VB_W), F32),
                        pltpu.VMEM((len(GLA_LEVELS), 2, tm, QB_W), BF16)],
        compiler_params=_cparams("parallel", "arbitrary"), name="gla_prompt",
    )(q3d, k3d, v3d, g3d, r3d, gn, rng, lvl)


def _swa_bias_decode(L):
    rows = HQ_A * L
    kpad = DEC_KEYS
    hq = np.arange(rows)[:, None] // L
    t = np.arange(rows)[:, None] % L
    s = np.arange(kpad)[None, :]
    dist = WINDOW + t - s
    valid = (dist >= 0) & (dist <= WINDOW) & (s < WINDOW + L)
    slopes = 2.0 ** (-8.0 * (hq + 1) / HQ_A)
    return jnp.asarray(np.where(valid, -slopes * dist, -np.inf), dtype=F32)


def _swa_decode_kernel(q_ref, kvn_ref, ck_ref, cv_ref, bias_ref, sink_ref,
                       o_ref, nk_ref, nv_ref, qs, os_, kc, vc, kn, vn, *, ns, L, sb):
    i = pl.program_id(0)
    rows_q = HQ_A * L
    lane = lax.broadcasted_iota(jnp.int32, (1, LANES), 1)

    @pl.when(i == 0)
    def _():
        q = q_ref[...].astype(F32)
        for kv in range(HKV_A):
            sel = (lane < HD_A) if kv == 0 else (lane >= HD_A)
            for g in range(G_A):
                base = (kv * G_A + g) * L * ns
                qs[base:base + L * ns, :] = jnp.where(sel, q[:, g * LANES:(g + 1) * LANES], 0.0)
        kc[...] = jnp.zeros_like(kc)
        vc[...] = jnp.zeros_like(vc)
        kn[...] = kvn_ref[:, 0:LANES]
        vn[...] = kvn_ref[:, LANES:]

    bias = bias_ref[...]
    sink = sink_ref[...]

    def body(t, carry):
        lhs = []
        for u in range(DEC_UNROLL):
            s = t * DEC_UNROLL + u
            seq = i * sb + s
            kc[u, 0:WINDOW, :] = ck_ref[s]
            vc[u, 0:WINDOW, :] = cv_ref[s]
            kc[u, WINDOW:WINDOW + L, :] = kn[pl.ds(seq, L, stride=ns), :]
            vc[u, WINDOW:WINDOW + L, :] = vn[pl.ds(seq, L, stride=ns), :]
            nk_ref[s] = kc[u, L:L + WINDOW, :]
            nv_ref[s] = vc[u, L:L + WINDOW, :]
            lhs.append(qs[pl.ds(seq, rows_q, stride=ns), :].astype(BF16))
        sc = jnp.einsum("uqd,ukd->uqk", jnp.stack(lhs), kc[...].astype(BF16),
                        preferred_element_type=F32) + bias
        m = jnp.maximum(jnp.max(sc, axis=-1, keepdims=True), sink)
        p = jnp.exp(sc - m)
        den = jnp.sum(p, axis=-1, keepdims=True) + jnp.exp(sink - m)
        res = jnp.einsum("uqk,ukd->uqd", p.astype(BF16), vc[...].astype(BF16),
                         preferred_element_type=F32) / den
        half = rows_q // 2
        for u in range(DEC_UNROLL):
            seq = i * sb + t * DEC_UNROLL + u
            os_[pl.ds(seq, half, stride=ns), :] = jnp.where(lane < HD_A, res[u, 0:half],
                                                            res[u, half:])
        return carry

    lax.fori_loop(0, sb // DEC_UNROLL, body, 0)

    @pl.when(i == pl.num_programs(0) - 1)
    def _():
        for g in range(G_A):
            o_ref[:, g * LANES:(g + 1) * LANES] = os_[g * L * ns:(g + 1) * L * ns, :].astype(BF16)


def _swa_decode(q_tm, kvn_tm, cache_k, cache_v, sinks, *, ns, L):
    sb = SEQ_BLOCK
    bias = _swa_bias_decode(L)
    sink_col = jnp.broadcast_to(jnp.repeat(sinks.astype(F32), L)[:, None], (HQ_A * L, 1))
    kern = functools.partial(_swa_decode_kernel, ns=ns, L=L, sb=sb)
    full = lambda a: pl.BlockSpec(a.shape, lambda i: (0,) * a.ndim)
    cache = pl.BlockSpec((sb, WINDOW, LANES), lambda i: (i, 0, 0))
    kpad = DEC_KEYS
    return pl.pallas_call(
        kern, grid=(ns // sb,),
        in_specs=[full(q_tm), full(kvn_tm), cache, cache, full(bias), full(sink_col)],
        out_specs=(pl.BlockSpec((L * ns, QA_W), lambda i: (0, 0)), cache, cache),
        out_shape=(jax.ShapeDtypeStruct((L * ns, QA_W), BF16),
                   jax.ShapeDtypeStruct(cache_k.shape, F32),
                   jax.ShapeDtypeStruct(cache_v.shape, F32)),
        scratch_shapes=[pltpu.VMEM((HQ_A * L * ns, LANES), F32),
                        pltpu.VMEM((G_A * L * ns, LANES), F32),
                        pltpu.VMEM((DEC_UNROLL, kpad, LANES), F32),
                        pltpu.VMEM((DEC_UNROLL, kpad, LANES), F32),
                        pltpu.VMEM((L * ns, LANES), F32), pltpu.VMEM((L * ns, LANES), F32)],
        compiler_params=_cparams("arbitrary"), name="swa_decode",
    )(q_tm, kvn_tm, cache_k, cache_v, bias, sink_col)


def _gla_decode_kernel(q_ref, k_ref, v_ref, g_ref, r_ref, gn_ref, ones_ref, s0_ref,
                       o_ref, s1_ref, qe2, ke2, v2, dec3, oi2, od, *, ns, L, sb):
    i = pl.program_id(0)
    lane = lax.broadcasted_iota(jnp.int32, (1, LANES), 1)
    npair = H_B // 2

    @pl.when(i == 0)
    def _():
        slab = lambda a, t: a[t * ns:(t + 1) * ns, :]
        q, k, g = q_ref[...], k_ref[...], g_ref[...]
        vf = v_ref[...].astype(F32)
        b = [slab(g, 0)]
        for t in range(1, L):
            b.append(b[-1] + slab(g, t))
        for t in range(L):
            acc = jnp.zeros((ns, VB_W), F32)
            for jj in range(t + 1):
                p = (slab(q, t) * slab(k, jj) * jnp.exp(b[t] - b[jj])).astype(BF16)
                acc = acc + _dot(p, ones_ref[...]) * slab(vf, jj)
            od[t * ns:(t + 1) * ns, :] = acc
            qe = slab(q, t) * jnp.exp(b[t])
            ke = slab(k, t) * jnp.exp(b[L - 1] - b[t])
            for par in range(2):
                sel = (lane < DK_B) if par == 0 else (lane >= DK_B)
                base = (par * L + t) * ns
                for pi in range(npair):
                    pr = slice(pi * LANES, (pi + 1) * LANES)
                    qe2[pi, base:base + ns, :] = jnp.where(sel, qe[:, pr], 0.0)
                    ke2[pi, base:base + ns, :] = jnp.where(sel, ke[:, pr], 0.0)
                    h = 2 * pi + par
                    v2[pi, base:base + ns, :] = slab(vf, t)[:, h * DV_B:(h + 1) * DV_B]
        dec3[...] = jnp.zeros_like(dec3)
        hi, mid, lo = _split3(jnp.exp(b[L - 1]))
        for pi in range(npair):
            pr = slice(pi * LANES, (pi + 1) * LANES)
            dec3[pi, 0:ns, :] = hi[:, pr].astype(F32)
            dec3[pi, ns:2 * ns, :] = mid[:, pr].astype(F32)
            dec3[pi, 2 * ns:3 * ns, :] = lo[:, pr].astype(F32)

    ones8 = jnp.ones((2 * L, LANES), BF16)

    def one_seq(s):
        seq = i * sb + s
        for pi in range(npair):
            take = lambda ref: ref[pi, pl.ds(seq, 2 * L, stride=ns), :].astype(BF16)
            st_p = s0_ref[s, pi * LANES:(pi + 1) * LANES, :]
            oi2[pi, pl.ds(seq, 2 * L, stride=ns), :] = _dot(take(qe2), st_p.astype(BF16))
            dcol = _dot_tn(take(dec3), ones8)
            upd = _dot_tn(take(ke2), take(v2))
            s1_ref[s, pi * LANES:(pi + 1) * LANES, :] = dcol * st_p + upd

    def body(t, carry):
        for u in range(DEC_UNROLL):
            one_seq(t * DEC_UNROLL + u)
        return carry

    lax.fori_loop(0, sb // DEC_UNROLL, body, 0)

    @pl.when(i == pl.num_programs(0) - 1)
    def _():
        for t in range(L):
            parts = []
            for h in range(H_B):
                base = ((h % 2) * L + t) * ns
                parts.append(oi2[h // 2, base:base + ns, :])
            o = jnp.concatenate(parts, axis=-1) + od[t * ns:(t + 1) * ns, :]
            o_ref[t * ns:(t + 1) * ns, :] = _head_norm_gate(
                o, r_ref[t * ns:(t + 1) * ns, :], gn_ref[...]).astype(BF16)


def _gla_decode(q_tm, k_tm, v_tm, g_tm, r_tm, gn, s0, *, ns, L):
    sb = SEQ_BLOCK
    assert 2 * L == SUBLANES
    _, ones_bd, _ = _gla_constants()
    kern = functools.partial(_gla_decode_kernel, ns=ns, L=L, sb=sb)
    full = lambda a: pl.BlockSpec(a.shape, lambda i: (0,) * a.ndim)
    st = pl.BlockSpec((sb, QB_W, DV_B), lambda i: (i, 0, 0))
    rows2 = 2 * L * ns
    return pl.pallas_call(
        kern, grid=(ns // sb,),
        in_specs=[full(q_tm), full(k_tm), full(v_tm), full(g_tm), full(r_tm), full(gn),
                  full(ones_bd), st],
        out_specs=(pl.BlockSpec((L * ns, VB_W), lambda i: (0, 0)), st),
        out_shape=(jax.ShapeDtypeStruct((L * ns, VB_W), BF16),
                   jax.ShapeDtypeStruct(s0.shape, F32)),
        scratch_shapes=[pltpu.VMEM((H_B // 2, rows2, LANES), F32) for _ in range(5)]
        + [pltpu.VMEM((L * ns, VB_W), F32)],
        compiler_params=_cparams("arbitrary"), name="gla_decode",
    )(q_tm, k_tm, v_tm, g_tm, r_tm, gn, ones_bd, s0)


def _qa_perm():
    return np.asarray([(kv * G_A + g) * HD_A + d
                       for g in range(G_A) for kv in range(HKV_A) for d in range(HD_A)])


def _prep_even(w_in, w_gate_up, b_gate, w_out):
    perm = _qa_perm()
    w_main = jnp.concatenate([w_in[:, :QA_W][:, perm], w_in[:, QA_W:MAIN_W]], axis=1).astype(BF16)
    w_g = jnp.pad(w_in[:, MAIN_W:], ((0, 0), (0, LANES - GATE_RANK))).astype(BF16)
    w_gu = jnp.pad(w_gate_up, ((0, LANES - GATE_RANK), (0, 0))).astype(BF16)
    w_o = jnp.concatenate([w_out[:QA_W][perm], w_out[QA_W:]], axis=0).astype(BF16)
    return w_main, w_g, w_gu, b_gate.reshape(1, -1), w_o


def _row(v):
    return v.reshape(1, -1)


def kernel(x_prompt, x_sample, cache_swa_k, cache_swa_v, state_gla, state_conv, state_ffn,
           norm_mix_pre, norm_mix_post, norm_ffn_pre, norm_ffn_post, w_in_even, w_gate_up, b_gate,
           attn_sinks, gla_norm, w_out_even, w_in_odd, conv_w_odd, w_out_odd, ffn_up, ffn_conv_w,
           ffn_conv_b, ffn_down):
    nb, S, _ = x_prompt.shape
    ns, L, _ = x_sample.shape
    depth = norm_mix_pre.shape[0]

    xp = x_prompt
    xs = x_sample.transpose(1, 0, 2).reshape(1, L * ns, D_MODEL)
    past_p = max(SUBLANES, 2)
    past_s = 2 * ns
    tm_p = min(TM_TOK, S)

    w_up_all = ffn_up.astype(BF16)
    w_dn_all = ffn_down.astype(BF16)

    ks_p, vs_p, gs_p, cs_p, fs_p = [], [], [], [], []
    ks_s, vs_s, gs_s, cs_s, fs_s = [], [], [], [], []

    for l in range(depth):
        gpre, gpost = _row(norm_mix_pre[l]), _row(norm_mix_post[l])
        if l % 2 == 0:
            e = l // 2
            w_main, w_g, w_gu, b_g, w_o = _prep_even(w_in_even[e], w_gate_up[e], b_gate[e],
                                                     w_out_even[e])
            gn = _row(gla_norm[e])
            qa, kv, qb, kb, vb, rb, gb = _inproj_even(xp.reshape(nb * S, D_MODEL), gpre,
                                                      w_main, w_g, w_gu, b_g)
            r3 = lambda a: a.reshape(nb, S, a.shape[-1])
            oa = _swa_prompt(r3(qa), r3(kv), attn_sinks[e])
            ob, st_t = _gla_prompt(r3(qb), r3(kb), r3(vb), r3(gb), r3(rb), gn)
            mix_p = ((oa, 0), (ob, 0))
            kv_last = r3(kv)[:, S - WINDOW:, :]
            ks_p.append(kv_last[..., :LANES].reshape(nb, WINDOW, HKV_A, HD_A))
            vs_p.append(kv_last[..., LANES:].reshape(nb, WINDOW, HKV_A, HD_A))
            gs_p.append(st_t.transpose(0, 2, 1).reshape(nb, H_B, DK_B, DV_B))
            qa, kv, qb, kb, vb, rb, gb = _inproj_even(xs.reshape(L * ns, D_MODEL), gpre,
                                                      w_main, w_g, w_gu, b_g)
            oa, nk, nv = _swa_decode(qa, kv, cache_swa_k[e].reshape(ns, WINDOW, LANES),
                                     cache_swa_v[e].reshape(ns, WINDOW, LANES), attn_sinks[e],
                                     ns=ns, L=L)
            ob, s1 = _gla_decode(qb, kb, vb, gb, rb, gn, state_gla[e].reshape(ns, QB_W, DV_B),
                                 ns=ns, L=L)
            mix_s = ((oa.reshape(1, L * ns, QA_W), 0), (ob.reshape(1, L * ns, VB_W), 0))
            ks_s.append(nk.reshape(ns, WINDOW, HKV_A, HD_A))
            vs_s.append(nv.reshape(ns, WINDOW, HKV_A, HD_A))
            gs_s.append(s1.reshape(ns, H_B, DK_B, DV_B))
        else:
            o = l // 2
            w_in = w_in_odd[o].astype(BF16)
            w_o = w_out_odd[o].astype(BF16)
            y, st = _odd_in(xp, jnp.zeros((nb, past_p, D_MODEL), F32), gpre, w_in, conv_w_odd[o],
                            tm=tm_p, rs=1)
            mix_p = ((y, 0), (y, 1))
            cs_p.append(st[:, past_p - 2:, :])
            past = state_conv[o].transpose(1, 0, 2).reshape(1, past_s, D_MODEL)
            y, st = _odd_in(xs, past, gpre, w_in, conv_w_odd[o], tm=L * ns, rs=ns)
            mix_s = ((y, 0), (y, 1))
            cs_s.append(st.reshape(2, ns, D_MODEL).transpose(1, 0, 2))

        gmix = gpost
        gpre, gpost = _row(norm_ffn_pre[l]), _row(norm_ffn_post[l])
        cb = _row(ffn_conv_b[l])
        xp, st = _ffn(xp, mix_p[0], mix_p[1], w_o, gmix, jnp.zeros((nb, past_p, F2), F32), gpre,
                      w_up_all, ffn_conv_w[l], cb, w_dn_all, gpost, l, tm=tm_p, rs=1)
        fs_p.append(st[:, past_p - 2:, :])
        past = state_ffn[l].transpose(1, 0, 2).reshape(1, past_s, F2)
        xs, st = _ffn(xs, mix_s[0], mix_s[1], w_o, gmix, past, gpre, w_up_all, ffn_conv_w[l], cb,
                      w_dn_all, gpost, l, tm=L * ns, rs=ns)
        fs_s.append(st.reshape(2, ns, F2).transpose(1, 0, 2))

    y_sample = xs.reshape(L, ns, D_MODEL).transpose(1, 0, 2)
    return (xp, y_sample, jnp.stack(ks_p), jnp.stack(vs_p), jnp.stack(gs_p), jnp.stack(cs_p),
            jnp.stack(fs_p), jnp.stack(ks_s), jnp.stack(vs_s), jnp.stack(gs_s), jnp.stack(cs_s),
            jnp.stack(fs_s))
```

```python
import functools

import numpy as np
import jax
import jax.numpy as jnp
from jax import lax
from jax.experimental import pallas as pl
from jax.experimental.pallas import tpu as pltpu

F32 = jnp.float32
BF16 = jnp.bfloat16

D_MODEL = 1024
WINDOW = 128
HD_A = 64
HQ_A = 8
HKV_A = 2
G_A = HQ_A // HKV_A
H_B = 4
DK_B = 64
DV_B = 128
GATE_RANK = 16
GATE_TAU = 16.0
GLA_CHUNK = 64
D_FF = ((8 * D_MODEL // 3 + 127) // 128) * 128
F2 = 2 * D_FF
EPS = 1e-6
GELU_C = float(np.sqrt(2.0 / np.pi))
GELU_A = 0.044715

QA_W = HQ_A * HD_A
KV_W = 2 * HKV_A * HD_A
QB_W = H_B * DK_B
VB_W = H_B * DV_B
MAIN_W = QA_W + KV_W + 2 * QB_W + 2 * VB_W

LANES = 128
SUBLANES = 8
VMEM_LIMIT = 56 * 1024 * 1024

TM_TOK = 512
TM_INPROJ = 1024
TM_ODD = 1024
TM_GLA = 512
TQ_SWA = 512
GLA_GROUP = 128
ODD_TC = 256
FFN_TF = 256
FFN_ELEM_DTYPE = BF16
SEQ_BLOCK = 16
DEC_UNROLL = 8
DEC_KEYS = 2 * WINDOW


def _cparams(*sem):
    return pltpu.CompilerParams(dimension_semantics=sem, vmem_limit_bytes=VMEM_LIMIT)


def _rms(x, g):
    return x * lax.rsqrt(jnp.mean(x * x, axis=-1, keepdims=True) + EPS) * g


def _dot(a, b):
    return jnp.dot(a, b, preferred_element_type=F32)


def _dot_nt(a, b):
    return lax.dot_general(a, b, (((1,), (1,)), ((), ())), preferred_element_type=F32)


def _dot_tn(a, b):
    return lax.dot_general(a, b, (((0,), (0,)), ((), ())), preferred_element_type=F32)


def _split3(x):
    hi = x.astype(BF16)
    r1 = x - hi.astype(F32)
    mid = r1.astype(BF16)
    lo = (r1 - mid.astype(F32)).astype(BF16)
    return hi, mid, lo


def _inproj_even_kernel(x_ref, gpre_ref, w_ref, wg_ref, wgu_ref, bg_ref,
                        qa_ref, kv_ref, qb_ref, kb_ref, vb_ref, rb_ref, gb_ref):
    h = _rms(x_ref[...], gpre_ref[...]).astype(BF16)

    def mm(lo, width):
        return _dot(h, w_ref[:, lo:lo + width])

    lo = 0
    qa_ref[...] = (mm(lo, QA_W) * (HD_A ** -0.5)).astype(BF16)
    lo += QA_W
    kv_ref[...] = mm(lo, KV_W)
    lo += KV_W
    qb_ref[...] = mm(lo, QB_W) * (DK_B ** -0.5)
    lo += QB_W
    kb_ref[...] = mm(lo, QB_W)
    lo += QB_W
    vb_ref[...] = mm(lo, VB_W).astype(BF16)
    lo += VB_W
    rb_ref[...] = mm(lo, VB_W)
    glr = _dot(h, wg_ref[...]).astype(BF16)
    z = _dot(glr, wgu_ref[...]) + bg_ref[...]
    gb_ref[...] = jax.nn.log_sigmoid(z) * (1.0 / GATE_TAU)


def _inproj_even(x2d, gpre, w_main, w_g, w_gu, b_g):
    T = x2d.shape[0]
    tm = min(TM_INPROJ, T)
    row = lambda w: pl.BlockSpec((tm, w), lambda i: (i, 0))
    full = lambda a: pl.BlockSpec(a.shape, lambda i: (0,) * a.ndim)
    out_shape = (
        jax.ShapeDtypeStruct((T, QA_W), BF16), jax.ShapeDtypeStruct((T, KV_W), F32),
        jax.ShapeDtypeStruct((T, QB_W), F32), jax.ShapeDtypeStruct((T, QB_W), F32),
        jax.ShapeDtypeStruct((T, VB_W), BF16), jax.ShapeDtypeStruct((T, VB_W), F32),
        jax.ShapeDtypeStruct((T, QB_W), F32))
    return pl.pallas_call(
        _inproj_even_kernel, grid=(T // tm,),
        in_specs=[row(D_MODEL), full(gpre), full(w_main), full(w_g), full(w_gu), full(b_g)],
        out_specs=(row(QA_W), row(KV_W), row(QB_W), row(QB_W), row(VB_W), row(VB_W), row(QB_W)),
        out_shape=out_shape, compiler_params=_cparams("parallel"), name="inproj_even",
    )(x2d, gpre, w_main, w_g, w_gu, b_g)


def _conv3_from_buf(buf, cw, cur, tm, past, rs, dtype=F32):
    cw = cw.astype(dtype)
    y = cw[0:1, :] * buf[past - 2 * rs:past - 2 * rs + tm, :].astype(dtype)
    y = y + cw[1:2, :] * buf[past - rs:past - rs + tm, :].astype(dtype)
    return y + cw[2:3, :] * cur.astype(dtype)


def _odd_in_kernel(x_ref, past_ref, gpre_ref, win_ref, cw_ref, y_ref, st_ref, carry, buf,
                   *, tm, past, rs, tc):
    @pl.when(pl.program_id(1) == 0)
    def _():
        carry[...] = past_ref[0]

    h = _rms(x_ref[0], gpre_ref[...]).astype(BF16)
    for c in range(D_MODEL // tc):
        lo = c * tc
        bg = _dot(h, win_ref[:, lo:lo + tc])
        cu = _dot(h, win_ref[:, D_MODEL + lo:D_MODEL + lo + tc]) * \
            _dot(h, win_ref[:, 2 * D_MODEL + lo:2 * D_MODEL + lo + tc])
        buf[0:past, :] = carry[:, lo:lo + tc]
        buf[past:past + tm, :] = cu
        carry[:, lo:lo + tc] = buf[tm:tm + past, :]
        z = _conv3_from_buf(buf, cw_ref[:, lo:lo + tc], cu, tm, past, rs)
        y_ref[0, :, lo:lo + tc] = (bg * z).astype(BF16)
    st_ref[0] = carry[...]


def _odd_in(x3d, past0, gpre, w_in, conv_w, *, tm, rs):
    nb, rows, _ = x3d.shape
    past = past0.shape[1]
    tc = ODD_TC
    kern = functools.partial(_odd_in_kernel, tm=tm, past=past, rs=rs, tc=tc)
    full = lambda a: pl.BlockSpec(a.shape, lambda b, j: (0,) * a.ndim)
    return pl.pallas_call(
        kern, grid=(nb, rows // tm),
        in_specs=[pl.BlockSpec((1, tm, D_MODEL), lambda b, j: (b, j, 0)),
                  pl.BlockSpec((1, past, D_MODEL), lambda b, j: (b, 0, 0)),
                  full(gpre), full(w_in), full(conv_w)],
        out_specs=(pl.BlockSpec((1, tm, D_MODEL), lambda b, j: (b, j, 0)),
                   pl.BlockSpec((1, past, D_MODEL), lambda b, j: (b, 0, 0))),
        out_shape=(jax.ShapeDtypeStruct((nb, rows, D_MODEL), BF16),
                   jax.ShapeDtypeStruct((nb, past, D_MODEL), F32)),
        scratch_shapes=[pltpu.VMEM((past, D_MODEL), F32), pltpu.VMEM((past + tm, tc), F32)],
        compiler_params=_cparams("parallel", "arbitrary"), name="odd_in",
    )(x3d, past0, gpre, w_in, conv_w)


def _ffn_kernel(x_ref, ya_ref, yb_ref, wo_ref, gmix_ref, past_ref, gpre_ref, wup_ref, cw_ref,
                cb_ref, wdn_ref, gpost_ref, gk_ref, o_ref, st_ref,
                carry, buf_g, buf_v, acc, h_scr, act, x1_scr, *, tm, past, rs, tf):
    @pl.when(pl.program_id(1) == 0)
    def _():
        carry[...] = past_ref[0]

    mix = _dot(ya_ref[0], wo_ref[0]) + _dot(yb_ref[0], wo_ref[1])
    x1_scr[...] = x_ref[0] + _rms(mix, gmix_ref[...])
    h_scr[...] = _rms(x1_scr[...], gpre_ref[...]).astype(BF16)
    nf = D_FF // tf

    def up(c):
        for lo, buf in ((c * tf, buf_g), (D_FF + c * tf, buf_v)):
            buf[c % 2, 0:past, :] = carry[:, lo:lo + tf]
            buf[c % 2, past:past + tm, :] = _dot(h_scr[...], wup_ref[:, lo:lo + tf])
            carry[:, lo:lo + tf] = buf[c % 2, tm:tm + past, :]

    gelu_c = gk_ref[0:1, 0:1].astype(FFN_ELEM_DTYPE)
    gelu_ca = gk_ref[0:1, 1:2].astype(FFN_ELEM_DTYPE)

    def elem(c):
        halves = []
        for lo, buf, scale in ((c * tf, buf_g, 1.0), (D_FF + c * tf, buf_v, 0.5)):
            b = buf.at[c % 2]
            halves.append(_conv3_from_buf(b, cw_ref[:, lo:lo + tf] * scale, b[past:past + tm, :],
                                          tm, past, rs, FFN_ELEM_DTYPE)
                          + (cb_ref[:, lo:lo + tf] * scale).astype(FFN_ELEM_DTYPE))
        x, half_v = halves
        t = jnp.tanh(x * (x * x * gelu_ca + gelu_c))
        act[c % 2] = ((x * t + x) * half_v).astype(BF16)

    def down(c):
        part = _dot(act[c % 2], wdn_ref[c * tf:(c + 1) * tf, :])
        if c == 0:
            acc[...] = part
        else:
            acc[...] += part

    for c in range(nf + 2):
        if c < nf:
            up(c)
        if 1 <= c <= nf:
            elem(c - 1)
        if c >= 2:
            down(c - 2)
    o_ref[0] = x1_scr[...] + _rms(acc[...], gpost_ref[...])
    st_ref[0] = carry[...]


def _ffn(x3d, ya, yb, w_o, gmix, past0, gpre, w_up, conv_w, conv_b, w_dn, gpost, layer, *, tm, rs):
    nb, rows, _ = x3d.shape
    past = past0.shape[1]
    half = w_o.shape[0] // 2
    kern = functools.partial(_ffn_kernel, tm=tm, past=past, rs=rs, tf=FFN_TF)
    gelu_k = jnp.zeros((1, LANES), F32).at[0, 0].set(GELU_C).at[0, 1].set(GELU_C * GELU_A)
    w_o2 = w_o.reshape(2, half, D_MODEL)
    once = dict(pipeline_mode=pl.Buffered(1))
    full = lambda a: pl.BlockSpec(a.shape, lambda b, j: (0,) * a.ndim, **once)
    per_layer = lambda a: pl.BlockSpec((None,) + a.shape[1:], lambda b, j: (layer, 0, 0), **once)
    tok = lambda w, cblk=0: pl.BlockSpec((1, tm, w), lambda b, j: (b, j, cblk))
    return pl.pallas_call(
        kern, grid=(nb, rows // tm),
        in_specs=[tok(D_MODEL), tok(half, ya[1]), tok(half, yb[1]), full(w_o2), full(gmix),
                  pl.BlockSpec((1, past, F2), lambda b, j: (b, 0, 0)),
                  full(gpre), per_layer(w_up), full(conv_w), full(conv_b), per_layer(w_dn),
                  full(gpost), full(gelu_k)],
        out_specs=(tok(D_MODEL), pl.BlockSpec((1, past, F2), lambda b, j: (b, 0, 0))),
        out_shape=(jax.ShapeDtypeStruct((nb, rows, D_MODEL), F32),
                   jax.ShapeDtypeStruct((nb, past, F2), F32)),
        scratch_shapes=[pltpu.VMEM((past, F2), F32), pltpu.VMEM((2, past + tm, FFN_TF), F32),
                        pltpu.VMEM((2, past + tm, FFN_TF), F32), pltpu.VMEM((tm, D_MODEL), F32),
                        pltpu.VMEM((tm, D_MODEL), BF16), pltpu.VMEM((2, tm, FFN_TF), BF16),
                        pltpu.VMEM((tm, D_MODEL), F32)],
        compiler_params=_cparams("parallel", "arbitrary"), name="conv_ffn",
    )(x3d, ya[0], yb[0], w_o2, gmix, past0, gpre, w_up, conv_w, conv_b, w_dn, gpost, gelu_k)


def _swa_bias_prompt():
    qi = np.arange(WINDOW)[:, None]
    sj = np.arange(2 * WINDOW)[None, :]
    dist = WINDOW + qi - sj
    valid = (dist >= 0) & (dist <= WINDOW)
    slopes = 2.0 ** (-8.0 * np.arange(1, HQ_A + 1) / HQ_A)
    bias = np.where(valid[None], -slopes[:, None, None] * dist[None].astype(np.float64), -np.inf)
    return jnp.asarray(bias, dtype=F32)


def _swa_prompt_kernel(sink_ref, q_ref, kvc_ref, kvp_ref, bias_ref, o_ref, *, tq):
    i = pl.program_id(1)
    kvc = kvc_ref[0]
    kvp = kvp_ref[0]
    kcat = jnp.concatenate([kvp[:, 0:LANES], kvc[:, 0:LANES]], axis=0).astype(BF16)
    vcat = jnp.concatenate([kvp[:, LANES:], kvc[:, LANES:]], axis=0).astype(BF16)
    lane = lax.broadcasted_iota(jnp.int32, (WINDOW, LANES), 1)
    col = lax.broadcasted_iota(jnp.int32, (WINDOW, 2 * WINDOW), 1)
    for j in range(tq // WINDOW):
        keys = kcat[j * WINDOW:(j + 2) * WINDOW, :]
        vals = vcat[j * WINDOW:(j + 2) * WINDOW, :]
        for g in range(G_A):
            q2 = q_ref[0, j * WINDOW:(j + 1) * WINDOW, g * LANES:(g + 1) * LANES]
            outs = []
            for kv in range(HKV_A):
                hq = kv * G_A + g
                in_head = (lane < HD_A) if kv == 0 else (lane >= HD_A)
                qm = jnp.where(in_head, q2, jnp.zeros_like(q2))
                s = _dot_nt(qm, keys) + bias_ref[hq]
                if j == 0:
                    s = jnp.where(jnp.logical_and(i == 0, col < WINDOW), -jnp.inf, s)
                sink = sink_ref[hq]
                m = jnp.maximum(jnp.max(s, axis=-1, keepdims=True), sink)
                p = jnp.exp(s - m)
                den = jnp.sum(p, axis=-1, keepdims=True) + jnp.exp(sink - m)
                outs.append(_dot(p.astype(BF16), vals) / den)
            o2 = jnp.where(lane < HD_A, outs[0], outs[1])
            o_ref[0, j * WINDOW:(j + 1) * WINDOW, g * LANES:(g + 1) * LANES] = o2.astype(BF16)


def _swa_prompt(q3d, kv3d, sinks):
    nb, S, _ = q3d.shape
    tq = min(TQ_SWA, S)
    bias = _swa_bias_prompt()
    kern = functools.partial(_swa_prompt_kernel, tq=tq)
    blocks_per_tile = tq // WINDOW
    return pl.pallas_call(
        kern, grid=(nb, S // tq),
        in_specs=[pl.BlockSpec(memory_space=pltpu.SMEM),
                  pl.BlockSpec((1, tq, QA_W), lambda b, i: (b, i, 0)),
                  pl.BlockSpec((1, tq, KV_W), lambda b, i: (b, i, 0)),
                  pl.BlockSpec((1, WINDOW, KV_W),
                               lambda b, i: (b, jnp.maximum(i * blocks_per_tile - 1, 0), 0)),
                  pl.BlockSpec(bias.shape, lambda b, i: (0, 0, 0))],
        out_specs=pl.BlockSpec((1, tq, QA_W), lambda b, i: (b, i, 0)),
        out_shape=jax.ShapeDtypeStruct((nb, S, QA_W), BF16),
        compiler_params=_cparams("parallel", "arbitrary"), name="swa_prompt",
    )(sinks, q3d, kv3d, kv3d, bias)


GLA_LEVELS = (0, 1, 2, 4, 8, 16, 32)
LOG2E = 1.4426950408889634


def _gla_constants():
    r = np.arange(GLA_GROUP)
    i, j = r[:, None], r[None, :]
    ranges = [(i >= j) & (i // GLA_CHUNK == j // GLA_CHUNK)]
    masks = [i == j]
    for s in GLA_LEVELS[1:]:
        mid = (i // (2 * s)) * (2 * s) + s
        upper = (i % (2 * s) >= s) & (j >= mid) & (j <= i)
        lower = (i % (2 * s) < s) & (j > i) & (j < mid)
        ranges.append(upper | lower)
        masks.append((i // (2 * s) == j // (2 * s)) & (i % (2 * s) >= s) & (j % (2 * s) < s))
    ones_bd = (np.arange(QB_W)[:, None] // DK_B) == (np.arange(VB_W)[None, :] // DV_B)
    return (jnp.asarray(np.concatenate(ranges, axis=0), dtype=BF16),
            jnp.asarray(ones_bd, dtype=BF16), jnp.asarray(np.stack(masks), dtype=F32))


def _head_norm_gate(o, r, gn):
    parts = []
    for h in range(H_B):
        oh = o[:, h * DV_B:(h + 1) * DV_B]
        parts.append(oh * lax.rsqrt(jnp.mean(oh * oh, axis=-1, keepdims=True) + EPS))
    return jnp.concatenate(parts, axis=-1) * gn * jax.nn.silu(r)


def _gla_prompt_kernel(q_ref, k_ref, v_ref, g_ref, r_ref, gn_ref, rng_ref, lvl_ref,
                       o_ref, st_ref, state, b_scr, o_scr, qk_scr, *, tm):
    @pl.when(pl.program_id(1) == 0)
    def _():
        state[...] = jnp.zeros_like(state)

    q = q_ref[0]
    k = k_ref[0]
    v = v_ref[0]
    lane = lax.broadcasted_iota(jnp.int32, (1, LANES), 1)
    head_lanes = (lane < DK_B, lane >= DK_B)
    head_cols = lax.broadcasted_iota(jnp.int32, (1, QB_W), 1) // DK_B

    g2 = g_ref[0] * LOG2E
    ghi = g2.astype(BF16)
    glo = (g2 - ghi.astype(F32)).astype(BF16)
    nlev = len(GLA_LEVELS)
    for gi in range(tm // GLA_GROUP):
        rows = slice(gi * GLA_GROUP, (gi + 1) * GLA_GROUP)
        d_all = _dot(rng_ref[...], ghi[rows]) + _dot(rng_ref[...], glo[rows])
        b_scr[rows, :] = d_all[0:GLA_GROUP]
        qk_scr[0, 0, rows, :] = q[rows].astype(BF16)
        qk_scr[0, 1, rows, :] = k[rows].astype(BF16)
        for li in range(1, nlev):
            e = jnp.exp2(d_all[li * GLA_GROUP:(li + 1) * GLA_GROUP])
            qk_scr[li, 0, rows, :] = (q[rows] * e).astype(BF16)
            qk_scr[li, 1, rows, :] = (k[rows] * e).astype(BF16)
    b = b_scr[...]

    for gi in range(tm // GLA_GROUP):
        rows = slice(gi * GLA_GROUP, (gi + 1) * GLA_GROUP)
        for pi in range(H_B // 2):
            pr = slice(pi * LANES, (pi + 1) * LANES)
            a = None
            for li in range(nlev):
                ke = qk_scr[li, 1, rows, pr]
                kem = jnp.concatenate([jnp.where(sel, ke, jnp.zeros_like(ke)) for sel in head_lanes],
                                      axis=0)
                term = _dot_nt(qk_scr[li, 0, rows, pr], kem)
                keep = lvl_ref[li] > 0.5
                keep = jnp.concatenate([keep, keep], axis=1)
                a = jnp.where(keep, term, 0.0 if a is None else a)
            a = a.astype(BF16)
            for par in range(2):
                cols = slice((2 * pi + par) * DV_B, (2 * pi + par + 1) * DV_B)
                o_scr[rows, cols] = _dot(a[:, par * GLA_GROUP:(par + 1) * GLA_GROUP], v[rows, cols])

    for c in range(tm // GLA_CHUNK):
        rows = slice(c * GLA_CHUNK, (c + 1) * GLA_CHUNK)
        bc = b[rows]
        b_end = bc[GLA_CHUNK - 1:GLA_CHUNK, :]
        qe = (q[rows] * jnp.exp2(bc)).astype(BF16)
        ke = (k[rows] * jnp.exp2(b_end - bc)).astype(BF16)
        st = state[...]
        qm = jnp.concatenate([jnp.where(head_cols == h, qe, jnp.zeros_like(qe))
                              for h in range(H_B)], axis=0)
        km = jnp.concatenate([jnp.where(head_cols == h, ke, jnp.zeros_like(ke))
                              for h in range(H_B)], axis=0)
        vm = jnp.concatenate([v[rows, h * DV_B:(h + 1) * DV_B] for h in range(H_B)], axis=0)
        oi = _dot_nt(qm, st.astype(BF16))
        for h in range(H_B):
            o_scr[rows, h * DV_B:(h + 1) * DV_B] += oi[h * GLA_CHUNK:(h + 1) * GLA_CHUNK]
        state[...] = st * jnp.exp2(b_end) + _dot_tn(vm, km)

    o_ref[0] = _head_norm_gate(o_scr[...], r_ref[0], gn_ref[...]).astype(BF16)
    st_ref[0] = state[...]


def _gla_prompt(q3d, k3d, v3d, g3d, r3d, gn):
    nb, S, _ = q3d.shape
    tm = min(TM_GLA, S)
    rng, _, lvl = _gla_constants()
    kern = functools.partial(_gla_prompt_kernel, tm=tm)
    tok = lambda w: pl.BlockSpec((1, tm, w), lambda b, j: (b, j, 0))
    full = lambda a: pl.BlockSpec(a.shape, lambda b, j: (0,) * a.ndim)
    return pl.pallas_call(
        kern, grid=(nb, S // tm),
        in_specs=[tok(QB_W), tok(QB_W), tok(VB_W), tok(QB_W), tok(VB_W),
                  full(gn), full(rng), full(lvl)],
        out_specs=(tok(VB_W), pl.BlockSpec((1, DV_B, QB_W), lambda b, j: (b, 0, 0))),
        out_shape=(jax.ShapeDtypeStruct((nb, S, VB_W), BF16),
                   jax.ShapeDtypeStruct((nb, DV_B, QB_W), F32)),
        scratch_shapes=[pltpu.VMEM((DV_B, QB_W), F32), pltpu.VMEM((tm, QB_W), F32),
                        pltpu.VMEM((tm, VB_W), F32),
                        pltpu.VMEM((len(GLA_LEVELS), 2, tm, QB_W), BF16)],
        compiler_params=_cparams("parallel", "arbitrary"), name="gla_prompt",
    )(q3d, k3d, v3d, g3d, r3d, gn, rng, lvl)


def _swa_bias_decode(L):
    rows = HQ_A * L
    kpad = DEC_KEYS
    hq = np.arange(rows)[:, None] // L
    t = np.arange(rows)[:, None] % L
    s = np.arange(kpad)[None, :]
    dist = WINDOW + t - s
    valid = (dist >= 0) & (dist <= WINDOW) & (s < WINDOW + L)
    slopes = 2.0 ** (-8.0 * (hq + 1) / HQ_A)
    return jnp.asarray(np.where(valid, -slopes * dist, -np.inf), dtype=F32)


def _swa_decode_kernel(q_ref, kvn_ref, ck_ref, cv_ref, bias_ref, sink_ref,
                       o_ref, nk_ref, nv_ref, qs, os_, kc, vc, kn, vn, *, ns, L, sb):
    i = pl.program_id(0)
    rows_q = HQ_A * L
    lane = lax.broadcasted_iota(jnp.int32, (1, LANES), 1)

    @pl.when(i == 0)
    def _():
        q = q_ref[...].astype(F32)
        for kv in range(HKV_A):
            sel = (lane < HD_A) if kv == 0 else (lane >= HD_A)
            for g in range(G_A):
                base = (kv * G_A + g) * L * ns
                qs[base:base + L * ns, :] = jnp.where(sel, q[:, g * LANES:(g + 1) * LANES], 0.0)
        kc[...] = jnp.zeros_like(kc)
        vc[...] = jnp.zeros_like(vc)
        kn[...] = kvn_ref[:, 0:LANES]
        vn[...] = kvn_ref[:, LANES:]

    bias = bias_ref[...]
    sink = sink_ref[...]

    def body(t, carry):
        lhs = []
        for u in range(DEC_UNROLL):
            s = t * DEC_UNROLL + u
            seq = i * sb + s
            kc[u, 0:WINDOW, :] = ck_ref[s]
            vc[u, 0:WINDOW, :] = cv_ref[s]
            kc[u, WINDOW:WINDOW + L, :] = kn[pl.ds(seq, L, stride=ns), :]
            vc[u, WINDOW:WINDOW + L, :] = vn[pl.ds(seq, L, stride=ns), :]
            nk_ref[s] = kc[u, L:L + WINDOW, :]
            nv_ref[s] = vc[u, L:L + WINDOW, :]
            lhs.append(qs[pl.ds(seq, rows_q, stride=ns), :].astype(BF16))
        sc = jnp.einsum("uqd,ukd->uqk", jnp.stack(lhs), kc[...].astype(BF16),
                        preferred_element_type=F32) + bias
        m = jnp.maximum(jnp.max(sc, axis=-1, keepdims=True), sink)
        p = jnp.exp(sc - m)
        den = jnp.sum(p, axis=-1, keepdims=True) + jnp.exp(sink - m)
        res = jnp.einsum("uqk,ukd->uqd", p.astype(BF16), vc[...].astype(BF16),
                         preferred_element_type=F32) / den
        half = rows_q // 2
        for u in range(DEC_UNROLL):
            seq = i * sb + t * DEC_UNROLL + u
            os_[pl.ds(seq, half, stride=ns), :] = jnp.where(lane < HD_A, res[u, 0:half],
                                                            res[u, half:])
        return carry

    lax.fori_loop(0, sb // DEC_UNROLL, body, 0)

    @pl.when(i == pl.num_programs(0) - 1)
    def _():
        for g in range(G_A):
            o_ref[:, g * LANES:(g + 1) * LANES] = os_[g * L * ns:(g + 1) * L * ns, :].astype(BF16)


def _swa_decode(q_tm, kvn_tm, cache_k, cache_v, sinks, *, ns, L):
    sb = SEQ_BLOCK
    bias = _swa_bias_decode(L)
    sink_col = jnp.broadcast_to(jnp.repeat(sinks.astype(F32), L)[:, None], (HQ_A * L, 1))
    kern = functools.partial(_swa_decode_kernel, ns=ns, L=L, sb=sb)
    full = lambda a: pl.BlockSpec(a.shape, lambda i: (0,) * a.ndim)
    cache = pl.BlockSpec((sb, WINDOW, LANES), lambda i: (i, 0, 0))
    kpad = DEC_KEYS
    return pl.pallas_call(
        kern, grid=(ns // sb,),
        in_specs=[full(q_tm), full(kvn_tm), cache, cache, full(bias), full(sink_col)],
        out_specs=(pl.BlockSpec((L * ns, QA_W), lambda i: (0, 0)), cache, cache),
        out_shape=(jax.ShapeDtypeStruct((L * ns, QA_W), BF16),
                   jax.ShapeDtypeStruct(cache_k.shape, F32),
                   jax.ShapeDtypeStruct(cache_v.shape, F32)),
        scratch_shapes=[pltpu.VMEM((HQ_A * L * ns, LANES), F32),
                        pltpu.VMEM((G_A * L * ns, LANES), F32),
                        pltpu.VMEM((DEC_UNROLL, kpad, LANES), F32),
                        pltpu.VMEM((DEC_UNROLL, kpad, LANES), F32),
                        pltpu.VMEM((L * ns, LANES), F32), pltpu.VMEM((L * ns, LANES), F32)],
        compiler_params=_cparams("arbitrary"), name="swa_decode",
    )(q_tm, kvn_tm, cache_k, cache_v, bias, sink_col)


def _gla_decode_kernel(q_ref, k_ref, v_ref, g_ref, r_ref, gn_ref, ones_ref, s0_ref,
                       o_ref, s1_ref, qe2, ke2, v2, dec3, oi2, od, *, ns, L, sb):
    i = pl.program_id(0)
    lane = lax.broadcasted_iota(jnp.int32, (1, LANES), 1)
    npair = H_B // 2

    @pl.when(i == 0)
    def _():
        slab = lambda a, t: a[t * ns:(t + 1) * ns, :]
        q, k, g = q_ref[...], k_ref[...], g_ref[...]
        vf = v_ref[...].astype(F32)
        b = [slab(g, 0)]
        for t in range(1, L):
            b.append(b[-1] + slab(g, t))
        for t in range(L):
            acc = jnp.zeros((ns, VB_W), F32)
            for jj in range(t + 1):
                p = (slab(q, t) * slab(k, jj) * jnp.exp(b[t] - b[jj])).astype(BF16)
                acc = acc + _dot(p, ones_ref[...]) * slab(vf, jj)
            od[t * ns:(t + 1) * ns, :] = acc
            qe = slab(q, t) * jnp.exp(b[t])
            ke = slab(k, t) * jnp.exp(b[L - 1] - b[t])
            for par in range(2):
                sel = (lane < DK_B) if par == 0 else (lane >= DK_B)
                base = (par * L + t) * ns
                for pi in range(npair):
                    pr = slice(pi * LANES, (pi + 1) * LANES)
                    qe2[pi, base:base + ns, :] = jnp.where(sel, qe[:, pr], 0.0)
                    ke2[pi, base:base + ns, :] = jnp.where(sel, ke[:, pr], 0.0)
                    h = 2 * pi + par
                    v2[pi, base:base + ns, :] = slab(vf, t)[:, h * DV_B:(h + 1) * DV_B]
        dec3[...] = jnp.zeros_like(dec3)
        hi, mid, lo = _split3(jnp.exp(b[L - 1]))
        for pi in range(npair):
            pr = slice(pi * LANES, (pi + 1) * LANES)
            dec3[pi, 0:ns, :] = hi[:, pr].astype(F32)
            dec3[pi, ns:2 * ns, :] = mid[:, pr].astype(F32)
            dec3[pi, 2 * ns:3 * ns, :] = lo[:, pr].astype(F32)

    ones8 = jnp.ones((2 * L, LANES), BF16)

    def one_seq(s):
        seq = i * sb + s
        for pi in range(npair):
            take = lambda ref: ref[pi, pl.ds(seq, 2 * L, stride=ns), :].astype(BF16)
            st_p = s0_ref[s, pi * LANES:(pi + 1) * LANES, :]
            oi2[pi, pl.ds(seq, 2 * L, stride=ns), :] = _dot(take(qe2), st_p.astype(BF16))
            dcol = _dot_tn(take(dec3), ones8)
            upd = _dot_tn(take(ke2), take(v2))
            s1_ref[s, pi * LANES:(pi + 1) * LANES, :] = dcol * st_p + upd

    def body(t, carry):
        for u in range(DEC_UNROLL):
            one_seq(t * DEC_UNROLL + u)
        return carry

    lax.fori_loop(0, sb // DEC_UNROLL, body, 0)

    @pl.when(i == pl.num_programs(0) - 1)
    def _():
        for t in range(L):
            parts = []
            for h in range(H_B):
                base = ((h % 2) * L + t) * ns
                parts.append(oi2[h // 2, base:base + ns, :])
            o = jnp.concatenate(parts, axis=-1) + od[t * ns:(t + 1) * ns, :]
            o_ref[t * ns:(t + 1) * ns, :] = _head_norm_gate(
                o, r_ref[t * ns:(t + 1) * ns, :], gn_ref[...]).astype(BF16)


def _gla_decode(q_tm, k_tm, v_tm, g_tm, r_tm, gn, s0, *, ns, L):
    sb = SEQ_BLOCK
    assert 2 * L == SUBLANES
    _, ones_bd, _ = _gla_constants()
    kern = functools.partial(_gla_decode_kernel, ns=ns, L=L, sb=sb)
    full = lambda a: pl.BlockSpec(a.shape, lambda i: (0,) * a.ndim)
    st = pl.BlockSpec((sb, QB_W, DV_B), lambda i: (i, 0, 0))
    rows2 = 2 * L * ns
    return pl.pallas_call(
        kern, grid=(ns // sb,),
        in_specs=[full(q_tm), full(k_tm), full(v_tm), full(g_tm), full(r_tm), full(gn),
                  full(ones_bd), st],
        out_specs=(pl.BlockSpec((L * ns, VB_W), lambda i: (0, 0)), st),
        out_shape=(jax.ShapeDtypeStruct((L * ns, VB_W), BF16),
                   jax.ShapeDtypeStruct(s0.shape, F32)),
        scratch_shapes=[pltpu.VMEM((H_B // 2, rows2, LANES), F32) for _ in range(5)]
        + [pltpu.VMEM((L * ns, VB_W), F32)],
        compiler_params=_cparams("arbitrary"), name="gla_decode",
    )(q_tm, k_tm, v_tm, g_tm, r_tm, gn, ones_bd, s0)


def _qa_perm():
    return np.asarray([(kv * G_A + g) * HD_A + d
                       for g in range(G_A) for kv in range(HKV_A) for d in range(HD_A)])


def _prep_even(w_in, w_gate_up, b_gate, w_out):
    perm = _qa_perm()
    w_main = jnp.concatenate([w_in[:, :QA_W][:, perm], w_in[:, QA_W:MAIN_W]], axis=1).astype(BF16)
    w_g = jnp.pad(w_in[:, MAIN_W:], ((0, 0), (0, LANES - GATE_RANK))).astype(BF16)
    w_gu = jnp.pad(w_gate_up, ((0, LANES - GATE_RANK), (0, 0))).astype(BF16)
    w_o = jnp.concatenate([w_out[:QA_W][perm], w_out[QA_W:]], axis=0).astype(BF16)
    return w_main, w_g, w_gu, b_gate.reshape(1, -1), w_o


def _row(v):
    return v.reshape(1, -1)


def kernel(x_prompt, x_sample, cache_swa_k, cache_swa_v, state_gla, state_conv, state_ffn,
           norm_mix_pre, norm_mix_post, norm_ffn_pre, norm_ffn_post, w_in_even, w_gate_up, b_gate,
           attn_sinks, gla_norm, w_out_even, w_in_odd, conv_w_odd, w_out_odd, ffn_up, ffn_conv_w,
           ffn_conv_b, ffn_down):
    nb, S, _ = x_prompt.shape
    ns, L, _ = x_sample.shape
    depth = norm_mix_pre.shape[0]

    xp = x_prompt
    xs = x_sample.transpose(1, 0, 2).reshape(1, L * ns, D_MODEL)
    past_p = max(SUBLANES, 2)
    past_s = 2 * ns
    tm_p = min(TM_TOK, S)

    w_up_all = ffn_up.astype(BF16)
    w_dn_all = ffn_down.astype(BF16)

    ks_p, vs_p, gs_p, cs_p, fs_p = [], [], [], [], []
    ks_s, vs_s, gs_s, cs_s, fs_s = [], [], [], [], []

    for l in range(depth):
        gpre, gpost = _row(norm_mix_pre[l]), _row(norm_mix_post[l])
        if l % 2 == 0:
            e = l // 2
            w_main, w_g, w_gu, b_g, w_o = _prep_even(w_in_even[e], w_gate_up[e], b_gate[e],
                                                     w_out_even[e])
            gn = _row(gla_norm[e])
            qa, kv, qb, kb, vb, rb, gb = _inproj_even(xp.reshape(nb * S, D_MODEL), gpre,
                                                      w_main, w_g, w_gu, b_g)
            r3 = lambda a: a.reshape(nb, S, a.shape[-1])
            oa = _swa_prompt(r3(qa), r3(kv), attn_sinks[e])
            ob, st_t = _gla_prompt(r3(qb), r3(kb), r3(vb), r3(gb), r3(rb), gn)
            mix_p = ((oa, 0), (ob, 0))
            kv_last = r3(kv)[:, S - WINDOW:, :]
            ks_p.append(kv_last[..., :LANES].reshape(nb, WINDOW, HKV_A, HD_A))
            vs_p.append(kv_last[..., LANES:].reshape(nb, WINDOW, HKV_A, HD_A))
            gs_p.append(st_t.transpose(0, 2, 1).reshape(nb, H_B, DK_B, DV_B))
            qa, kv, qb, kb, vb, rb, gb = _inproj_even(xs.reshape(L * ns, D_MODEL), gpre,
                                                      w_main, w_g, w_gu, b_g)
            oa, nk, nv = _swa_decode(qa, kv, cache_swa_k[e].reshape(ns, WINDOW, LANES),
                                     cache_swa_v[e].reshape(ns, WINDOW, LANES), attn_sinks[e],
                                     ns=ns, L=L)
            ob, s1 = _gla_decode(qb, kb, vb, gb, rb, gn, state_gla[e].reshape(ns, QB_W, DV_B),
                                 ns=ns, L=L)
            mix_s = ((oa.reshape(1, L * ns, QA_W), 0), (ob.reshape(1, L * ns, VB_W), 0))
            ks_s.append(nk.reshape(ns, WINDOW, HKV_A, HD_A))
            vs_s.append(nv.reshape(ns, WINDOW, HKV_A, HD_A))
            gs_s.append(s1.reshape(ns, H_B, DK_B, DV_B))
        else:
            o = l // 2
            w_in = w_in_odd[o].astype(BF16)
            w_o = w_out_odd[o].astype(BF16)
            y, st = _odd_in(xp, jnp.zeros((nb, past_p, D_MODEL), F32), gpre, w_in, conv_w_odd[o],
                            tm=min(TM_ODD, S), rs=1)
            mix_p = ((y, 0), (y, 1))
            cs_p.append(st[:, past_p - 2:, :])
            past = state_conv[o].transpose(1, 0, 2).reshape(1, past_s, D_MODEL)
            y, st = _odd_in(xs, past, gpre, w_in, conv_w_odd[o], tm=L * ns, rs=ns)
            mix_s = ((y, 0), (y, 1))
            cs_s.append(st.reshape(2, ns, D_MODEL).transpose(1, 0, 2))

        gmix = gpost
        gpre, gpost = _row(norm_ffn_pre[l]), _row(norm_ffn_post[l])
        cb = _row(ffn_conv_b[l])
        xp, st = _ffn(xp, mix_p[0], mix_p[1], w_o, gmix, jnp.zeros((nb, past_p, F2), F32), gpre,
                      w_up_all, ffn_conv_w[l], cb, w_dn_all, gpost, l, tm=tm_p, rs=1)
        fs_p.append(st[:, past_p - 2:, :])
        past = state_ffn[l].transpose(1, 0, 2).reshape(1, past_s, F2)
        xs, st = _ffn(xs, mix_s[0], mix_s[1], w_o, gmix, past, gpre, w_up_all, ffn_conv_w[l], cb,
                      w_dn_all, gpost, l, tm=L * ns, rs=ns)
        fs_s.append(st.reshape(2, ns, F2).transpose(1, 0, 2))

    y_sample = xs.reshape(L, ns, D_MODEL).transpose(1, 0, 2)
    return (xp, y_sample, jnp.stack(ks_p), jnp.stack(vs_p), jnp.stack(gs_p), jnp.stack(cs_p),
            jnp.stack(fs_p), jnp.stack(ks_s), jnp.stack(vs_s), jnp.stack(gs_s), jnp.stack(cs_s),
            jnp.stack(fs_s))
```

```python
import functools

import numpy as np
import jax
import jax.numpy as jnp
from jax import lax
from jax.experimental import pallas as pl
from jax.experimental.pallas import tpu as pltpu

F32 = jnp.float32
BF16 = jnp.bfloat16

D_MODEL = 1024
WINDOW = 128
HD_A = 64
HQ_A = 8
HKV_A = 2
G_A = HQ_A // HKV_A
H_B = 4
DK_B = 64
DV_B = 128
GATE_RANK = 16
GATE_TAU = 16.0
GLA_CHUNK = 64
D_FF = ((8 * D_MODEL // 3 + 127) // 128) * 128
F2 = 2 * D_FF
EPS = 1e-6
GELU_C = float(np.sqrt(2.0 / np.pi))
GELU_A = 0.044715

QA_W = HQ_A * HD_A
KV_W = 2 * HKV_A * HD_A
QB_W = H_B * DK_B
VB_W = H_B * DV_B
MAIN_W = QA_W + KV_W + 2 * QB_W + 2 * VB_W

LANES = 128
SUBLANES = 8
VMEM_LIMIT = 56 * 1024 * 1024

TM_TOK = 512
TM_INPROJ = 1024
TM_ODD = 1024
TM_GLA = 1024
TQ_SWA = 512
GLA_GROUP = 128
ODD_TC = 256
FFN_TF = 256
FFN_ELEM_DTYPE = BF16
SEQ_BLOCK = 16
DEC_UNROLL = 8


def _cparams(*sem):
    return pltpu.CompilerParams(dimension_semantics=sem, vmem_limit_bytes=VMEM_LIMIT)


def _rms(x, g):
    return x * lax.rsqrt(jnp.mean(x * x, axis=-1, keepdims=True) + EPS) * g


def _dot(a, b):
    return jnp.dot(a, b, preferred_element_type=F32)


def _dot_nt(a, b):
    return lax.dot_general(a, b, (((1,), (1,)), ((), ())), preferred_element_type=F32)


def _dot_tn(a, b):
    return lax.dot_general(a, b, (((0,), (0,)), ((), ())), preferred_element_type=F32)


def _split3(x):
    hi = x.astype(BF16)
    r1 = x - hi.astype(F32)
    mid = r1.astype(BF16)
    lo = (r1 - mid.astype(F32)).astype(BF16)
    return hi, mid, lo


def _inproj_even_kernel(x_ref, gpre_ref, w_ref, wg_ref, wgu_ref, bg_ref,
                        qa_ref, kv_ref, qb_ref, kb_ref, vb_ref, rb_ref, gb_ref):
    h = _rms(x_ref[...], gpre_ref[...]).astype(BF16)

    def mm(lo, width):
        return _dot(h, w_ref[:, lo:lo + width])

    lo = 0
    qa_ref[...] = (mm(lo, QA_W) * (HD_A ** -0.5)).astype(BF16)
    lo += QA_W
    kv_ref[...] = mm(lo, KV_W)
    lo += KV_W
    qb_ref[...] = mm(lo, QB_W) * (DK_B ** -0.5)
    lo += QB_W
    kb_ref[...] = mm(lo, QB_W)
    lo += QB_W
    vb_ref[...] = mm(lo, VB_W).astype(BF16)
    lo += VB_W
    rb_ref[...] = mm(lo, VB_W)
    glr = _dot(h, wg_ref[...]).astype(BF16)
    z = _dot(glr, wgu_ref[...]) + bg_ref[...]
    gb_ref[...] = jax.nn.log_sigmoid(z) * (1.0 / GATE_TAU)


def _inproj_even(x2d, gpre, w_main, w_g, w_gu, b_g):
    T = x2d.shape[0]
    tm = min(TM_INPROJ, T)
    row = lambda w: pl.BlockSpec((tm, w), lambda i: (i, 0))
    full = lambda a: pl.BlockSpec(a.shape, lambda i: (0,) * a.ndim)
    out_shape = (
        jax.ShapeDtypeStruct((T, QA_W), BF16), jax.ShapeDtypeStruct((T, KV_W), F32),
        jax.ShapeDtypeStruct((T, QB_W), F32), jax.ShapeDtypeStruct((T, QB_W), F32),
        jax.ShapeDtypeStruct((T, VB_W), BF16), jax.ShapeDtypeStruct((T, VB_W), F32),
        jax.ShapeDtypeStruct((T, QB_W), F32))
    return pl.pallas_call(
        _inproj_even_kernel, grid=(T // tm,),
        in_specs=[row(D_MODEL), full(gpre), full(w_main), full(w_g), full(w_gu), full(b_g)],
        out_specs=(row(QA_W), row(KV_W), row(QB_W), row(QB_W), row(VB_W), row(VB_W), row(QB_W)),
        out_shape=out_shape, compiler_params=_cparams("parallel"), name="inproj_even",
    )(x2d, gpre, w_main, w_g, w_gu, b_g)


def _conv3_from_buf(buf, cw, cur, tm, past, rs, dtype=F32):
    cw = cw.astype(dtype)
    y = cw[0:1, :] * buf[past - 2 * rs:past - 2 * rs + tm, :].astype(dtype)
    y = y + cw[1:2, :] * buf[past - rs:past - rs + tm, :].astype(dtype)
    return y + cw[2:3, :] * cur.astype(dtype)


def _odd_in_kernel(x_ref, past_ref, gpre_ref, win_ref, cw_ref, y_ref, st_ref, carry, buf,
                   *, tm, past, rs, tc):
    @pl.when(pl.program_id(1) == 0)
    def _():
        carry[...] = past_ref[0]

    h = _rms(x_ref[0], gpre_ref[...]).astype(BF16)
    for c in range(D_MODEL // tc):
        lo = c * tc
        bg = _dot(h, win_ref[:, lo:lo + tc])
        cu = _dot(h, win_ref[:, D_MODEL + lo:D_MODEL + lo + tc]) * \
            _dot(h, win_ref[:, 2 * D_MODEL + lo:2 * D_MODEL + lo + tc])
        buf[0:past, :] = carry[:, lo:lo + tc]
        buf[past:past + tm, :] = cu
        carry[:, lo:lo + tc] = buf[tm:tm + past, :]
        z = _conv3_from_buf(buf, cw_ref[:, lo:lo + tc], cu, tm, past, rs)
        y_ref[0, :, lo:lo + tc] = (bg * z).astype(BF16)
    st_ref[0] = carry[...]


def _odd_in(x3d, past0, gpre, w_in, conv_w, *, tm, rs):
    nb, rows, _ = x3d.shape
    past = past0.shape[1]
    tc = ODD_TC
    kern = functools.partial(_odd_in_kernel, tm=tm, past=past, rs=rs, tc=tc)
    full = lambda a: pl.BlockSpec(a.shape, lambda b, j: (0,) * a.ndim)
    return pl.pallas_call(
        kern, grid=(nb, rows // tm),
        in_specs=[pl.BlockSpec((1, tm, D_MODEL), lambda b, j: (b, j, 0)),
                  pl.BlockSpec((1, past, D_MODEL), lambda b, j: (b, 0, 0)),
                  full(gpre), full(w_in), full(conv_w)],
        out_specs=(pl.BlockSpec((1, tm, D_MODEL), lambda b, j: (b, j, 0)),
                   pl.BlockSpec((1, past, D_MODEL), lambda b, j: (b, 0, 0))),
        out_shape=(jax.ShapeDtypeStruct((nb, rows, D_MODEL), BF16),
                   jax.ShapeDtypeStruct((nb, past, D_MODEL), F32)),
        scratch_shapes=[pltpu.VMEM((past, D_MODEL), F32), pltpu.VMEM((past + tm, tc), F32)],
        compiler_params=_cparams("parallel", "arbitrary"), name="odd_in",
    )(x3d, past0, gpre, w_in, conv_w)


def _ffn_kernel(x_ref, ya_ref, yb_ref, wo_ref, gmix_ref, past_ref, gpre_ref, wup_ref, cw_ref,
                cb_ref, wdn_ref, gpost_ref, gk_ref, o_ref, st_ref,
                carry, buf_g, buf_v, acc, h_scr, act, x1_scr, *, tm, past, rs, tf):
    @pl.when(pl.program_id(1) == 0)
    def _():
        carry[...] = past_ref[0]

    mix = _dot(ya_ref[0], wo_ref[0]) + _dot(yb_ref[0], wo_ref[1])
    x1_scr[...] = x_ref[0] + _rms(mix, gmix_ref[...])
    h_scr[...] = _rms(x1_scr[...], gpre_ref[...]).astype(BF16)
    nf = D_FF // tf

    def up(c):
        for lo, buf in ((c * tf, buf_g), (D_FF + c * tf, buf_v)):
            buf[c % 2, 0:past, :] = carry[:, lo:lo + tf]
            buf[c % 2, past:past + tm, :] = _dot(h_scr[...], wup_ref[:, lo:lo + tf])
            carry[:, lo:lo + tf] = buf[c % 2, tm:tm + past, :]

    gelu_c = gk_ref[0:1, 0:1].astype(FFN_ELEM_DTYPE)
    gelu_ca = gk_ref[0:1, 1:2].astype(FFN_ELEM_DTYPE)

    def elem(c):
        halves = []
        for lo, buf, scale in ((c * tf, buf_g, 1.0), (D_FF + c * tf, buf_v, 0.5)):
            b = buf.at[c % 2]
            halves.append(_conv3_from_buf(b, cw_ref[:, lo:lo + tf] * scale, b[past:past + tm, :],
                                          tm, past, rs, FFN_ELEM_DTYPE)
                          + (cb_ref[:, lo:lo + tf] * scale).astype(FFN_ELEM_DTYPE))
        x, half_v = halves
        t = jnp.tanh(x * (x * x * gelu_ca + gelu_c))
        act[c % 2] = ((x * t + x) * half_v).astype(BF16)

    def down(c):
        part = _dot(act[c % 2], wdn_ref[c * tf:(c + 1) * tf, :])
        if c == 0:
            acc[...] = part
        else:
            acc[...] += part

    for c in range(nf + 2):
        if c < nf:
            up(c)
        if 1 <= c <= nf:
            elem(c - 1)
        if c >= 2:
            down(c - 2)
    o_ref[0] = x1_scr[...] + _rms(acc[...], gpost_ref[...])
    st_ref[0] = carry[...]


def _ffn(x3d, ya, yb, w_o, gmix, past0, gpre, w_up, conv_w, conv_b, w_dn, gpost, layer, *, tm, rs):
    nb, rows, _ = x3d.shape
    past = past0.shape[1]
    half = w_o.shape[0] // 2
    kern = functools.partial(_ffn_kernel, tm=tm, past=past, rs=rs, tf=FFN_TF)
    gelu_k = jnp.zeros((1, LANES), F32).at[0, 0].set(GELU_C).at[0, 1].set(GELU_C * GELU_A)
    w_o2 = w_o.reshape(2, half, D_MODEL)
    once = dict(pipeline_mode=pl.Buffered(1))
    full = lambda a: pl.BlockSpec(a.shape, lambda b, j: (0,) * a.ndim, **once)
    per_layer = lambda a: pl.BlockSpec((None,) + a.shape[1:], lambda b, j: (layer, 0, 0), **once)
    tok = lambda w, cblk=0: pl.BlockSpec((1, tm, w), lambda b, j: (b, j, cblk))
    return pl.pallas_call(
        kern, grid=(nb, rows // tm),
        in_specs=[tok(D_MODEL), tok(half, ya[1]), tok(half, yb[1]), full(w_o2), full(gmix),
                  pl.BlockSpec((1, past, F2), lambda b, j: (b, 0, 0)),
                  full(gpre), per_layer(w_up), full(conv_w), full(conv_b), per_layer(w_dn),
                  full(gpost), full(gelu_k)],
        out_specs=(tok(D_MODEL), pl.BlockSpec((1, past, F2), lambda b, j: (b, 0, 0))),
        out_shape=(jax.ShapeDtypeStruct((nb, rows, D_MODEL), F32),
                   jax.ShapeDtypeStruct((nb, past, F2), F32)),
        scratch_shapes=[pltpu.VMEM((past, F2), F32), pltpu.VMEM((2, past + tm, FFN_TF), F32),
                        pltpu.VMEM((2, past + tm, FFN_TF), F32), pltpu.VMEM((tm, D_MODEL), F32),
                        pltpu.VMEM((tm, D_MODEL), BF16), pltpu.VMEM((2, tm, FFN_TF), BF16),
                        pltpu.VMEM((tm, D_MODEL), F32)],
        compiler_params=_cparams("parallel", "arbitrary"), name="conv_ffn",
    )(x3d, ya[0], yb[0], w_o2, gmix, past0, gpre, w_up, conv_w, conv_b, w_dn, gpost, gelu_k)


def _swa_bias_prompt():
    qi = np.arange(WINDOW)[:, None]
    sj = np.arange(2 * WINDOW)[None, :]
    dist = WINDOW + qi - sj
    valid = (dist >= 0) & (dist <= WINDOW)
    slopes = 2.0 ** (-8.0 * np.arange(1, HQ_A + 1) / HQ_A)
    bias = np.where(valid[None], -slopes[:, None, None] * dist[None].astype(np.float64), -np.inf)
    return jnp.asarray(bias, dtype=F32)


def _swa_prompt_kernel(sink_ref, q_ref, kvc_ref, kvp_ref, bias_ref, o_ref, *, tq):
    i = pl.program_id(1)
    kvc = kvc_ref[0]
    kvp = kvp_ref[0]
    kcat = jnp.concatenate([kvp[:, 0:LANES], kvc[:, 0:LANES]], axis=0).astype(BF16)
    vcat = jnp.concatenate([kvp[:, LANES:], kvc[:, LANES:]], axis=0).astype(BF16)
    lane = lax.broadcasted_iota(jnp.int32, (WINDOW, LANES), 1)
    col = lax.broadcasted_iota(jnp.int32, (WINDOW, 2 * WINDOW), 1)
    for j in range(tq // WINDOW):
        keys = kcat[j * WINDOW:(j + 2) * WINDOW, :]
        vals = vcat[j * WINDOW:(j + 2) * WINDOW, :]
        for g in range(G_A):
            q2 = q_ref[0, j * WINDOW:(j + 1) * WINDOW, g * LANES:(g + 1) * LANES]
            outs = []
            for kv in range(HKV_A):
                hq = kv * G_A + g
                in_head = (lane < HD_A) if kv == 0 else (lane >= HD_A)
                qm = jnp.where(in_head, q2, jnp.zeros_like(q2))
                s = _dot_nt(qm, keys) + bias_ref[hq]
                if j == 0:
                    s = jnp.where(jnp.logical_and(i == 0, col < WINDOW), -jnp.inf, s)
                sink = sink_ref[hq]
                m = jnp.maximum(jnp.max(s, axis=-1, keepdims=True), sink)
                p = jnp.exp(s - m)
                den = jnp.sum(p, axis=-1, keepdims=True) + jnp.exp(sink - m)
                outs.append(_dot(p.astype(BF16), vals) / den)
            o2 = jnp.where(lane < HD_A, outs[0], outs[1])
            o_ref[0, j * WINDOW:(j + 1) * WINDOW, g * LANES:(g + 1) * LANES] = o2.astype(BF16)


def _swa_prompt(q3d, kv3d, sinks):
    nb, S, _ = q3d.shape
    tq = min(TQ_SWA, S)
    bias = _swa_bias_prompt()
    kern = functools.partial(_swa_prompt_kernel, tq=tq)
    blocks_per_tile = tq // WINDOW
    return pl.pallas_call(
        kern, grid=(nb, S // tq),
        in_specs=[pl.BlockSpec(memory_space=pltpu.SMEM),
                  pl.BlockSpec((1, tq, QA_W), lambda b, i: (b, i, 0)),
                  pl.BlockSpec((1, tq, KV_W), lambda b, i: (b, i, 0)),
                  pl.BlockSpec((1, WINDOW, KV_W),
                               lambda b, i: (b, jnp.maximum(i * blocks_per_tile - 1, 0), 0)),
                  pl.BlockSpec(bias.shape, lambda b, i: (0, 0, 0))],
        out_specs=pl.BlockSpec((1, tq, QA_W), lambda b, i: (b, i, 0)),
        out_shape=jax.ShapeDtypeStruct((nb, S, QA_W), BF16),
        compiler_params=_cparams("parallel", "arbitrary"), name="swa_prompt",
    )(sinks, q3d, kv3d, kv3d, bias)


GLA_LEVELS = (0, 1, 2, 4, 8, 16, 32)
LOG2E = 1.4426950408889634


def _gla_constants():
    r = np.arange(GLA_GROUP)
    i, j = r[:, None], r[None, :]
    ranges = [(i >= j) & (i // GLA_CHUNK == j // GLA_CHUNK)]
    masks = [i == j]
    for s in GLA_LEVELS[1:]:
        mid = (i // (2 * s)) * (2 * s) + s
        upper = (i % (2 * s) >= s) & (j >= mid) & (j <= i)
        lower = (i % (2 * s) < s) & (j > i) & (j < mid)
        ranges.append(upper | lower)
        masks.append((i // (2 * s) == j // (2 * s)) & (i % (2 * s) >= s) & (j % (2 * s) < s))
    ones_bd = (np.arange(QB_W)[:, None] // DK_B) == (np.arange(VB_W)[None, :] // DV_B)
    return (jnp.asarray(np.concatenate(ranges, axis=0), dtype=BF16),
            jnp.asarray(ones_bd, dtype=BF16), jnp.asarray(np.stack(masks), dtype=F32))


def _head_norm_gate(o, r, gn):
    parts = []
    for h in range(H_B):
        oh = o[:, h * DV_B:(h + 1) * DV_B]
        parts.append(oh * lax.rsqrt(jnp.mean(oh * oh, axis=-1, keepdims=True) + EPS))
    return jnp.concatenate(parts, axis=-1) * gn * jax.nn.silu(r)


def _gla_prompt_kernel(q_ref, k_ref, v_ref, g_ref, r_ref, gn_ref, rng_ref, lvl_ref,
                       o_ref, st_ref, state, b_scr, o_scr, qk_scr, *, tm):
    @pl.when(pl.program_id(1) == 0)
    def _():
        state[...] = jnp.zeros_like(state)

    q = q_ref[0]
    k = k_ref[0]
    v = v_ref[0]
    lane = lax.broadcasted_iota(jnp.int32, (1, LANES), 1)
    head_lanes = (lane < DK_B, lane >= DK_B)
    head_cols = lax.broadcasted_iota(jnp.int32, (1, QB_W), 1) // DK_B

    g2 = g_ref[0] * LOG2E
    ghi = g2.astype(BF16)
    glo = (g2 - ghi.astype(F32)).astype(BF16)
    nlev = len(GLA_LEVELS)
    for gi in range(tm // GLA_GROUP):
        rows = slice(gi * GLA_GROUP, (gi + 1) * GLA_GROUP)
        d_all = _dot(rng_ref[...], ghi[rows]) + _dot(rng_ref[...], glo[rows])
        b_scr[rows, :] = d_all[0:GLA_GROUP]
        qk_scr[0, 0, rows, :] = q[rows].astype(BF16)
        qk_scr[0, 1, rows, :] = k[rows].astype(BF16)
        for li in range(1, nlev):
            e = jnp.exp2(d_all[li * GLA_GROUP:(li + 1) * GLA_GROUP])
            qk_scr[li, 0, rows, :] = (q[rows] * e).astype(BF16)
            qk_scr[li, 1, rows, :] = (k[rows] * e).astype(BF16)
    b = b_scr[...]

    for gi in range(tm // GLA_GROUP):
        rows = slice(gi * GLA_GROUP, (gi + 1) * GLA_GROUP)
        for pi in range(H_B // 2):
            pr = slice(pi * LANES, (pi + 1) * LANES)
            a = None
            for li in range(nlev):
                ke = qk_scr[li, 1, rows, pr]
                kem = jnp.concatenate([jnp.where(sel, ke, jnp.zeros_like(ke)) for sel in head_lanes],
                                      axis=0)
                term = _dot_nt(qk_scr[li, 0, rows, pr], kem)
                keep = lvl_ref[li] > 0.5
                keep = jnp.concatenate([keep, keep], axis=1)
                a = jnp.where(keep, term, 0.0 if a is None else a)
            a = a.astype(BF16)
            for par in range(2):
                cols = slice((2 * pi + par) * DV_B, (2 * pi + par + 1) * DV_B)
                o_scr[rows, cols] = _dot(a[:, par * GLA_GROUP:(par + 1) * GLA_GROUP], v[rows, cols])

    for c in range(tm // GLA_CHUNK):
        rows = slice(c * GLA_CHUNK, (c + 1) * GLA_CHUNK)
        bc = b[rows]
        b_end = bc[GLA_CHUNK - 1:GLA_CHUNK, :]
        qe = (q[rows] * jnp.exp2(bc)).astype(BF16)
        ke = (k[rows] * jnp.exp2(b_end - bc)).astype(BF16)
        st = state[...]
        qm = jnp.concatenate([jnp.where(head_cols == h, qe, jnp.zeros_like(qe))
                              for h in range(H_B)], axis=0)
        km = jnp.concatenate([jnp.where(head_cols == h, ke, jnp.zeros_like(ke))
                              for h in range(H_B)], axis=0)
        vm = jnp.concatenate([v[rows, h * DV_B:(h + 1) * DV_B] for h in range(H_B)], axis=0)
        oi = _dot_nt(qm, st.astype(BF16))
        for h in range(H_B):
            o_scr[rows, h * DV_B:(h + 1) * DV_B] += oi[h * GLA_CHUNK:(h + 1) * GLA_CHUNK]
        state[...] = st * jnp.exp2(b_end) + _dot_tn(vm, km)

    o_ref[0] = _head_norm_gate(o_scr[...], r_ref[0], gn_ref[...]).astype(BF16)
    st_ref[0] = state[...]


def _gla_prompt(q3d, k3d, v3d, g3d, r3d, gn):
    nb, S, _ = q3d.shape
    tm = min(TM_GLA, S)
    rng, _, lvl = _gla_constants()
    kern = functools.partial(_gla_prompt_kernel, tm=tm)
    tok = lambda w: pl.BlockSpec((1, tm, w), lambda b, j: (b, j, 0))
    full = lambda a: pl.BlockSpec(a.shape, lambda b, j: (0,) * a.ndim)
    return pl.pallas_call(
        kern, grid=(nb, S // tm),
        in_specs=[tok(QB_W), tok(QB_W), tok(VB_W), tok(QB_W), tok(VB_W),
                  full(gn), full(rng), full(lvl)],
        out_specs=(tok(VB_W), pl.BlockSpec((1, DV_B, QB_W), lambda b, j: (b, 0, 0))),
        out_shape=(jax.ShapeDtypeStruct((nb, S, VB_W), BF16),
                   jax.ShapeDtypeStruct((nb, DV_B, QB_W), F32)),
        scratch_shapes=[pltpu.VMEM((DV_B, QB_W), F32), pltpu.VMEM((tm, QB_W), F32),
                        pltpu.VMEM((tm, VB_W), F32),
                        pltpu.VMEM((len(GLA_LEVELS), 2, tm, QB_W), BF16)],
        compiler_params=_cparams("parallel", "arbitrary"), name="gla_prompt",
    )(q3d, k3d, v3d, g3d, r3d, gn, rng, lvl)


def _swa_bias_decode(L):
    rows = HQ_A * L
    hq = np.arange(rows)[:, None] // L
    t = np.arange(rows)[:, None] % L
    s = np.arange(WINDOW + SUBLANES)[None, :]
    dist = WINDOW + t - s
    valid = (dist >= 0) & (dist <= WINDOW) & (s < WINDOW + L)
    slopes = 2.0 ** (-8.0 * (hq + 1) / HQ_A)
    bias = np.where(valid, -slopes * dist, -np.inf)
    return jnp.asarray(bias[:, :WINDOW], dtype=F32), jnp.asarray(bias[:, WINDOW:], dtype=F32)


def _swa_decode_kernel(q_ref, kvn_ref, kt_ref, vt_ref, bias_c_ref, bias_n_ref, sink_ref,
                       o_ref, qs, os_, kn, vn, *, ns, L, sb):
    i = pl.program_id(0)
    rows_q = HQ_A * L
    lane = lax.broadcasted_iota(jnp.int32, (1, LANES), 1)

    @pl.when(i == 0)
    def _():
        q = q_ref[...].astype(F32)
        for kv in range(HKV_A):
            sel = (lane < HD_A) if kv == 0 else (lane >= HD_A)
            for g in range(G_A):
                base = (kv * G_A + g) * L * ns
                qs[base:base + L * ns, :] = jnp.where(sel, q[:, g * LANES:(g + 1) * LANES], 0.0)
        kn[...] = jnp.zeros_like(kn)
        vn[...] = jnp.zeros_like(vn)
        kn[0:L * ns, :] = kvn_ref[:, 0:LANES]
        vn[0:L * ns, :] = kvn_ref[:, LANES:]

    bias_c = bias_c_ref[...]
    bias_n = bias_n_ref[...]
    sink = sink_ref[...]

    def body(t, carry):
        lhs, k_new, v_new = [], [], []
        for u in range(DEC_UNROLL):
            seq = i * sb + t * DEC_UNROLL + u
            lhs.append(qs[pl.ds(seq, rows_q, stride=ns), :].astype(BF16))
            k_new.append(kn[pl.ds(seq, SUBLANES, stride=ns), :].astype(BF16))
            v_new.append(vn[pl.ds(seq, SUBLANES, stride=ns), :].astype(BF16))
        lhs, k_new, v_new = jnp.stack(lhs), jnp.stack(k_new), jnp.stack(v_new)
        blk = pl.ds(pl.multiple_of(t * DEC_UNROLL, DEC_UNROLL), DEC_UNROLL)
        kt = kt_ref[blk].astype(BF16)
        vt = vt_ref[blk].astype(BF16)
        sc_c = jnp.einsum("uqd,udw->uqw", lhs, kt, preferred_element_type=F32) + bias_c
        sc_n = jnp.einsum("uqd,utd->uqt", lhs, k_new, preferred_element_type=F32) + bias_n
        m = jnp.maximum(jnp.maximum(jnp.max(sc_c, axis=-1, keepdims=True),
                                    jnp.max(sc_n, axis=-1, keepdims=True)), sink)
        p_c = jnp.exp(sc_c - m)
        p_n = jnp.exp(sc_n - m)
        den = (jnp.sum(p_c, axis=-1, keepdims=True) + jnp.sum(p_n, axis=-1, keepdims=True)
               + jnp.exp(sink - m))
        res = (jnp.einsum("uqw,udw->uqd", p_c.astype(BF16), vt, preferred_element_type=F32)
               + jnp.einsum("uqt,utd->uqd", p_n.astype(BF16), v_new,
                            preferred_element_type=F32)) / den
        half = rows_q // 2
        for u in range(DEC_UNROLL):
            seq = i * sb + t * DEC_UNROLL + u
            os_[pl.ds(seq, half, stride=ns), :] = jnp.where(lane < HD_A, res[u, 0:half],
                                                            res[u, half:])
        return carry

    lax.fori_loop(0, sb // DEC_UNROLL, body, 0)

    @pl.when(i == pl.num_programs(0) - 1)
    def _():
        for g in range(G_A):
            o_ref[:, g * LANES:(g + 1) * LANES] = os_[g * L * ns:(g + 1) * L * ns, :].astype(BF16)


def _swa_decode(q_tm, kvn_tm, cache_kt, cache_vt, sinks, *, ns, L):
    sb = SEQ_BLOCK
    bias_c, bias_n = _swa_bias_decode(L)
    sink_col = jnp.broadcast_to(jnp.repeat(sinks.astype(F32), L)[:, None], (HQ_A * L, 1))
    kern = functools.partial(_swa_decode_kernel, ns=ns, L=L, sb=sb)
    full = lambda a: pl.BlockSpec(a.shape, lambda i: (0,) * a.ndim)
    cache = pl.BlockSpec((sb, LANES, WINDOW), lambda i: (i, 0, 0))
    return pl.pallas_call(
        kern, grid=(ns // sb,),
        in_specs=[full(q_tm), full(kvn_tm), cache, cache, full(bias_c), full(bias_n),
                  full(sink_col)],
        out_specs=pl.BlockSpec((L * ns, QA_W), lambda i: (0, 0)),
        out_shape=jax.ShapeDtypeStruct((L * ns, QA_W), BF16),
        scratch_shapes=[pltpu.VMEM((HQ_A * L * ns, LANES), F32),
                        pltpu.VMEM((G_A * L * ns, LANES), F32),
                        pltpu.VMEM((SUBLANES * ns, LANES), F32),
                        pltpu.VMEM((SUBLANES * ns, LANES), F32)],
        compiler_params=_cparams("arbitrary"), name="swa_decode",
    )(q_tm, kvn_tm, cache_kt, cache_vt, bias_c, bias_n, sink_col)


def _gla_decode_kernel(q_ref, k_ref, v_ref, g_ref, r_ref, gn_ref, ones_ref, s0_ref,
                       o_ref, s1_ref, qe2, ke2, v2, dec3, oi2, od, *, ns, L, sb):
    i = pl.program_id(0)
    lane = lax.broadcasted_iota(jnp.int32, (1, LANES), 1)
    npair = H_B // 2

    @pl.when(i == 0)
    def _():
        slab = lambda a, t: a[t * ns:(t + 1) * ns, :]
        q, k, g = q_ref[...], k_ref[...], g_ref[...]
        vf = v_ref[...].astype(F32)
        b = [slab(g, 0)]
        for t in range(1, L):
            b.append(b[-1] + slab(g, t))
        for t in range(L):
            acc = jnp.zeros((ns, VB_W), F32)
            for jj in range(t + 1):
                p = (slab(q, t) * slab(k, jj) * jnp.exp(b[t] - b[jj])).astype(BF16)
                acc = acc + _dot(p, ones_ref[...]) * slab(vf, jj)
            od[t * ns:(t + 1) * ns, :] = acc
            qe = slab(q, t) * jnp.exp(b[t])
            ke = slab(k, t) * jnp.exp(b[L - 1] - b[t])
            for par in range(2):
                sel = (lane < DK_B) if par == 0 else (lane >= DK_B)
                base = (par * L + t) * ns
                for pi in range(npair):
                    pr = slice(pi * LANES, (pi + 1) * LANES)
                    qe2[pi, base:base + ns, :] = jnp.where(sel, qe[:, pr], 0.0)
                    ke2[pi, base:base + ns, :] = jnp.where(sel, ke[:, pr], 0.0)
                    h = 2 * pi + par
                    v2[pi, base:base + ns, :] = slab(vf, t)[:, h * DV_B:(h + 1) * DV_B]
        dec3[...] = jnp.zeros_like(dec3)
        hi, mid, lo = _split3(jnp.exp(b[L - 1]))
        for pi in range(npair):
            pr = slice(pi * LANES, (pi + 1) * LANES)
            dec3[pi, 0:ns, :] = hi[:, pr].astype(F32)
            dec3[pi, ns:2 * ns, :] = mid[:, pr].astype(F32)
            dec3[pi, 2 * ns:3 * ns, :] = lo[:, pr].astype(F32)

    ones8 = jnp.ones((2 * L, LANES), BF16)

    def one_seq(s):
        seq = i * sb + s
        for pi in range(npair):
            take = lambda ref: ref[pi, pl.ds(seq, 2 * L, stride=ns), :].astype(BF16)
            st_p = s0_ref[s, pi * LANES:(pi + 1) * LANES, :]
            oi2[pi, pl.ds(seq, 2 * L, stride=ns), :] = _dot(take(qe2), st_p.astype(BF16))
            dcol = _dot_tn(take(dec3), ones8)
            upd = _dot_tn(take(ke2), take(v2))
            s1_ref[s, pi * LANES:(pi + 1) * LANES, :] = dcol * st_p + upd

    def body(t, carry):
        for u in range(DEC_UNROLL):
            one_seq(t * DEC_UNROLL + u)
        return carry

    lax.fori_loop(0, sb // DEC_UNROLL, body, 0)

    @pl.when(i == pl.num_programs(0) - 1)
    def _():
        for t in range(L):
            parts = []
            for h in range(H_B):
                base = ((h % 2) * L + t) * ns
                parts.append(oi2[h // 2, base:base + ns, :])
            o = jnp.concatenate(parts, axis=-1) + od[t * ns:(t + 1) * ns, :]
            o_ref[t * ns:(t + 1) * ns, :] = _head_norm_gate(
                o, r_ref[t * ns:(t + 1) * ns, :], gn_ref[...]).astype(BF16)


def _gla_decode(q_tm, k_tm, v_tm, g_tm, r_tm, gn, s0, *, ns, L):
    sb = SEQ_BLOCK
    assert 2 * L == SUBLANES
    _, ones_bd, _ = _gla_constants()
    kern = functools.partial(_gla_decode_kernel, ns=ns, L=L, sb=sb)
    full = lambda a: pl.BlockSpec(a.shape, lambda i: (0,) * a.ndim)
    st = pl.BlockSpec((sb, QB_W, DV_B), lambda i: (i, 0, 0))
    rows2 = 2 * L * ns
    return pl.pallas_call(
        kern, grid=(ns // sb,),
        in_specs=[full(q_tm), full(k_tm), full(v_tm), full(g_tm), full(r_tm), full(gn),
                  full(ones_bd), st],
        out_specs=(pl.BlockSpec((L * ns, VB_W), lambda i: (0, 0)), st),
        out_shape=(jax.ShapeDtypeStruct((L * ns, VB_W), BF16),
                   jax.ShapeDtypeStruct(s0.shape, F32)),
        scratch_shapes=[pltpu.VMEM((H_B // 2, rows2, LANES), F32) for _ in range(5)]
        + [pltpu.VMEM((L * ns, VB_W), F32)],
        compiler_params=_cparams("arbitrary"), name="gla_decode",
    )(q_tm, k_tm, v_tm, g_tm, r_tm, gn, ones_bd, s0)


def _qa_perm():
    return np.asarray([(kv * G_A + g) * HD_A + d
                       for g in range(G_A) for kv in range(HKV_A) for d in range(HD_A)])


def _prep_even(w_in, w_gate_up, b_gate, w_out):
    perm = _qa_perm()
    w_main = jnp.concatenate([w_in[:, :QA_W][:, perm], w_in[:, QA_W:MAIN_W]], axis=1).astype(BF16)
    w_g = jnp.pad(w_in[:, MAIN_W:], ((0, 0), (0, LANES - GATE_RANK))).astype(BF16)
    w_gu = jnp.pad(w_gate_up, ((0, LANES - GATE_RANK), (0, 0))).astype(BF16)
    w_o = jnp.concatenate([w_out[:QA_W][perm], w_out[QA_W:]], axis=0).astype(BF16)
    return w_main, w_g, w_gu, b_gate.reshape(1, -1), w_o


def _row(v):
    return v.reshape(1, -1)


def kernel(x_prompt, x_sample, cache_swa_k, cache_swa_v, state_gla, state_conv, state_ffn,
           norm_mix_pre, norm_mix_post, norm_ffn_pre, norm_ffn_post, w_in_even, w_gate_up, b_gate,
           attn_sinks, gla_norm, w_out_even, w_in_odd, conv_w_odd, w_out_odd, ffn_up, ffn_conv_w,
           ffn_conv_b, ffn_down):
    nb, S, _ = x_prompt.shape
    ns, L, _ = x_sample.shape
    depth = norm_mix_pre.shape[0]

    xp = x_prompt
    xs = x_sample.transpose(1, 0, 2).reshape(1, L * ns, D_MODEL)
    past_p = max(SUBLANES, 2)
    past_s = 2 * ns
    tm_p = min(TM_TOK, S)

    w_up_all = ffn_up.astype(BF16)
    w_dn_all = ffn_down.astype(BF16)

    ks_p, vs_p, gs_p, cs_p, fs_p = [], [], [], [], []
    ks_s, vs_s, gs_s, cs_s, fs_s = [], [], [], [], []

    for l in range(depth):
        gpre, gpost = _row(norm_mix_pre[l]), _row(norm_mix_post[l])
        if l % 2 == 0:
            e = l // 2
            w_main, w_g, w_gu, b_g, w_o = _prep_even(w_in_even[e], w_gate_up[e], b_gate[e],
                                                     w_out_even[e])
            gn = _row(gla_norm[e])
            qa, kv, qb, kb, vb, rb, gb = _inproj_even(xp.reshape(nb * S, D_MODEL), gpre,
                                                      w_main, w_g, w_gu, b_g)
            r3 = lambda a: a.reshape(nb, S, a.shape[-1])
            oa = _swa_prompt(r3(qa), r3(kv), attn_sinks[e])
            ob, st_t = _gla_prompt(r3(qb), r3(kb), r3(vb), r3(gb), r3(rb), gn)
            mix_p = ((oa, 0), (ob, 0))
            kv_last = r3(kv)[:, S - WINDOW:, :]
            ks_p.append(kv_last[..., :LANES].reshape(nb, WINDOW, HKV_A, HD_A))
            vs_p.append(kv_last[..., LANES:].reshape(nb, WINDOW, HKV_A, HD_A))
            gs_p.append(st_t.transpose(0, 2, 1).reshape(nb, H_B, DK_B, DV_B))
            qa, kv, qb, kb, vb, rb, gb = _inproj_even(xs.reshape(L * ns, D_MODEL), gpre,
                                                      w_main, w_g, w_gu, b_g)
            feat_major = lambda c: c.transpose(0, 2, 3, 1).reshape(ns, LANES, WINDOW)
            oa = _swa_decode(qa, kv, feat_major(cache_swa_k[e]), feat_major(cache_swa_v[e]),
                             attn_sinks[e], ns=ns, L=L)
            kv_new = kv.reshape(L, ns, 2, HKV_A, HD_A).transpose(2, 1, 0, 3, 4)
            nk = jnp.concatenate([cache_swa_k[e][:, L:], kv_new[0]], axis=1)
            nv = jnp.concatenate([cache_swa_v[e][:, L:], kv_new[1]], axis=1)
            ob, s1 = _gla_decode(qb, kb, vb, gb, rb, gn, state_gla[e].reshape(ns, QB_W, DV_B),
                                 ns=ns, L=L)
            mix_s = ((oa.reshape(1, L * ns, QA_W), 0), (ob.reshape(1, L * ns, VB_W), 0))
            ks_s.append(nk.reshape(ns, WINDOW, HKV_A, HD_A))
            vs_s.append(nv.reshape(ns, WINDOW, HKV_A, HD_A))
            gs_s.append(s1.reshape(ns, H_B, DK_B, DV_B))
        else:
            o = l // 2
            w_in = w_in_odd[o].astype(BF16)
            w_o = w_out_odd[o].astype(BF16)
            y, st = _odd_in(xp, jnp.zeros((nb, past_p, D_MODEL), F32), gpre, w_in, conv_w_odd[o],
                            tm=min(TM_ODD, S), rs=1)
            mix_p = ((y, 0), (y, 1))
            cs_p.append(st[:, past_p - 2:, :])
            past = state_conv[o].transpose(1, 0, 2).reshape(1, past_s, D_MODEL)
            y, st = _odd_in(xs, past, gpre, w_in, conv_w_odd[o], tm=L * ns, rs=ns)
            mix_s = ((y, 0), (y, 1))
            cs_s.append(st.reshape(2, ns, D_MODEL).transpose(1, 0, 2))

        gmix = gpost
        gpre, gpost = _row(norm_ffn_pre[l]), _row(norm_ffn_post[l])
        cb = _row(ffn_conv_b[l])
        xp, st = _ffn(xp, mix_p[0], mix_p[1], w_o, gmix, jnp.zeros((nb, past_p, F2), F32), gpre,
                      w_up_all, ffn_conv_w[l], cb, w_dn_all, gpost, l, tm=tm_p, rs=1)
        fs_p.append(st[:, past_p - 2:, :])
        past = state_ffn[l].transpose(1, 0, 2).reshape(1, past_s, F2)
        xs, st = _ffn(xs, mix_s[0], mix_s[1], w_o, gmix, past, gpre, w_up_all, ffn_conv_w[l], cb,
                      w_dn_all, gpost, l, tm=L * ns, rs=ns)
        fs_s.append(st.reshape(2, ns, F2).transpose(1, 0, 2))

    y_sample = xs.reshape(L, ns, D_MODEL).transpose(1, 0, 2)
    return (xp, y_sample, jnp.stack(ks_p), jnp.stack(vs_p), jnp.stack(gs_p), jnp.stack(cs_p),
            jnp.stack(fs_p), jnp.stack(ks_s), jnp.stack(vs_s), jnp.stack(gs_s), jnp.stack(cs_s),
            jnp.stack(fs_s))
```

```python
import functools

import numpy as np
import jax
import jax.numpy as jnp
from jax import lax
from jax.experimental import pallas as pl
from jax.experimental.pallas import tpu as pltpu

F32 = jnp.float32
BF16 = jnp.bfloat16

D_MODEL = 1024
WINDOW = 128
HD_A = 64
HQ_A = 8
HKV_A = 2
G_A = HQ_A // HKV_A
H_B = 4
DK_B = 64
DV_B = 128
GATE_RANK = 16
GATE_TAU = 16.0
GLA_CHUNK = 64
D_FF = ((8 * D_MODEL // 3 + 127) // 128) * 128
F2 = 2 * D_FF
EPS = 1e-6
GELU_C = float(np.sqrt(2.0 / np.pi))
GELU_A = 0.044715

QA_W = HQ_A * HD_A
KV_W = 2 * HKV_A * HD_A
QB_W = H_B * DK_B
VB_W = H_B * DV_B
MAIN_W = QA_W + KV_W + 2 * QB_W + 2 * VB_W

LANES = 128
SUBLANES = 8
VMEM_LIMIT = 56 * 1024 * 1024

TM_TOK = 512
TM_INPROJ = 1024
TM_ODD = 1024
TM_ATTN = 1024
GLA_GROUP = 128
ODD_TC = 256
FFN_TF = 256
FFN_ELEM_DTYPE = BF16
SEQ_BLOCK = 32
DEC_UNROLL = 8
DEC_KEYS = 2 * WINDOW


def _cparams(*sem):
    return pltpu.CompilerParams(dimension_semantics=sem, vmem_limit_bytes=VMEM_LIMIT)


def _rms(x, g):
    return x * lax.rsqrt(jnp.mean(x * x, axis=-1, keepdims=True) + EPS) * g


def _dot(a, b):
    return jnp.dot(a, b, preferred_element_type=F32)


def _dot_nt(a, b):
    return lax.dot_general(a, b, (((1,), (1,)), ((), ())), preferred_element_type=F32)


def _dot_tn(a, b):
    return lax.dot_general(a, b, (((0,), (0,)), ((), ())), preferred_element_type=F32)


def _split3(x):
    hi = x.astype(BF16)
    r1 = x - hi.astype(F32)
    mid = r1.astype(BF16)
    lo = (r1 - mid.astype(F32)).astype(BF16)
    return hi, mid, lo


def _inproj_even_kernel(x_ref, gpre_ref, w_ref, wg_ref, wgu_ref, bg_ref,
                        qa_ref, kv_ref, qb_ref, kb_ref, vb_ref, rb_ref, gb_ref):
    h = _rms(x_ref[...], gpre_ref[...]).astype(BF16)

    def mm(lo, width):
        return _dot(h, w_ref[:, lo:lo + width])

    lo = 0
    qa_ref[...] = (mm(lo, QA_W) * (HD_A ** -0.5)).astype(BF16)
    lo += QA_W
    kv_ref[...] = mm(lo, KV_W)
    lo += KV_W
    qb_ref[...] = mm(lo, QB_W) * (DK_B ** -0.5)
    lo += QB_W
    kb_ref[...] = mm(lo, QB_W)
    lo += QB_W
    vb_ref[...] = mm(lo, VB_W).astype(BF16)
    lo += VB_W
    rb_ref[...] = mm(lo, VB_W)
    glr = _dot(h, wg_ref[...]).astype(BF16)
    z = _dot(glr, wgu_ref[...]) + bg_ref[...]
    gb_ref[...] = jax.nn.log_sigmoid(z) * (1.0 / GATE_TAU)


def _inproj_even(x2d, gpre, w_main, w_g, w_gu, b_g):
    T = x2d.shape[0]
    tm = min(TM_INPROJ, T)
    row = lambda w: pl.BlockSpec((tm, w), lambda i: (i, 0))
    full = lambda a: pl.BlockSpec(a.shape, lambda i: (0,) * a.ndim)
    out_shape = (
        jax.ShapeDtypeStruct((T, QA_W), BF16), jax.ShapeDtypeStruct((T, KV_W), F32),
        jax.ShapeDtypeStruct((T, QB_W), F32), jax.ShapeDtypeStruct((T, QB_W), F32),
        jax.ShapeDtypeStruct((T, VB_W), BF16), jax.ShapeDtypeStruct((T, VB_W), F32),
        jax.ShapeDtypeStruct((T, QB_W), F32))
    return pl.pallas_call(
        _inproj_even_kernel, grid=(T // tm,),
        in_specs=[row(D_MODEL), full(gpre), full(w_main), full(w_g), full(w_gu), full(b_g)],
        out_specs=(row(QA_W), row(KV_W), row(QB_W), row(QB_W), row(VB_W), row(VB_W), row(QB_W)),
        out_shape=out_shape, compiler_params=_cparams("parallel"), name="inproj_even",
    )(x2d, gpre, w_main, w_g, w_gu, b_g)


def _conv3_from_buf(buf, cw, cur, tm, past, rs, dtype=F32):
    cw = cw.astype(dtype)
    y = cw[0:1, :] * buf[past - 2 * rs:past - 2 * rs + tm, :].astype(dtype)
    y = y + cw[1:2, :] * buf[past - rs:past - rs + tm, :].astype(dtype)
    return y + cw[2:3, :] * cur.astype(dtype)


def _odd_in_kernel(x_ref, past_ref, gpre_ref, win_ref, cw_ref, y_ref, st_ref, carry, buf,
                   *, tm, past, rs, tc):
    @pl.when(pl.program_id(1) == 0)
    def _():
        carry[...] = past_ref[0]

    h = _rms(x_ref[0], gpre_ref[...]).astype(BF16)
    for c in range(D_MODEL // tc):
        lo = c * tc
        bg = _dot(h, win_ref[:, lo:lo + tc])
        cu = _dot(h, win_ref[:, D_MODEL + lo:D_MODEL + lo + tc]) * \
            _dot(h, win_ref[:, 2 * D_MODEL + lo:2 * D_MODEL + lo + tc])
        buf[0:past, :] = carry[:, lo:lo + tc]
        buf[past:past + tm, :] = cu
        carry[:, lo:lo + tc] = buf[tm:tm + past, :]
        z = _conv3_from_buf(buf, cw_ref[:, lo:lo + tc], cu, tm, past, rs)
        y_ref[0, :, lo:lo + tc] = (bg * z).astype(BF16)
    st_ref[0] = carry[...]


def _odd_in(x3d, past0, gpre, w_in, conv_w, *, tm, rs):
    nb, rows, _ = x3d.shape
    past = past0.shape[1]
    tc = ODD_TC
    kern = functools.partial(_odd_in_kernel, tm=tm, past=past, rs=rs, tc=tc)
    full = lambda a: pl.BlockSpec(a.shape, lambda b, j: (0,) * a.ndim)
    return pl.pallas_call(
        kern, grid=(nb, rows // tm),
        in_specs=[pl.BlockSpec((1, tm, D_MODEL), lambda b, j: (b, j, 0)),
                  pl.BlockSpec((1, past, D_MODEL), lambda b, j: (b, 0, 0)),
                  full(gpre), full(w_in), full(conv_w)],
        out_specs=(pl.BlockSpec((1, tm, D_MODEL), lambda b, j: (b, j, 0)),
                   pl.BlockSpec((1, past, D_MODEL), lambda b, j: (b, 0, 0))),
        out_shape=(jax.ShapeDtypeStruct((nb, rows, D_MODEL), BF16),
                   jax.ShapeDtypeStruct((nb, past, D_MODEL), F32)),
        scratch_shapes=[pltpu.VMEM((past, D_MODEL), F32), pltpu.VMEM((past + tm, tc), F32)],
        compiler_params=_cparams("parallel", "arbitrary"), name="odd_in",
    )(x3d, past0, gpre, w_in, conv_w)


def _ffn_kernel(x_ref, ya_ref, yb_ref, wo_ref, gmix_ref, past_ref, gpre_ref, wup_ref, cw_ref,
                cb_ref, wdn_ref, gpost_ref, gk_ref, o_ref, st_ref,
                carry, buf_g, buf_v, acc, h_scr, act, x1_scr, *, tm, past, rs, tf):
    @pl.when(pl.program_id(1) == 0)
    def _():
        carry[...] = past_ref[0]

    mix = _dot(ya_ref[0], wo_ref[0]) + _dot(yb_ref[0], wo_ref[1])
    x1_scr[...] = x_ref[0] + _rms(mix, gmix_ref[...])
    h_scr[...] = _rms(x1_scr[...], gpre_ref[...]).astype(BF16)
    nf = D_FF // tf

    def up(c):
        for lo, buf in ((c * tf, buf_g), (D_FF + c * tf, buf_v)):
            buf[c % 2, 0:past, :] = carry[:, lo:lo + tf]
            buf[c % 2, past:past + tm, :] = _dot(h_scr[...], wup_ref[:, lo:lo + tf])
            carry[:, lo:lo + tf] = buf[c % 2, tm:tm + past, :]

    gelu_c = gk_ref[0:1, 0:1].astype(FFN_ELEM_DTYPE)
    gelu_ca = gk_ref[0:1, 1:2].astype(FFN_ELEM_DTYPE)

    def elem(c):
        halves = []
        for lo, buf, scale in ((c * tf, buf_g, 1.0), (D_FF + c * tf, buf_v, 0.5)):
            b = buf.at[c % 2]
            halves.append(_conv3_from_buf(b, cw_ref[:, lo:lo + tf] * scale, b[past:past + tm, :],
                                          tm, past, rs, FFN_ELEM_DTYPE)
                          + (cb_ref[:, lo:lo + tf] * scale).astype(FFN_ELEM_DTYPE))
        x, half_v = halves
        t = jnp.tanh(x * (x * x * gelu_ca + gelu_c))
        act[c % 2] = ((x * t + x) * half_v).astype(BF16)

    def down(c):
        part = _dot(act[c % 2], wdn_ref[c * tf:(c + 1) * tf, :])
        if c == 0:
            acc[...] = part
        else:
            acc[...] += part

    for c in range(nf + 2):
        if c < nf:
            up(c)
        if 1 <= c <= nf:
            elem(c - 1)
        if c >= 2:
            down(c - 2)
    o_ref[0] = x1_scr[...] + _rms(acc[...], gpost_ref[...])
    st_ref[0] = carry[...]


def _ffn(x3d, ya, yb, w_o, gmix, past0, gpre, w_up, conv_w, conv_b, w_dn, gpost, layer, *, tm, rs):
    nb, rows, _ = x3d.shape
    past = past0.shape[1]
    half = w_o.shape[0] // 2
    kern = functools.partial(_ffn_kernel, tm=tm, past=past, rs=rs, tf=FFN_TF)
    gelu_k = jnp.zeros((1, LANES), F32).at[0, 0].set(GELU_C).at[0, 1].set(GELU_C * GELU_A)
    w_o2 = w_o.reshape(2, half, D_MODEL)
    once = dict(pipeline_mode=pl.Buffered(1))
    full = lambda a: pl.BlockSpec(a.shape, lambda b, j: (0,) * a.ndim, **once)
    per_layer = lambda a: pl.BlockSpec((None,) + a.shape[1:], lambda b, j: (layer, 0, 0), **once)
    tok = lambda w, cblk=0: pl.BlockSpec((1, tm, w), lambda b, j: (b, j, cblk))
    return pl.pallas_call(
        kern, grid=(nb, rows // tm),
        in_specs=[tok(D_MODEL), tok(half, ya[1]), tok(half, yb[1]), full(w_o2), full(gmix),
                  pl.BlockSpec((1, past, F2), lambda b, j: (b, 0, 0)),
                  full(gpre), per_layer(w_up), full(conv_w), full(conv_b), per_layer(w_dn),
                  full(gpost), full(gelu_k)],
        out_specs=(tok(D_MODEL), pl.BlockSpec((1, past, F2), lambda b, j: (b, 0, 0))),
        out_shape=(jax.ShapeDtypeStruct((nb, rows, D_MODEL), F32),
                   jax.ShapeDtypeStruct((nb, past, F2), F32)),
        scratch_shapes=[pltpu.VMEM((past, F2), F32), pltpu.VMEM((2, past + tm, FFN_TF), F32),
                        pltpu.VMEM((2, past + tm, FFN_TF), F32), pltpu.VMEM((tm, D_MODEL), F32),
                        pltpu.VMEM((tm, D_MODEL), BF16), pltpu.VMEM((2, tm, FFN_TF), BF16),
                        pltpu.VMEM((tm, D_MODEL), F32)],
        compiler_params=_cparams("parallel", "arbitrary"), name="conv_ffn",
    )(x3d, ya[0], yb[0], w_o2, gmix, past0, gpre, w_up, conv_w, conv_b, w_dn, gpost, gelu_k)


def _swa_bias_prompt():
    qi = np.arange(WINDOW)[:, None]
    sj = np.arange(2 * WINDOW)[None, :]
    dist = WINDOW + qi - sj
    valid = (dist >= 0) & (dist <= WINDOW)
    slopes = 2.0 ** (-8.0 * np.arange(1, HQ_A + 1) / HQ_A)
    bias = np.where(valid[None], -slopes[:, None, None] * dist[None].astype(np.float64), -np.inf)
    return jnp.asarray(bias, dtype=F32)


def _swa_prompt_kernel(sink_ref, q_ref, kvc_ref, kvp_ref, bias_ref, o_ref, *, tq):
    i = pl.program_id(1)
    kvc = kvc_ref[0]
    kvp = kvp_ref[0]
    kcat = jnp.concatenate([kvp[:, 0:LANES], kvc[:, 0:LANES]], axis=0).astype(BF16)
    vcat = jnp.concatenate([kvp[:, LANES:], kvc[:, LANES:]], axis=0).astype(BF16)
    lane = lax.broadcasted_iota(jnp.int32, (WINDOW, LANES), 1)
    col = lax.broadcasted_iota(jnp.int32, (WINDOW, 2 * WINDOW), 1)
    for j in range(tq // WINDOW):
        keys = kcat[j * WINDOW:(j + 2) * WINDOW, :]
        vals = vcat[j * WINDOW:(j + 2) * WINDOW, :]
        for g in range(G_A):
            q2 = q_ref[0, j * WINDOW:(j + 1) * WINDOW, g * LANES:(g + 1) * LANES]
            outs = []
            for kv in range(HKV_A):
                hq = kv * G_A + g
                in_head = (lane < HD_A) if kv == 0 else (lane >= HD_A)
                qm = jnp.where(in_head, q2, jnp.zeros_like(q2))
                s = _dot_nt(qm, keys) + bias_ref[hq]
                if j == 0:
                    s = jnp.where(jnp.logical_and(i == 0, col < WINDOW), -jnp.inf, s)
                sink = sink_ref[hq]
                m = jnp.maximum(jnp.max(s, axis=-1, keepdims=True), sink)
                p = jnp.exp(s - m)
                den = jnp.sum(p, axis=-1, keepdims=True) + jnp.exp(sink - m)
                outs.append(_dot(p.astype(BF16), vals) / den)
            o2 = jnp.where(lane < HD_A, outs[0], outs[1])
            o_ref[0, j * WINDOW:(j + 1) * WINDOW, g * LANES:(g + 1) * LANES] = o2.astype(BF16)


GLA_LEVELS = (0, 1, 2, 4, 8, 16, 32)
LOG2E = 1.4426950408889634


def _gla_constants():
    r = np.arange(GLA_GROUP)
    i, j = r[:, None], r[None, :]
    ranges = [(i >= j) & (i // GLA_CHUNK == j // GLA_CHUNK)]
    masks = [i == j]
    for s in GLA_LEVELS[1:]:
        mid = (i // (2 * s)) * (2 * s) + s
        upper = (i % (2 * s) >= s) & (j >= mid) & (j <= i)
        lower = (i % (2 * s) < s) & (j > i) & (j < mid)
        ranges.append(upper | lower)
        masks.append((i // (2 * s) == j // (2 * s)) & (i % (2 * s) >= s) & (j % (2 * s) < s))
    ones_bd = (np.arange(QB_W)[:, None] // DK_B) == (np.arange(VB_W)[None, :] // DV_B)
    return (jnp.asarray(np.concatenate(ranges, axis=0), dtype=BF16),
            jnp.asarray(ones_bd, dtype=BF16), jnp.asarray(np.stack(masks), dtype=F32))


def _head_norm_gate(o, r, gn):
    parts = []
    for h in range(H_B):
        oh = o[:, h * DV_B:(h + 1) * DV_B]
        parts.append(oh * lax.rsqrt(jnp.mean(oh * oh, axis=-1, keepdims=True) + EPS))
    return jnp.concatenate(parts, axis=-1) * gn * jax.nn.silu(r)


def _gla_prompt_kernel(q_ref, k_ref, v_ref, g_ref, r_ref, gn_ref, rng_ref, lvl_ref,
                       o_ref, st_ref, state, b_scr, o_scr, qk_scr, *, tm):
    @pl.when(pl.program_id(1) == 0)
    def _():
        state[...] = jnp.zeros_like(state)

    q = q_ref[0]
    k = k_ref[0]
    v = v_ref[0]
    lane = lax.broadcasted_iota(jnp.int32, (1, LANES), 1)
    head_lanes = (lane < DK_B, lane >= DK_B)
    head_cols = lax.broadcasted_iota(jnp.int32, (1, QB_W), 1) // DK_B

    g2 = g_ref[0] * LOG2E
    ghi = g2.astype(BF16)
    glo = (g2 - ghi.astype(F32)).astype(BF16)
    nlev = len(GLA_LEVELS)
    for gi in range(tm // GLA_GROUP):
        rows = slice(gi * GLA_GROUP, (gi + 1) * GLA_GROUP)
        d_all = _dot(rng_ref[...], ghi[rows]) + _dot(rng_ref[...], glo[rows])
        b_scr[rows, :] = d_all[0:GLA_GROUP]
        qk_scr[0, 0, rows, :] = q[rows].astype(BF16)
        qk_scr[0, 1, rows, :] = k[rows].astype(BF16)
        for li in range(1, nlev):
            e = jnp.exp2(d_all[li * GLA_GROUP:(li + 1) * GLA_GROUP])
            qk_scr[li, 0, rows, :] = (q[rows] * e).astype(BF16)
            qk_scr[li, 1, rows, :] = (k[rows] * e).astype(BF16)
    b = b_scr[...]

    for gi in range(tm // GLA_GROUP):
        rows = slice(gi * GLA_GROUP, (gi + 1) * GLA_GROUP)
        for pi in range(H_B // 2):
            pr = slice(pi * LANES, (pi + 1) * LANES)
            a = None
            for li in range(nlev):
                ke = qk_scr[li, 1, rows, pr]
                kem = jnp.concatenate([jnp.where(sel, ke, jnp.zeros_like(ke)) for sel in head_lanes],
                                      axis=0)
                term = _dot_nt(qk_scr[li, 0, rows, pr], kem)
                keep = lvl_ref[li] > 0.5
                keep = jnp.concatenate([keep, keep], axis=1)
                a = jnp.where(keep, term, 0.0 if a is None else a)
            a = a.astype(BF16)
            for par in range(2):
                cols = slice((2 * pi + par) * DV_B, (2 * pi + par + 1) * DV_B)
                o_scr[rows, cols] = _dot(a[:, par * GLA_GROUP:(par + 1) * GLA_GROUP], v[rows, cols])

    for c in range(tm // GLA_CHUNK):
        rows = slice(c * GLA_CHUNK, (c + 1) * GLA_CHUNK)
        bc = b[rows]
        b_end = bc[GLA_CHUNK - 1:GLA_CHUNK, :]
        qe = (q[rows] * jnp.exp2(bc)).astype(BF16)
        ke = (k[rows] * jnp.exp2(b_end - bc)).astype(BF16)
        st = state[...]
        qm = jnp.concatenate([jnp.where(head_cols == h, qe, jnp.zeros_like(qe))
                              for h in range(H_B)], axis=0)
        km = jnp.concatenate([jnp.where(head_cols == h, ke, jnp.zeros_like(ke))
                              for h in range(H_B)], axis=0)
        vm = jnp.concatenate([v[rows, h * DV_B:(h + 1) * DV_B] for h in range(H_B)], axis=0)
        oi = _dot_nt(qm, st.astype(BF16))
        for h in range(H_B):
            o_scr[rows, h * DV_B:(h + 1) * DV_B] += oi[h * GLA_CHUNK:(h + 1) * GLA_CHUNK]
        state[...] = st * jnp.exp2(b_end) + _dot_tn(vm, km)

    o_ref[0] = _head_norm_gate(o_scr[...], r_ref[0], gn_ref[...]).astype(BF16)
    st_ref[0] = state[...]


def _attn_prompt_kernel(sink_ref, qa_ref, kvc_ref, kvp_ref, bias_ref,
                        q_ref, k_ref, v_ref, g_ref, r_ref, gn_ref, rng_ref, lvl_ref,
                        oa_ref, ob_ref, st_ref, state, b_scr, o_scr, qk_scr, *, tm):
    _gla_prompt_kernel(q_ref, k_ref, v_ref, g_ref, r_ref, gn_ref, rng_ref, lvl_ref,
                       ob_ref, st_ref, state, b_scr, o_scr, qk_scr, tm=tm)
    _swa_prompt_kernel(sink_ref, qa_ref, kvc_ref, kvp_ref, bias_ref, oa_ref, tq=tm)


def _attn_prompt(qa3d, kv3d, sinks, q3d, k3d, v3d, g3d, r3d, gn):
    nb, S, _ = q3d.shape
    tm = min(TM_ATTN, S)
    bias = _swa_bias_prompt()
    rng, _, lvl = _gla_constants()
    kern = functools.partial(_attn_prompt_kernel, tm=tm)
    blocks_per_tile = tm // WINDOW
    tok = lambda w: pl.BlockSpec((1, tm, w), lambda b, j: (b, j, 0))
    full = lambda a: pl.BlockSpec(a.shape, lambda b, j: (0,) * a.ndim)
    prev_kv = pl.BlockSpec((1, WINDOW, KV_W),
                           lambda b, j: (b, jnp.maximum(j * blocks_per_tile - 1, 0), 0))
    return pl.pallas_call(
        kern, grid=(nb, S // tm),
        in_specs=[pl.BlockSpec(memory_space=pltpu.SMEM), tok(QA_W), tok(KV_W), prev_kv, full(bias),
                  tok(QB_W), tok(QB_W), tok(VB_W), tok(QB_W), tok(VB_W),
                  full(gn), full(rng), full(lvl)],
        out_specs=(tok(QA_W), tok(VB_W), pl.BlockSpec((1, DV_B, QB_W), lambda b, j: (b, 0, 0))),
        out_shape=(jax.ShapeDtypeStruct((nb, S, QA_W), BF16),
                   jax.ShapeDtypeStruct((nb, S, VB_W), BF16),
                   jax.ShapeDtypeStruct((nb, DV_B, QB_W), F32)),
        scratch_shapes=[pltpu.VMEM((DV_B, QB_W), F32), pltpu.VMEM((tm, QB_W), F32),
                        pltpu.VMEM((tm, VB_W), F32),
                        pltpu.VMEM((len(GLA_LEVELS), 2, tm, QB_W), BF16)],
        compiler_params=_cparams("parallel", "arbitrary"), name="attn_prompt",
    )(sinks, qa3d, kv3d, kv3d, bias, q3d, k3d, v3d, g3d, r3d, gn, rng, lvl)


def _swa_bias_decode(L):
    rows = HQ_A * L
    kpad = DEC_KEYS
    hq = np.arange(rows)[:, None] // L
    t = np.arange(rows)[:, None] % L
    s = np.arange(kpad)[None, :]
    dist = WINDOW + t - s
    valid = (dist >= 0) & (dist <= WINDOW) & (s < WINDOW + L)
    slopes = 2.0 ** (-8.0 * (hq + 1) / HQ_A)
    return jnp.asarray(np.where(valid, -slopes * dist, -np.inf), dtype=F32)


def _swa_decode_kernel(q_ref, kvn_ref, ck_ref, cv_ref, bias_ref, sink_ref,
                       o_ref, nk_ref, nv_ref, qs, os_, kc, vc, kn, vn, *, ns, L, sb):
    i = pl.program_id(0)
    rows_q = HQ_A * L
    lane = lax.broadcasted_iota(jnp.int32, (1, LANES), 1)

    @pl.when(i == 0)
    def _():
        q = q_ref[...].astype(F32)
        for kv in range(HKV_A):
            sel = (lane < HD_A) if kv == 0 else (lane >= HD_A)
            for g in range(G_A):
                base = (kv * G_A + g) * L * ns
                qs[base:base + L * ns, :] = jnp.where(sel, q[:, g * LANES:(g + 1) * LANES], 0.0)
        kc[...] = jnp.zeros_like(kc)
        vc[...] = jnp.zeros_like(vc)
        kn[...] = kvn_ref[:, 0:LANES]
        vn[...] = kvn_ref[:, LANES:]

    bias = bias_ref[...]
    sink = sink_ref[...]

    def body(t, carry):
        lhs = []
        for u in range(DEC_UNROLL):
            s = t * DEC_UNROLL + u
            seq = i * sb + s
            kc[u, 0:WINDOW, :] = ck_ref[s]
            vc[u, 0:WINDOW, :] = cv_ref[s]
            kc[u, WINDOW:WINDOW + L, :] = kn[pl.ds(seq, L, stride=ns), :]
            vc[u, WINDOW:WINDOW + L, :] = vn[pl.ds(seq, L, stride=ns), :]
            nk_ref[s] = kc[u, L:L + WINDOW, :]
            nv_ref[s] = vc[u, L:L + WINDOW, :]
            lhs.append(qs[pl.ds(seq, rows_q, stride=ns), :].astype(BF16))
        sc = jnp.einsum("uqd,ukd->uqk", jnp.stack(lhs), kc[...].astype(BF16),
                        preferred_element_type=F32) + bias
        m = jnp.maximum(jnp.max(sc, axis=-1, keepdims=True), sink)
        p = jnp.exp(sc - m)
        den = jnp.sum(p, axis=-1, keepdims=True) + jnp.exp(sink - m)
        res = jnp.einsum("uqk,ukd->uqd", p.astype(BF16), vc[...].astype(BF16),
                         preferred_element_type=F32) / den
        half = rows_q // 2
        for u in range(DEC_UNROLL):
            seq = i * sb + t * DEC_UNROLL + u
            os_[pl.ds(seq, half, stride=ns), :] = jnp.where(lane < HD_A, res[u, 0:half],
                                                            res[u, half:])
        return carry

    lax.fori_loop(0, sb // DEC_UNROLL, body, 0)

    @pl.when(i == pl.num_programs(0) - 1)
    def _():
        for g in range(G_A):
            o_ref[:, g * LANES:(g + 1) * LANES] = os_[g * L * ns:(g + 1) * L * ns, :].astype(BF16)


def _swa_decode(q_tm, kvn_tm, cache_k, cache_v, sinks, *, ns, L):
    sb = min(SEQ_BLOCK, ns)
    bias = _swa_bias_decode(L)
    sink_col = jnp.broadcast_to(jnp.repeat(sinks.astype(F32), L)[:, None], (HQ_A * L, 1))
    kern = functools.partial(_swa_decode_kernel, ns=ns, L=L, sb=sb)
    full = lambda a: pl.BlockSpec(a.shape, lambda i: (0,) * a.ndim)
    cache = pl.BlockSpec((sb, WINDOW, LANES), lambda i: (i, 0, 0))
    kpad = DEC_KEYS
    return pl.pallas_call(
        kern, grid=(ns // sb,),
        in_specs=[full(q_tm), full(kvn_tm), cache, cache, full(bias), full(sink_col)],
        out_specs=(pl.BlockSpec((L * ns, QA_W), lambda i: (0, 0)), cache, cache),
        out_shape=(jax.ShapeDtypeStruct((L * ns, QA_W), BF16),
                   jax.ShapeDtypeStruct(cache_k.shape, F32),
                   jax.ShapeDtypeStruct(cache_v.shape, F32)),
        scratch_shapes=[pltpu.VMEM((HQ_A * L * ns, LANES), F32),
                        pltpu.VMEM((G_A * L * ns, LANES), F32),
                        pltpu.VMEM((DEC_UNROLL, kpad, LANES), F32),
                        pltpu.VMEM((DEC_UNROLL, kpad, LANES), F32),
                        pltpu.VMEM((L * ns, LANES), F32), pltpu.VMEM((L * ns, LANES), F32)],
        compiler_params=_cparams("arbitrary"), name="swa_decode",
    )(q_tm, kvn_tm, cache_k, cache_v, bias, sink_col)


def _gla_decode_kernel(q_ref, k_ref, v_ref, g_ref, r_ref, gn_ref, ones_ref, s0_ref,
                       o_ref, s1_ref, qe2, ke2, v2, dec3, oi2, od, *, ns, L, sb):
    i = pl.program_id(0)
    lane = lax.broadcasted_iota(jnp.int32, (1, LANES), 1)
    npair = H_B // 2

    @pl.when(i == 0)
    def _():
        slab = lambda a, t: a[t * ns:(t + 1) * ns, :]
        q, k, g = q_ref[...], k_ref[...], g_ref[...]
        vf = v_ref[...].astype(F32)
        b = [slab(g, 0)]
        for t in range(1, L):
            b.append(b[-1] + slab(g, t))
        for t in range(L):
            acc = jnp.zeros((ns, VB_W), F32)
            for jj in range(t + 1):
                p = (slab(q, t) * slab(k, jj) * jnp.exp(b[t] - b[jj])).astype(BF16)
                acc = acc + _dot(p, ones_ref[...]) * slab(vf, jj)
            od[t * ns:(t + 1) * ns, :] = acc
            qe = slab(q, t) * jnp.exp(b[t])
            ke = slab(k, t) * jnp.exp(b[L - 1] - b[t])
            for par in range(2):
                sel = (lane < DK_B) if par == 0 else (lane >= DK_B)
                base = (par * L + t) * ns
                for pi in range(npair):
                    pr = slice(pi * LANES, (pi + 1) * LANES)
                    qe2[pi, base:base + ns, :] = jnp.where(sel, qe[:, pr], 0.0)
                    ke2[pi, base:base + ns, :] = jnp.where(sel, ke[:, pr], 0.0)
                    h = 2 * pi + par
                    v2[pi, base:base + ns, :] = slab(vf, t)[:, h * DV_B:(h + 1) * DV_B]
        dec3[...] = jnp.zeros_like(dec3)
        hi, mid, lo = _split3(jnp.exp(b[L - 1]))
        for pi in range(npair):
            pr = slice(pi * LANES, (pi + 1) * LANES)
            dec3[pi, 0:ns, :] = hi[:, pr].astype(F32)
            dec3[pi, ns:2 * ns, :] = mid[:, pr].astype(F32)
            dec3[pi, 2 * ns:3 * ns, :] = lo[:, pr].astype(F32)

    ones8 = jnp.ones((2 * L, LANES), BF16)

    def one_seq(s):
        seq = i * sb + s
        for pi in range(npair):
            take = lambda ref: ref[pi, pl.ds(seq, 2 * L, stride=ns), :].astype(BF16)
            st_p = s0_ref[s, pi * LANES:(pi + 1) * LANES, :]
            oi2[pi, pl.ds(seq, 2 * L, stride=ns), :] = _dot(take(qe2), st_p.astype(BF16))
            dcol = _dot_tn(take(dec3), ones8)
            upd = _dot_tn(take(ke2), take(v2))
            s1_ref[s, pi * LANES:(pi + 1) * LANES, :] = dcol * st_p + upd

    def body(t, carry):
        for u in range(DEC_UNROLL):
            one_seq(t * DEC_UNROLL + u)
        return carry

    lax.fori_loop(0, sb // DEC_UNROLL, body, 0)

    @pl.when(i == pl.num_programs(0) - 1)
    def _():
        for t in range(L):
            parts = []
            for h in range(H_B):
                base = ((h % 2) * L + t) * ns
                parts.append(oi2[h // 2, base:base + ns, :])
            o = jnp.concatenate(parts, axis=-1) + od[t * ns:(t + 1) * ns, :]
            o_ref[t * ns:(t + 1) * ns, :] = _head_norm_gate(
                o, r_ref[t * ns:(t + 1) * ns, :], gn_ref[...]).astype(BF16)


def _gla_decode(q_tm, k_tm, v_tm, g_tm, r_tm, gn, s0, *, ns, L):
    sb = min(SEQ_BLOCK, ns)
    assert 2 * L == SUBLANES
    _, ones_bd, _ = _gla_constants()
    kern = functools.partial(_gla_decode_kernel, ns=ns, L=L, sb=sb)
    full = lambda a: pl.BlockSpec(a.shape, lambda i: (0,) * a.ndim)
    st = pl.BlockSpec((sb, QB_W, DV_B), lambda i: (i, 0, 0))
    rows2 = 2 * L * ns
    return pl.pallas_call(
        kern, grid=(ns // sb,),
        in_specs=[full(q_tm), full(k_tm), full(v_tm), full(g_tm), full(r_tm), full(gn),
                  full(ones_bd), st],
        out_specs=(pl.BlockSpec((L * ns, VB_W), lambda i: (0, 0)), st),
        out_shape=(jax.ShapeDtypeStruct((L * ns, VB_W), BF16),
                   jax.ShapeDtypeStruct(s0.shape, F32)),
        scratch_shapes=[pltpu.VMEM((H_B // 2, rows2, LANES), F32) for _ in range(5)]
        + [pltpu.VMEM((L * ns, VB_W), F32)],
        compiler_params=_cparams("arbitrary"), name="gla_decode",
    )(q_tm, k_tm, v_tm, g_tm, r_tm, gn, ones_bd, s0)


def _qa_perm():
    return np.asarray([(kv * G_A + g) * HD_A + d
                       for g in range(G_A) for kv in range(HKV_A) for d in range(HD_A)])


def _prep_even(w_in, w_gate_up, b_gate, w_out):
    perm = _qa_perm()
    w_main = jnp.concatenate([w_in[:, :QA_W][:, perm], w_in[:, QA_W:MAIN_W]], axis=1).astype(BF16)
    w_g = jnp.pad(w_in[:, MAIN_W:], ((0, 0), (0, LANES - GATE_RANK))).astype(BF16)
    w_gu = jnp.pad(w_gate_up, ((0, LANES - GATE_RANK), (0, 0))).astype(BF16)
    w_o = jnp.concatenate([w_out[:QA_W][perm], w_out[QA_W:]], axis=0).astype(BF16)
    return w_main, w_g, w_gu, b_gate.reshape(1, -1), w_o


def _row(v):
    return v.reshape(1, -1)


def kernel(x_prompt, x_sample, cache_swa_k, cache_swa_v, state_gla, state_conv, state_ffn,
           norm_mix_pre, norm_mix_post, norm_ffn_pre, norm_ffn_post, w_in_even, w_gate_up, b_gate,
           attn_sinks, gla_norm, w_out_even, w_in_odd, conv_w_odd, w_out_odd, ffn_up, ffn_conv_w,
           ffn_conv_b, ffn_down):
    nb, S, _ = x_prompt.shape
    ns, L, _ = x_sample.shape
    depth = norm_mix_pre.shape[0]

    xp = x_prompt
    xs = x_sample.transpose(1, 0, 2).reshape(1, L * ns, D_MODEL)
    past_p = max(SUBLANES, 2)
    past_s = 2 * ns
    tm_p = min(TM_TOK, S)

    w_up_all = ffn_up.astype(BF16)
    w_dn_all = ffn_down.astype(BF16)

    ks_p, vs_p, gs_p, cs_p, fs_p = [], [], [], [], []
    ks_s, vs_s, gs_s, cs_s, fs_s = [], [], [], [], []

    for l in range(depth):
        gpre, gpost = _row(norm_mix_pre[l]), _row(norm_mix_post[l])
        if l % 2 == 0:
            e = l // 2
            w_main, w_g, w_gu, b_g, w_o = _prep_even(w_in_even[e], w_gate_up[e], b_gate[e],
                                                     w_out_even[e])
            gn = _row(gla_norm[e])
            qa, kv, qb, kb, vb, rb, gb = _inproj_even(xp.reshape(nb * S, D_MODEL), gpre,
                                                      w_main, w_g, w_gu, b_g)
            r3 = lambda a: a.reshape(nb, S, a.shape[-1])
            oa, ob, st_t = _attn_prompt(r3(qa), r3(kv), attn_sinks[e],
                                        r3(qb), r3(kb), r3(vb), r3(gb), r3(rb), gn)
            mix_p = ((oa, 0), (ob, 0))
            kv_last = r3(kv)[:, S - WINDOW:, :]
            ks_p.append(kv_last[..., :LANES].reshape(nb, WINDOW, HKV_A, HD_A))
            vs_p.append(kv_last[..., LANES:].reshape(nb, WINDOW, HKV_A, HD_A))
            gs_p.append(st_t.transpose(0, 2, 1).reshape(nb, H_B, DK_B, DV_B))
            qa, kv, qb, kb, vb, rb, gb = _inproj_even(xs.reshape(L * ns, D_MODEL), gpre,
                                                      w_main, w_g, w_gu, b_g)
            oa, nk, nv = _swa_decode(qa, kv, cache_swa_k[e].reshape(ns, WINDOW, LANES),
                                     cache_swa_v[e].reshape(ns, WINDOW, LANES), attn_sinks[e],
                                     ns=ns, L=L)
            ob, s1 = _gla_decode(qb, kb, vb, gb, rb, gn, state_gla[e].reshape(ns, QB_W, DV_B),
                                 ns=ns, L=L)
            mix_s = ((oa.reshape(1, L * ns, QA_W), 0), (ob.reshape(1, L * ns, VB_W), 0))
            ks_s.append(nk.reshape(ns, WINDOW, HKV_A, HD_A))
            vs_s.append(nv.reshape(ns, WINDOW, HKV_A, HD_A))
            gs_s.append(s1.reshape(ns, H_B, DK_B, DV_B))
        else:
            o = l // 2
            w_in = w_in_odd[o].astype(BF16)
            w_o = w_out_odd[o].astype(BF16)
            y, st = _odd_in(xp, jnp.zeros((nb, past_p, D_MODEL), F32), gpre, w_in, conv_w_odd[o],
                            tm=min(TM_ODD, S), rs=1)
            mix_p = ((y, 0), (y, 1))
            cs_p.append(st[:, past_p - 2:, :])
            past = state_conv[o].transpose(1, 0, 2).reshape(1, past_s, D_MODEL)
            y, st = _odd_in(xs, past, gpre, w_in, conv_w_odd[o], tm=L * ns, rs=ns)
            mix_s = ((y, 0), (y, 1))
            cs_s.append(st.reshape(2, ns, D_MODEL).transpose(1, 0, 2))

        gmix = gpost
        gpre, gpost = _row(norm_ffn_pre[l]), _row(norm_ffn_post[l])
        cb = _row(ffn_conv_b[l])
        xp, st = _ffn(xp, mix_p[0], mix_p[1], w_o, gmix, jnp.zeros((nb, past_p, F2), F32), gpre,
                      w_up_all, ffn_conv_w[l], cb, w_dn_all, gpost, l, tm=tm_p, rs=1)
        fs_p.append(st[:, past_p - 2:, :])
        past = state_ffn[l].transpose(1, 0, 2).reshape(1, past_s, F2)
        xs, st = _ffn(xs, mix_s[0], mix_s[1], w_o, gmix, past, gpre, w_up_all, ffn_conv_w[l], cb,
                      w_dn_all, gpost, l, tm=L * ns, rs=ns)
        fs_s.append(st.reshape(2, ns, F2).transpose(1, 0, 2))

    y_sample = xs.reshape(L, ns, D_MODEL).transpose(1, 0, 2)
    return (xp, y_sample, jnp.stack(ks_p), jnp.stack(vs_p), jnp.stack(gs_p), jnp.stack(cs_p),
            jnp.stack(fs_p), jnp.stack(ks_s), jnp.stack(vs_s), jnp.stack(gs_s), jnp.stack(cs_s),
            jnp.stack(fs_s))
```

```python
import functools

import numpy as np
import jax
import jax.numpy as jnp
from jax import lax
from jax.experimental import pallas as pl
from jax.experimental.pallas import tpu as pltpu

F32 = jnp.float32
BF16 = jnp.bfloat16

D_MODEL = 1024
WINDOW = 128
HD_A = 64
HQ_A = 8
HKV_A = 2
G_A = HQ_A // HKV_A
H_B = 4
DK_B = 64
DV_B = 128
GATE_RANK = 16
GATE_TAU = 16.0
GLA_CHUNK = 64
D_FF = ((8 * D_MODEL // 3 + 127) // 128) * 128
F2 = 2 * D_FF
EPS = 1e-6
GELU_C = float(np.sqrt(2.0 / np.pi))
GELU_A = 0.044715

QA_W = HQ_A * HD_A
KV_W = 2 * HKV_A * HD_A
QB_W = H_B * DK_B
VB_W = H_B * DV_B
MAIN_W = QA_W + KV_W + 2 * QB_W + 2 * VB_W

LANES = 128
SUBLANES = 8
VMEM_LIMIT = 56 * 1024 * 1024

TM_TOK = 512
TM_INPROJ = 1024
TM_ODD = 1024
TM_ATTN = 1024
GLA_GROUP = 128
ODD_TC = 256
FFN_TF = 256
FFN_ELEM_DTYPE = BF16
SEQ_BLOCK = 32
DEC_UNROLL = 8
DEC_KEYS = 2 * WINDOW


def _cparams(*sem):
    return pltpu.CompilerParams(dimension_semantics=sem, vmem_limit_bytes=VMEM_LIMIT)


def _rms(x, g):
    return x * lax.rsqrt(jnp.mean(x * x, axis=-1, keepdims=True) + EPS) * g


def _dot(a, b):
    return jnp.dot(a, b, preferred_element_type=F32)


def _dot_nt(a, b):
    return lax.dot_general(a, b, (((1,), (1,)), ((), ())), preferred_element_type=F32)


def _dot_tn(a, b):
    return lax.dot_general(a, b, (((0,), (0,)), ((), ())), preferred_element_type=F32)


def _split3(x):
    hi = x.astype(BF16)
    r1 = x - hi.astype(F32)
    mid = r1.astype(BF16)
    lo = (r1 - mid.astype(F32)).astype(BF16)
    return hi, mid, lo


def _inproj_even_kernel(x_ref, gpre_ref, w_ref, wg_ref, wgu_ref, bg_ref,
                        qa_ref, kv_ref, qb_ref, kb_ref, vb_ref, rb_ref, gb_ref):
    h = _rms(x_ref[...], gpre_ref[...]).astype(BF16)

    def mm(lo, width):
        return _dot(h, w_ref[:, lo:lo + width])

    lo = 0
    qa_ref[...] = (mm(lo, QA_W) * (HD_A ** -0.5)).astype(BF16)
    lo += QA_W
    kv_ref[...] = mm(lo, KV_W)
    lo += KV_W
    qb_ref[...] = mm(lo, QB_W) * (DK_B ** -0.5)
    lo += QB_W
    kb_ref[...] = mm(lo, QB_W)
    lo += QB_W
    vb_ref[...] = mm(lo, VB_W).astype(BF16)
    lo += VB_W
    rb_ref[...] = mm(lo, VB_W)
    glr = _dot(h, wg_ref[...]).astype(BF16)
    z = _dot(glr, wgu_ref[...]) + bg_ref[...]
    gb_ref[...] = jax.nn.log_sigmoid(z) * (1.0 / GATE_TAU)


def _inproj_even(x2d, gpre, w_main, w_g, w_gu, b_g):
    T = x2d.shape[0]
    tm = min(TM_INPROJ, T)
    row = lambda w: pl.BlockSpec((tm, w), lambda i: (i, 0))
    full = lambda a: pl.BlockSpec(a.shape, lambda i: (0,) * a.ndim)
    out_shape = (
        jax.ShapeDtypeStruct((T, QA_W), BF16), jax.ShapeDtypeStruct((T, KV_W), F32),
        jax.ShapeDtypeStruct((T, QB_W), F32), jax.ShapeDtypeStruct((T, QB_W), F32),
        jax.ShapeDtypeStruct((T, VB_W), BF16), jax.ShapeDtypeStruct((T, VB_W), F32),
        jax.ShapeDtypeStruct((T, QB_W), F32))
    return pl.pallas_call(
        _inproj_even_kernel, grid=(T // tm,),
        in_specs=[row(D_MODEL), full(gpre), full(w_main), full(w_g), full(w_gu), full(b_g)],
        out_specs=(row(QA_W), row(KV_W), row(QB_W), row(QB_W), row(VB_W), row(VB_W), row(QB_W)),
        out_shape=out_shape, compiler_params=_cparams("parallel"), name="inproj_even",
    )(x2d, gpre, w_main, w_g, w_gu, b_g)


def _conv3_from_buf(buf, cw, cur, tm, past, rs, dtype=F32):
    cw = cw.astype(dtype)
    y = cw[0:1, :] * buf[past - 2 * rs:past - 2 * rs + tm, :].astype(dtype)
    y = y + cw[1:2, :] * buf[past - rs:past - rs + tm, :].astype(dtype)
    return y + cw[2:3, :] * cur.astype(dtype)


def _odd_in_kernel(x_ref, past_ref, gpre_ref, win_ref, cw_ref, y_ref, st_ref, carry, buf,
                   *, tm, past, rs, tc):
    @pl.when(pl.program_id(1) == 0)
    def _():
        carry[...] = past_ref[0]

    h = _rms(x_ref[0], gpre_ref[...]).astype(BF16)
    for c in range(D_MODEL // tc):
        lo = c * tc
        bg = _dot(h, win_ref[:, lo:lo + tc])
        cu = _dot(h, win_ref[:, D_MODEL + lo:D_MODEL + lo + tc]) * \
            _dot(h, win_ref[:, 2 * D_MODEL + lo:2 * D_MODEL + lo + tc])
        buf[0:past, :] = carry[:, lo:lo + tc]
        buf[past:past + tm, :] = cu
        carry[:, lo:lo + tc] = buf[tm:tm + past, :]
        z = _conv3_from_buf(buf, cw_ref[:, lo:lo + tc], cu, tm, past, rs)
        y_ref[0, :, lo:lo + tc] = (bg * z).astype(BF16)
    st_ref[0] = carry[...]


def _odd_in(x3d, past0, gpre, w_in, conv_w, *, tm, rs):
    nb, rows, _ = x3d.shape
    past = past0.shape[1]
    tc = ODD_TC
    kern = functools.partial(_odd_in_kernel, tm=tm, past=past, rs=rs, tc=tc)
    full = lambda a: pl.BlockSpec(a.shape, lambda b, j: (0,) * a.ndim)
    return pl.pallas_call(
        kern, grid=(nb, rows // tm),
        in_specs=[pl.BlockSpec((1, tm, D_MODEL), lambda b, j: (b, j, 0)),
                  pl.BlockSpec((1, past, D_MODEL), lambda b, j: (b, 0, 0)),
                  full(gpre), full(w_in), full(conv_w)],
        out_specs=(pl.BlockSpec((1, tm, D_MODEL), lambda b, j: (b, j, 0)),
                   pl.BlockSpec((1, past, D_MODEL), lambda b, j: (b, 0, 0))),
        out_shape=(jax.ShapeDtypeStruct((nb, rows, D_MODEL), BF16),
                   jax.ShapeDtypeStruct((nb, past, D_MODEL), F32)),
        scratch_shapes=[pltpu.VMEM((past, D_MODEL), F32), pltpu.VMEM((past + tm, tc), F32)],
        compiler_params=_cparams("parallel", "arbitrary"), name="odd_in",
    )(x3d, past0, gpre, w_in, conv_w)


def _ffn_kernel(x_ref, ya_ref, yb_ref, wo_ref, gmix_ref, past_ref, gpre_ref, wup_ref, cw_ref,
                cb_ref, wdn_ref, gpost_ref, gk_ref, o_ref, st_ref,
                carry, buf_g, buf_v, acc, h_scr, act, x1_scr, *, tm, past, rs, tf):
    @pl.when(pl.program_id(1) == 0)
    def _():
        carry[...] = past_ref[0]

    mix = _dot(ya_ref[0], wo_ref[0]) + _dot(yb_ref[0], wo_ref[1])
    x1_scr[...] = x_ref[0] + _rms(mix, gmix_ref[...])
    h_scr[...] = _rms(x1_scr[...], gpre_ref[...]).astype(BF16)
    nf = D_FF // tf

    def up(c):
        for lo, buf in ((c * tf, buf_g), (D_FF + c * tf, buf_v)):
            buf[c % 2, 0:past, :] = carry[:, lo:lo + tf]
            buf[c % 2, past:past + tm, :] = _dot(h_scr[...], wup_ref[:, lo:lo + tf])
            carry[:, lo:lo + tf] = buf[c % 2, tm:tm + past, :]

    gelu_c = gk_ref[0:1, 0:1].astype(FFN_ELEM_DTYPE)
    gelu_ca = gk_ref[0:1, 1:2].astype(FFN_ELEM_DTYPE)

    def elem(c):
        halves = []
        for lo, buf, scale in ((c * tf, buf_g, 1.0), (D_FF + c * tf, buf_v, 0.5)):
            b = buf.at[c % 2]
            halves.append(_conv3_from_buf(b, cw_ref[:, lo:lo + tf] * scale, b[past:past + tm, :],
                                          tm, past, rs, FFN_ELEM_DTYPE)
                          + (cb_ref[:, lo:lo + tf] * scale).astype(FFN_ELEM_DTYPE))
        x, half_v = halves
        t = jnp.tanh(x * (x * x * gelu_ca + gelu_c))
        act[c % 2] = ((x * t + x) * half_v).astype(BF16)

    def down(c):
        part = _dot(act[c % 2], wdn_ref[c * tf:(c + 1) * tf, :])
        if c == 0:
            acc[...] = part
        else:
            acc[...] += part

    for c in range(nf + 2):
        if c < nf:
            up(c)
        if 1 <= c <= nf:
            elem(c - 1)
        if c >= 2:
            down(c - 2)
    o_ref[0] = x1_scr[...] + _rms(acc[...], gpost_ref[...])
    st_ref[0] = carry[...]


def _ffn(x3d, ya, yb, w_o, gmix, past0, gpre, w_up, conv_w, conv_b, w_dn, gpost, layer, *, tm, rs):
    nb, rows, _ = x3d.shape
    past = past0.shape[1]
    half = w_o.shape[0] // 2
    kern = functools.partial(_ffn_kernel, tm=tm, past=past, rs=rs, tf=FFN_TF)
    gelu_k = jnp.zeros((1, LANES), F32).at[0, 0].set(GELU_C).at[0, 1].set(GELU_C * GELU_A)
    w_o2 = w_o.reshape(2, half, D_MODEL)
    once = dict(pipeline_mode=pl.Buffered(1))
    full = lambda a: pl.BlockSpec(a.shape, lambda b, j: (0,) * a.ndim, **once)
    per_layer = lambda a: pl.BlockSpec((None,) + a.shape[1:], lambda b, j: (layer, 0, 0), **once)
    tok = lambda w, cblk=0: pl.BlockSpec((1, tm, w), lambda b, j: (b, j, cblk))
    return pl.pallas_call(
        kern, grid=(nb, rows // tm),
        in_specs=[tok(D_MODEL), tok(half, ya[1]), tok(half, yb[1]), full(w_o2), full(gmix),
                  pl.BlockSpec((1, past, F2), lambda b, j: (b, 0, 0)),
                  full(gpre), per_layer(w_up), full(conv_w), full(conv_b), per_layer(w_dn),
                  full(gpost), full(gelu_k)],
        out_specs=(tok(D_MODEL), pl.BlockSpec((1, past, F2), lambda b, j: (b, 0, 0))),
        out_shape=(jax.ShapeDtypeStruct((nb, rows, D_MODEL), F32),
                   jax.ShapeDtypeStruct((nb, past, F2), F32)),
        scratch_shapes=[pltpu.VMEM((past, F2), F32), pltpu.VMEM((2, past + tm, FFN_TF), F32),
                        pltpu.VMEM((2, past + tm, FFN_TF), F32), pltpu.VMEM((tm, D_MODEL), F32),
                        pltpu.VMEM((tm, D_MODEL), BF16), pltpu.VMEM((2, tm, FFN_TF), BF16),
                        pltpu.VMEM((tm, D_MODEL), F32)],
        compiler_params=_cparams("parallel", "arbitrary"), name="conv_ffn",
    )(x3d, ya[0], yb[0], w_o2, gmix, past0, gpre, w_up, conv_w, conv_b, w_dn, gpost, gelu_k)


def _swa_bias_prompt():
    qi = np.arange(WINDOW)[:, None]
    sj = np.arange(2 * WINDOW)[None, :]
    dist = WINDOW + qi - sj
    valid = (dist >= 0) & (dist <= WINDOW)
    slopes = 2.0 ** (-8.0 * np.arange(1, HQ_A + 1) / HQ_A)
    bias = np.where(valid[None], -slopes[:, None, None] * dist[None].astype(np.float64), -np.inf)
    return jnp.asarray(bias, dtype=F32)


def _swa_prompt_kernel(sink_ref, q_ref, kvc_ref, kvp_ref, bias_ref, o_ref, *, tq):
    i = pl.program_id(1)
    kvc = kvc_ref[0]
    kvp = kvp_ref[0]
    kcat = jnp.concatenate([kvp[:, 0:LANES], kvc[:, 0:LANES]], axis=0).astype(BF16)
    vcat = jnp.concatenate([kvp[:, LANES:], kvc[:, LANES:]], axis=0).astype(BF16)
    lane = lax.broadcasted_iota(jnp.int32, (WINDOW, LANES), 1)
    col = lax.broadcasted_iota(jnp.int32, (WINDOW, 2 * WINDOW), 1)
    for j in range(tq // WINDOW):
        keys = kcat[j * WINDOW:(j + 2) * WINDOW, :]
        vals = vcat[j * WINDOW:(j + 2) * WINDOW, :]
        for g in range(G_A):
            q2 = q_ref[0, j * WINDOW:(j + 1) * WINDOW, g * LANES:(g + 1) * LANES]
            outs = []
            for kv in range(HKV_A):
                hq = kv * G_A + g
                in_head = (lane < HD_A) if kv == 0 else (lane >= HD_A)
                qm = jnp.where(in_head, q2, jnp.zeros_like(q2))
                s = _dot_nt(qm, keys) + bias_ref[hq]
                if j == 0:
                    s = jnp.where(jnp.logical_and(i == 0, col < WINDOW), -jnp.inf, s)
                sink = sink_ref[hq]
                m = jnp.maximum(jnp.max(s, axis=-1, keepdims=True), sink)
                p = jnp.exp(s - m)
                den = jnp.sum(p, axis=-1, keepdims=True) + jnp.exp(sink - m)
                outs.append(_dot(p.astype(BF16), vals) / den)
            o2 = jnp.where(lane < HD_A, outs[0], outs[1])
            o_ref[0, j * WINDOW:(j + 1) * WINDOW, g * LANES:(g + 1) * LANES] = o2.astype(BF16)


GLA_LEVELS = (0, 1, 2, 4, 8, 16, 32)
LOG2E = 1.4426950408889634


def _gla_constants():
    r = np.arange(GLA_GROUP)
    i, j = r[:, None], r[None, :]
    ranges = [(i >= j) & (i // GLA_CHUNK == j // GLA_CHUNK)]
    masks = [i == j]
    for s in GLA_LEVELS[1:]:
        mid = (i // (2 * s)) * (2 * s) + s
        upper = (i % (2 * s) >= s) & (j >= mid) & (j <= i)
        lower = (i % (2 * s) < s) & (j > i) & (j < mid)
        ranges.append(upper | lower)
        masks.append((i // (2 * s) == j // (2 * s)) & (i % (2 * s) >= s) & (j % (2 * s) < s))
    ones_bd = (np.arange(QB_W)[:, None] // DK_B) == (np.arange(VB_W)[None, :] // DV_B)
    return (jnp.asarray(np.concatenate(ranges, axis=0), dtype=BF16),
            jnp.asarray(ones_bd, dtype=BF16), jnp.asarray(np.stack(masks), dtype=F32))


def _head_norm_gate(o, r, gn):
    parts = []
    for h in range(H_B):
        oh = o[:, h * DV_B:(h + 1) * DV_B]
        parts.append(oh * lax.rsqrt(jnp.mean(oh * oh, axis=-1, keepdims=True) + EPS))
    return jnp.concatenate(parts, axis=-1) * gn * jax.nn.silu(r)


def _gla_prompt_kernel(q_ref, k_ref, v_ref, g_ref, r_ref, gn_ref, rng_ref, lvl_ref,
                       o_ref, st_ref, state, b_scr, o_scr, qk_scr, *, tm):
    @pl.when(pl.program_id(1) == 0)
    def _():
        state[...] = jnp.zeros_like(state)

    q = q_ref[0]
    k = k_ref[0]
    v = v_ref[0]
    lane = lax.broadcasted_iota(jnp.int32, (1, LANES), 1)
    head_lanes = (lane < DK_B, lane >= DK_B)
    head_cols = lax.broadcasted_iota(jnp.int32, (1, QB_W), 1) // DK_B

    g2 = g_ref[0] * LOG2E
    ghi = g2.astype(BF16)
    glo = (g2 - ghi.astype(F32)).astype(BF16)
    nlev = len(GLA_LEVELS)
    for gi in range(tm // GLA_GROUP):
        rows = slice(gi * GLA_GROUP, (gi + 1) * GLA_GROUP)
        d_all = _dot(rng_ref[...], ghi[rows]) + _dot(rng_ref[...], glo[rows])
        b_scr[rows, :] = d_all[0:GLA_GROUP]
        qk_scr[0, 0, rows, :] = q[rows].astype(BF16)
        qk_scr[0, 1, rows, :] = k[rows].astype(BF16)
        for li in range(1, nlev):
            e = jnp.exp2(d_all[li * GLA_GROUP:(li + 1) * GLA_GROUP])
            qk_scr[li, 0, rows, :] = (q[rows] * e).astype(BF16)
            qk_scr[li, 1, rows, :] = (k[rows] * e).astype(BF16)
    b = b_scr[...]

    for gi in range(tm // GLA_GROUP):
        rows = slice(gi * GLA_GROUP, (gi + 1) * GLA_GROUP)
        for pi in range(H_B // 2):
            pr = slice(pi * LANES, (pi + 1) * LANES)
            a = None
            for li in range(nlev):
                ke = qk_scr[li, 1, rows, pr]
                kem = jnp.concatenate([jnp.where(sel, ke, jnp.zeros_like(ke)) for sel in head_lanes],
                                      axis=0)
                term = _dot_nt(qk_scr[li, 0, rows, pr], kem)
                keep = lvl_ref[li] > 0.5
                keep = jnp.concatenate([keep, keep], axis=1)
                a = jnp.where(keep, term, 0.0 if a is None else a)
            a = a.astype(BF16)
            for par in range(2):
                cols = slice((2 * pi + par) * DV_B, (2 * pi + par + 1) * DV_B)
                o_scr[rows, cols] = _dot(a[:, par * GLA_GROUP:(par + 1) * GLA_GROUP], v[rows, cols])

    for c in range(tm // GLA_CHUNK):
        rows = slice(c * GLA_CHUNK, (c + 1) * GLA_CHUNK)
        bc = b[rows]
        b_end = bc[GLA_CHUNK - 1:GLA_CHUNK, :]
        qe = (q[rows] * jnp.exp2(bc)).astype(BF16)
        ke = (k[rows] * jnp.exp2(b_end - bc)).astype(BF16)
        st = state[...]
        qm = jnp.concatenate([jnp.where(head_cols == h, qe, jnp.zeros_like(qe))
                              for h in range(H_B)], axis=0)
        km = jnp.concatenate([jnp.where(head_cols == h, ke, jnp.zeros_like(ke))
                              for h in range(H_B)], axis=0)
        vm = jnp.concatenate([v[rows, h * DV_B:(h + 1) * DV_B] for h in range(H_B)], axis=0)
        oi = _dot_nt(qm, st.astype(BF16))
        for h in range(H_B):
            o_scr[rows, h * DV_B:(h + 1) * DV_B] += oi[h * GLA_CHUNK:(h + 1) * GLA_CHUNK]
        state[...] = st * jnp.exp2(b_end) + _dot_tn(vm, km)

    o_ref[0] = _head_norm_gate(o_scr[...], r_ref[0], gn_ref[...]).astype(BF16)
    st_ref[0] = state[...]


N_ATTN_IN = 13


def _attn_prompt_kernel(*refs, tm, n_cast):
    (sink_ref, qa_ref, kvc_ref, kvp_ref, bias_ref,
     q_ref, k_ref, v_ref, g_ref, r_ref, gn_ref, rng_ref, lvl_ref) = refs[:N_ATTN_IN]
    cast_in = refs[N_ATTN_IN:N_ATTN_IN + n_cast]
    oa_ref, ob_ref, st_ref = refs[N_ATTN_IN + n_cast:N_ATTN_IN + n_cast + 3]
    cast_out = refs[N_ATTN_IN + n_cast + 3:N_ATTN_IN + 2 * n_cast + 3]
    state, b_scr, o_scr, qk_scr = refs[N_ATTN_IN + 2 * n_cast + 3:]
    _gla_prompt_kernel(q_ref, k_ref, v_ref, g_ref, r_ref, gn_ref, rng_ref, lvl_ref,
                       ob_ref, st_ref, state, b_scr, o_scr, qk_scr, tm=tm)
    _swa_prompt_kernel(sink_ref, qa_ref, kvc_ref, kvp_ref, bias_ref, oa_ref, tq=tm)
    for src, dst in zip(cast_in, cast_out):
        dst[...] = src[...].astype(BF16)


def _attn_prompt(qa3d, kv3d, sinks, q3d, k3d, v3d, g3d, r3d, gn, to_bf16):
    nb, S, _ = q3d.shape
    tm = min(TM_ATTN, S)
    nj = S // tm
    nsteps = nb * nj
    bias = _swa_bias_prompt()
    rng, _, lvl = _gla_constants()
    kern = functools.partial(_attn_prompt_kernel, tm=tm, n_cast=len(to_bf16))
    blocks_per_tile = tm // WINDOW
    tok = lambda w: pl.BlockSpec((1, tm, w), lambda b, j: (b, j, 0))
    full = lambda a: pl.BlockSpec(a.shape, lambda b, j: (0,) * a.ndim)
    prev_kv = pl.BlockSpec((1, WINDOW, KV_W),
                           lambda b, j: (b, jnp.maximum(j * blocks_per_tile - 1, 0), 0))

    def slab(a):
        rows = a.shape[0] // nsteps
        assert rows * nsteps == a.shape[0] and rows % (2 * SUBLANES) == 0, a.shape
        return pl.BlockSpec((rows, a.shape[1]), lambda b, j: (b * nj + j, 0))

    outs = pl.pallas_call(
        kern, grid=(nb, nj),
        in_specs=[pl.BlockSpec(memory_space=pltpu.SMEM), tok(QA_W), tok(KV_W), prev_kv, full(bias),
                  tok(QB_W), tok(QB_W), tok(VB_W), tok(QB_W), tok(VB_W),
                  full(gn), full(rng), full(lvl)] + [slab(a) for a in to_bf16],
        out_specs=[tok(QA_W), tok(VB_W), pl.BlockSpec((1, DV_B, QB_W), lambda b, j: (b, 0, 0))]
        + [slab(a) for a in to_bf16],
        out_shape=[jax.ShapeDtypeStruct((nb, S, QA_W), BF16),
                   jax.ShapeDtypeStruct((nb, S, VB_W), BF16),
                   jax.ShapeDtypeStruct((nb, DV_B, QB_W), F32)]
        + [jax.ShapeDtypeStruct(a.shape, BF16) for a in to_bf16],
        scratch_shapes=[pltpu.VMEM((DV_B, QB_W), F32), pltpu.VMEM((tm, QB_W), F32),
                        pltpu.VMEM((tm, VB_W), F32),
                        pltpu.VMEM((len(GLA_LEVELS), 2, tm, QB_W), BF16)],
        compiler_params=_cparams("parallel", "arbitrary"), name="attn_prompt",
    )(sinks, qa3d, kv3d, kv3d, bias, q3d, k3d, v3d, g3d, r3d, gn, rng, lvl, *to_bf16)
    return outs[0], outs[1], outs[2], outs[3:]


def _swa_bias_decode(L):
    rows = HQ_A * L
    kpad = DEC_KEYS
    hq = np.arange(rows)[:, None] // L
    t = np.arange(rows)[:, None] % L
    s = np.arange(kpad)[None, :]
    dist = WINDOW + t - s
    valid = (dist >= 0) & (dist <= WINDOW) & (s < WINDOW + L)
    slopes = 2.0 ** (-8.0 * (hq + 1) / HQ_A)
    return jnp.asarray(np.where(valid, -slopes * dist, -np.inf), dtype=F32)


def _swa_decode_kernel(q_ref, kvn_ref, ck_ref, cv_ref, bias_ref, sink_ref,
                       o_ref, nk_ref, nv_ref, qs, os_, kc, vc, kn, vn, *, ns, L, sb):
    i = pl.program_id(0)
    rows_q = HQ_A * L
    lane = lax.broadcasted_iota(jnp.int32, (1, LANES), 1)

    @pl.when(i == 0)
    def _():
        q = q_ref[...].astype(F32)
        for kv in range(HKV_A):
            sel = (lane < HD_A) if kv == 0 else (lane >= HD_A)
            for g in range(G_A):
                base = (kv * G_A + g) * L * ns
                qs[base:base + L * ns, :] = jnp.where(sel, q[:, g * LANES:(g + 1) * LANES], 0.0)
        kc[...] = jnp.zeros_like(kc)
        vc[...] = jnp.zeros_like(vc)
        kn[...] = kvn_ref[:, 0:LANES]
        vn[...] = kvn_ref[:, LANES:]

    bias = bias_ref[...]
    sink = sink_ref[...]

    def body(t, carry):
        lhs = []
        for u in range(DEC_UNROLL):
            s = t * DEC_UNROLL + u
            seq = i * sb + s
            kc[u, 0:WINDOW, :] = ck_ref[s]
            vc[u, 0:WINDOW, :] = cv_ref[s]
            kc[u, WINDOW:WINDOW + L, :] = kn[pl.ds(seq, L, stride=ns), :]
            vc[u, WINDOW:WINDOW + L, :] = vn[pl.ds(seq, L, stride=ns), :]
            nk_ref[s] = kc[u, L:L + WINDOW, :]
            nv_ref[s] = vc[u, L:L + WINDOW, :]
            lhs.append(qs[pl.ds(seq, rows_q, stride=ns), :].astype(BF16))
        sc = jnp.einsum("uqd,ukd->uqk", jnp.stack(lhs), kc[...].astype(BF16),
                        preferred_element_type=F32) + bias
        m = jnp.maximum(jnp.max(sc, axis=-1, keepdims=True), sink)
        p = jnp.exp(sc - m)
        den = jnp.sum(p, axis=-1, keepdims=True) + jnp.exp(sink - m)
        res = jnp.einsum("uqk,ukd->uqd", p.astype(BF16), vc[...].astype(BF16),
                         preferred_element_type=F32) / den
        half = rows_q // 2
        for u in range(DEC_UNROLL):
            seq = i * sb + t * DEC_UNROLL + u
            os_[pl.ds(seq, half, stride=ns), :] = jnp.where(lane < HD_A, res[u, 0:half],
                                                            res[u, half:])
        return carry

    lax.fori_loop(0, sb // DEC_UNROLL, body, 0)

    @pl.when(i == pl.num_programs(0) - 1)
    def _():
        for g in range(G_A):
            o_ref[:, g * LANES:(g + 1) * LANES] = os_[g * L * ns:(g + 1) * L * ns, :].astype(BF16)


def _swa_decode(q_tm, kvn_tm, cache_k, cache_v, sinks, *, ns, L):
    sb = min(SEQ_BLOCK, ns)
    bias = _swa_bias_decode(L)
    sink_col = jnp.broadcast_to(jnp.repeat(sinks.astype(F32), L)[:, None], (HQ_A * L, 1))
    kern = functools.partial(_swa_decode_kernel, ns=ns, L=L, sb=sb)
    full = lambda a: pl.BlockSpec(a.shape, lambda i: (0,) * a.ndim)
    cache = pl.BlockSpec((sb, WINDOW, LANES), lambda i: (i, 0, 0))
    kpad = DEC_KEYS
    return pl.pallas_call(
        kern, grid=(ns // sb,),
        in_specs=[full(q_tm), full(kvn_tm), cache, cache, full(bias), full(sink_col)],
        out_specs=(pl.BlockSpec((L * ns, QA_W), lambda i: (0, 0)), cache, cache),
        out_shape=(jax.ShapeDtypeStruct((L * ns, QA_W), BF16),
                   jax.ShapeDtypeStruct(cache_k.shape, F32),
                   jax.ShapeDtypeStruct(cache_v.shape, F32)),
        scratch_shapes=[pltpu.VMEM((HQ_A * L * ns, LANES), F32),
                        pltpu.VMEM((G_A * L * ns, LANES), F32),
                        pltpu.VMEM((DEC_UNROLL, kpad, LANES), F32),
                        pltpu.VMEM((DEC_UNROLL, kpad, LANES), F32),
                        pltpu.VMEM((L * ns, LANES), F32), pltpu.VMEM((L * ns, LANES), F32)],
        compiler_params=_cparams("arbitrary"), name="swa_decode",
    )(q_tm, kvn_tm, cache_k, cache_v, bias, sink_col)


def _gla_decode_kernel(q_ref, k_ref, v_ref, g_ref, r_ref, gn_ref, ones_ref, s0_ref,
                       o_ref, s1_ref, qe2, ke2, v2, dec3, oi2, od, *, ns, L, sb):
    i = pl.program_id(0)
    lane = lax.broadcasted_iota(jnp.int32, (1, LANES), 1)
    npair = H_B // 2

    @pl.when(i == 0)
    def _():
        slab = lambda a, t: a[t * ns:(t + 1) * ns, :]
        q, k, g = q_ref[...], k_ref[...], g_ref[...]
        vf = v_ref[...].astype(F32)
        b = [slab(g, 0)]
        for t in range(1, L):
            b.append(b[-1] + slab(g, t))
        for t in range(L):
            acc = jnp.zeros((ns, VB_W), F32)
            for jj in range(t + 1):
                p = (slab(q, t) * slab(k, jj) * jnp.exp(b[t] - b[jj])).astype(BF16)
                acc = acc + _dot(p, ones_ref[...]) * slab(vf, jj)
            od[t * ns:(t + 1) * ns, :] = acc
            qe = slab(q, t) * jnp.exp(b[t])
            ke = slab(k, t) * jnp.exp(b[L - 1] - b[t])
            for par in range(2):
                sel = (lane < DK_B) if par == 0 else (lane >= DK_B)
                base = (par * L + t) * ns
                for pi in range(npair):
                    pr = slice(pi * LANES, (pi + 1) * LANES)
                    qe2[pi, base:base + ns, :] = jnp.where(sel, qe[:, pr], 0.0)
                    ke2[pi, base:base + ns, :] = jnp.where(sel, ke[:, pr], 0.0)
                    h = 2 * pi + par
                    v2[pi, base:base + ns, :] = slab(vf, t)[:, h * DV_B:(h + 1) * DV_B]
        dec3[...] = jnp.zeros_like(dec3)
        hi, mid, lo = _split3(jnp.exp(b[L - 1]))
        for pi in range(npair):
            pr = slice(pi * LANES, (pi + 1) * LANES)
            dec3[pi, 0:ns, :] = hi[:, pr].astype(F32)
            dec3[pi, ns:2 * ns, :] = mid[:, pr].astype(F32)
            dec3[pi, 2 * ns:3 * ns, :] = lo[:, pr].astype(F32)

    ones8 = jnp.ones((2 * L, LANES), BF16)

    def one_seq(s):
        seq = i * sb + s
        for pi in range(npair):
            take = lambda ref: ref[pi, pl.ds(seq, 2 * L, stride=ns), :].astype(BF16)
            st_p = s0_ref[s, pi * LANES:(pi + 1) * LANES, :]
            oi2[pi, pl.ds(seq, 2 * L, stride=ns), :] = _dot(take(qe2), st_p.astype(BF16))
            dcol = _dot_tn(take(dec3), ones8)
            upd = _dot_tn(take(ke2), take(v2))
            s1_ref[s, pi * LANES:(pi + 1) * LANES, :] = dcol * st_p + upd

    def body(t, carry):
        for u in range(DEC_UNROLL):
            one_seq(t * DEC_UNROLL + u)
        return carry

    lax.fori_loop(0, sb // DEC_UNROLL, body, 0)

    @pl.when(i == pl.num_programs(0) - 1)
    def _():
        for t in range(L):
            parts = []
            for h in range(H_B):
                base = ((h % 2) * L + t) * ns
                parts.append(oi2[h // 2, base:base + ns, :])
            o = jnp.concatenate(parts, axis=-1) + od[t * ns:(t + 1) * ns, :]
            o_ref[t * ns:(t + 1) * ns, :] = _head_norm_gate(
                o, r_ref[t * ns:(t + 1) * ns, :], gn_ref[...]).astype(BF16)


def _gla_decode(q_tm, k_tm, v_tm, g_tm, r_tm, gn, s0, *, ns, L):
    sb = min(SEQ_BLOCK, ns)
    assert 2 * L == SUBLANES
    _, ones_bd, _ = _gla_constants()
    kern = functools.partial(_gla_decode_kernel, ns=ns, L=L, sb=sb)
    full = lambda a: pl.BlockSpec(a.shape, lambda i: (0,) * a.ndim)
    st = pl.BlockSpec((sb, QB_W, DV_B), lambda i: (i, 0, 0))
    rows2 = 2 * L * ns
    return pl.pallas_call(
        kern, grid=(ns // sb,),
        in_specs=[full(q_tm), full(k_tm), full(v_tm), full(g_tm), full(r_tm), full(gn),
                  full(ones_bd), st],
        out_specs=(pl.BlockSpec((L * ns, VB_W), lambda i: (0, 0)), st),
        out_shape=(jax.ShapeDtypeStruct((L * ns, VB_W), BF16),
                   jax.ShapeDtypeStruct(s0.shape, F32)),
        scratch_shapes=[pltpu.VMEM((H_B // 2, rows2, LANES), F32) for _ in range(5)]
        + [pltpu.VMEM((L * ns, VB_W), F32)],
        compiler_params=_cparams("arbitrary"), name="gla_decode",
    )(q_tm, k_tm, v_tm, g_tm, r_tm, gn, ones_bd, s0)


def _qa_perm():
    return np.asarray([(kv * G_A + g) * HD_A + d
                       for g in range(G_A) for kv in range(HKV_A) for d in range(HD_A)])


def _prep_even(w_in, w_gate_up, b_gate, w_out):
    perm = _qa_perm()
    w_main = jnp.concatenate([w_in[:, :QA_W][:, perm], w_in[:, QA_W:MAIN_W]], axis=1).astype(BF16)
    w_g = jnp.pad(w_in[:, MAIN_W:], ((0, 0), (0, LANES - GATE_RANK))).astype(BF16)
    w_gu = jnp.pad(w_gate_up, ((0, LANES - GATE_RANK), (0, 0))).astype(BF16)
    w_o = jnp.concatenate([w_out[:QA_W][perm], w_out[QA_W:]], axis=0).astype(BF16)
    return w_main, w_g, w_gu, b_gate.reshape(1, -1), w_o


def _row(v):
    return v.reshape(1, -1)


def kernel(x_prompt, x_sample, cache_swa_k, cache_swa_v, state_gla, state_conv, state_ffn,
           norm_mix_pre, norm_mix_post, norm_ffn_pre, norm_ffn_post, w_in_even, w_gate_up, b_gate,
           attn_sinks, gla_norm, w_out_even, w_in_odd, conv_w_odd, w_out_odd, ffn_up, ffn_conv_w,
           ffn_conv_b, ffn_down):
    nb, S, _ = x_prompt.shape
    ns, L, _ = x_sample.shape
    depth = norm_mix_pre.shape[0]

    xp = x_prompt
    xs = x_sample.transpose(1, 0, 2).reshape(1, L * ns, D_MODEL)
    past_p = max(SUBLANES, 2)
    past_s = 2 * ns
    tm_p = min(TM_TOK, S)

    ks_p, vs_p, gs_p, cs_p, fs_p = [], [], [], [], []
    ks_s, vs_s, gs_s, cs_s, fs_s = [], [], [], [], []

    for l in range(depth):
        gpre, gpost = _row(norm_mix_pre[l]), _row(norm_mix_post[l])
        if l % 2 == 0:
            e = l // 2
            w_main, w_g, w_gu, b_g, w_o = _prep_even(w_in_even[e], w_gate_up[e], b_gate[e],
                                                     w_out_even[e])
            gn = _row(gla_norm[e])
            qa, kv, qb, kb, vb, rb, gb = _inproj_even(xp.reshape(nb * S, D_MODEL), gpre,
                                                      w_main, w_g, w_gu, b_g)
            r3 = lambda a: a.reshape(nb, S, a.shape[-1])
            to_bf16 = []
            if e == 0:
                to_bf16 = [ffn_up.reshape(-1, F2), ffn_down.reshape(-1, D_MODEL)]
                if w_in_odd.shape[0]:
                    to_bf16 += [w_in_odd.reshape(-1, 3 * D_MODEL), w_out_odd.reshape(-1, D_MODEL)]
            oa, ob, st_t, narrowed = _attn_prompt(r3(qa), r3(kv), attn_sinks[e], r3(qb), r3(kb),
                                                  r3(vb), r3(gb), r3(rb), gn, to_bf16)
            if e == 0:
                w_up_all = narrowed[0].reshape(ffn_up.shape)
                w_dn_all = narrowed[1].reshape(ffn_down.shape)
                if w_in_odd.shape[0]:
                    w_in_odd_bf = narrowed[2].reshape(w_in_odd.shape)
                    w_out_odd_bf = narrowed[3].reshape(w_out_odd.shape)
            mix_p = ((oa, 0), (ob, 0))
            kv_last = r3(kv)[:, S - WINDOW:, :]
            ks_p.append(kv_last[..., :LANES].reshape(nb, WINDOW, HKV_A, HD_A))
            vs_p.append(kv_last[..., LANES:].reshape(nb, WINDOW, HKV_A, HD_A))
            gs_p.append(st_t.transpose(0, 2, 1).reshape(nb, H_B, DK_B, DV_B))
            qa, kv, qb, kb, vb, rb, gb = _inproj_even(xs.reshape(L * ns, D_MODEL), gpre,
                                                      w_main, w_g, w_gu, b_g)
            oa, nk, nv = _swa_decode(qa, kv, cache_swa_k[e].reshape(ns, WINDOW, LANES),
                                     cache_swa_v[e].reshape(ns, WINDOW, LANES), attn_sinks[e],
                                     ns=ns, L=L)
            ob, s1 = _gla_decode(qb, kb, vb, gb, rb, gn, state_gla[e].reshape(ns, QB_W, DV_B),
                                 ns=ns, L=L)
            mix_s = ((oa.reshape(1, L * ns, QA_W), 0), (ob.reshape(1, L * ns, VB_W), 0))
            ks_s.append(nk.reshape(ns, WINDOW, HKV_A, HD_A))
            vs_s.append(nv.reshape(ns, WINDOW, HKV_A, HD_A))
            gs_s.append(s1.reshape(ns, H_B, DK_B, DV_B))
        else:
            o = l // 2
            w_in = w_in_odd_bf[o]
            w_o = w_out_odd_bf[o]
            y, st = _odd_in(xp, jnp.zeros((nb, past_p, D_MODEL), F32), gpre, w_in, conv_w_odd[o],
                            tm=min(TM_ODD, S), rs=1)
            mix_p = ((y, 0), (y, 1))
            cs_p.append(st[:, past_p - 2:, :])
            past = state_conv[o].transpose(1, 0, 2).reshape(1, past_s, D_MODEL)
            y, st = _odd_in(xs, past, gpre, w_in, conv_w_odd[o], tm=L * ns, rs=ns)
            mix_s = ((y, 0), (y, 1))
            cs_s.append(st.reshape(2, ns, D_MODEL).transpose(1, 0, 2))

        gmix = gpost
        gpre, gpost = _row(norm_ffn_pre[l]), _row(norm_ffn_post[l])
        cb = _row(ffn_conv_b[l])
        xp, st = _ffn(xp, mix_p[0], mix_p[1], w_o, gmix, jnp.zeros((nb, past_p, F2), F32), gpre,
                      w_up_all, ffn_conv_w[l], cb, w_dn_all, gpost, l, tm=tm_p, rs=1)
        fs_p.append(st[:, past_p - 2:, :])
        past = state_ffn[l].transpose(1, 0, 2).reshape(1, past_s, F2)
        xs, st = _ffn(xs, mix_s[0], mix_s[1], w_o, gmix, past, gpre, w_up_all, ffn_conv_w[l], cb,
                      w_dn_all, gpost, l, tm=L * ns, rs=ns)
        fs_s.append(st.reshape(2, ns, F2).transpose(1, 0, 2))

    y_sample = xs.reshape(L, ns, D_MODEL).transpose(1, 0, 2)
    return (xp, y_sample, jnp.stack(ks_p), jnp.stack(vs_p), jnp.stack(gs_p), jnp.stack(cs_p),
            jnp.stack(fs_p), jnp.stack(ks_s), jnp.stack(vs_s), jnp.stack(gs_s), jnp.stack(cs_s),
            jnp.stack(fs_s))
```

```python
import functools

import numpy as np
import jax
import jax.numpy as jnp
from jax import lax
from jax.experimental import pallas as pl
from jax.experimental.pallas import tpu as pltpu

F32 = jnp.float32
BF16 = jnp.bfloat16

D_MODEL = 1024
WINDOW = 128
HD_A = 64
HQ_A = 8
HKV_A = 2
G_A = HQ_A // HKV_A
H_B = 4
DK_B = 64
DV_B = 128
GATE_RANK = 16
GATE_TAU = 16.0
GLA_CHUNK = 64
D_FF = ((8 * D_MODEL // 3 + 127) // 128) * 128
F2 = 2 * D_FF
EPS = 1e-6
GELU_C = float(np.sqrt(2.0 / np.pi))
GELU_A = 0.044715

QA_W = HQ_A * HD_A
KV_W = 2 * HKV_A * HD_A
QB_W = H_B * DK_B
VB_W = H_B * DV_B
MAIN_W = QA_W + KV_W + 2 * QB_W + 2 * VB_W

LANES = 128
SUBLANES = 8
VMEM_LIMIT = 56 * 1024 * 1024

TM_TOK = 512
TM_INPROJ = 1024
TM_ODD = 1024
TM_ATTN = 1024
GLA_GROUP = 128
ODD_TC = 256
FFN_TF = 256
FFN_ELEM_DTYPE = BF16
SEQ_BLOCK = 32
DEC_UNROLL = 8
DEC_KEYS = 2 * WINDOW


def _cparams(*sem):
    return pltpu.CompilerParams(dimension_semantics=sem, vmem_limit_bytes=VMEM_LIMIT)


def _rms(x, g):
    return x * lax.rsqrt(jnp.mean(x * x, axis=-1, keepdims=True) + EPS) * g


def _dot(a, b):
    return jnp.dot(a, b, preferred_element_type=F32)


def _dot_nt(a, b):
    return lax.dot_general(a, b, (((1,), (1,)), ((), ())), preferred_element_type=F32)


def _dot_tn(a, b):
    return lax.dot_general(a, b, (((0,), (0,)), ((), ())), preferred_element_type=F32)


def _split3(x):
    hi = x.astype(BF16)
    r1 = x - hi.astype(F32)
    mid = r1.astype(BF16)
    lo = (r1 - mid.astype(F32)).astype(BF16)
    return hi, mid, lo


def _inproj_even_kernel(x_ref, gpre_ref, w_ref, wg_ref, wgu_ref, bg_ref,
                        qa_ref, kv_ref, qb_ref, kb_ref, vb_ref, rb_ref, gb_ref):
    h = _rms(x_ref[...], gpre_ref[...]).astype(BF16)

    def mm(lo, width):
        return _dot(h, w_ref[:, lo:lo + width])

    lo = 0
    qa_ref[...] = (mm(lo, QA_W) * (HD_A ** -0.5)).astype(BF16)
    lo += QA_W
    kv_ref[...] = mm(lo, KV_W)
    lo += KV_W
    qb_ref[...] = mm(lo, QB_W) * (DK_B ** -0.5)
    lo += QB_W
    kb_ref[...] = mm(lo, QB_W)
    lo += QB_W
    vb_ref[...] = mm(lo, VB_W).astype(BF16)
    lo += VB_W
    rb_ref[...] = mm(lo, VB_W)
    glr = _dot(h, wg_ref[...]).astype(BF16)
    z = _dot(glr, wgu_ref[...]) + bg_ref[...]
    gb_ref[...] = jax.nn.log_sigmoid(z) * (1.0 / GATE_TAU)


def _inproj_even(x2d, gpre, w_main, w_g, w_gu, b_g):
    T = x2d.shape[0]
    tm = min(TM_INPROJ, T)
    row = lambda w: pl.BlockSpec((tm, w), lambda i: (i, 0))
    full = lambda a: pl.BlockSpec(a.shape, lambda i: (0,) * a.ndim)
    out_shape = (
        jax.ShapeDtypeStruct((T, QA_W), BF16), jax.ShapeDtypeStruct((T, KV_W), F32),
        jax.ShapeDtypeStruct((T, QB_W), F32), jax.ShapeDtypeStruct((T, QB_W), F32),
        jax.ShapeDtypeStruct((T, VB_W), BF16), jax.ShapeDtypeStruct((T, VB_W), F32),
        jax.ShapeDtypeStruct((T, QB_W), F32))
    return pl.pallas_call(
        _inproj_even_kernel, grid=(T // tm,),
        in_specs=[row(D_MODEL), full(gpre), full(w_main), full(w_g), full(w_gu), full(b_g)],
        out_specs=(row(QA_W), row(KV_W), row(QB_W), row(QB_W), row(VB_W), row(VB_W), row(QB_W)),
        out_shape=out_shape, compiler_params=_cparams("parallel"), name="inproj_even",
    )(x2d, gpre, w_main, w_g, w_gu, b_g)


def _conv3_from_buf(buf, cw, cur, tm, past, rs, dtype=F32):
    cw = cw.astype(dtype)
    y = cw[0:1, :] * buf[past - 2 * rs:past - 2 * rs + tm, :].astype(dtype)
    y = y + cw[1:2, :] * buf[past - rs:past - rs + tm, :].astype(dtype)
    return y + cw[2:3, :] * cur.astype(dtype)


def _load_history(carry, past_ref, rs, seq_major):
    if seq_major:
        carry[0:rs, :] = past_ref[:, 0, :]
        carry[rs:2 * rs, :] = past_ref[:, 1, :]
    else:
        carry[...] = past_ref[0]


def _store_history(st_ref, carry, rs, seq_major):
    if seq_major:
        st_ref[:, 0, :] = carry[0:rs, :]
        st_ref[:, 1, :] = carry[rs:2 * rs, :]
    else:
        st_ref[0] = carry[...]


def _history_specs(past0, index, channels, seq_major):
    if seq_major:
        ns = past0.shape[1]
        return (2 * ns,
                pl.BlockSpec((None, ns, 2, channels), lambda b, j: (index, 0, 0, 0)),
                pl.BlockSpec((ns, 2, channels), lambda b, j: (0, 0, 0)),
                jax.ShapeDtypeStruct((ns, 2, channels), F32))
    past = past0.shape[1]
    spec = pl.BlockSpec((1, past, channels), lambda b, j: (b, 0, 0))
    return past, spec, spec, jax.ShapeDtypeStruct((past0.shape[0], past, channels), F32)


def _odd_in_kernel(x_ref, past_ref, gpre_ref, win_ref, cw_ref, y_ref, st_ref, carry, buf,
                   *, tm, past, rs, tc, seq_major):
    @pl.when(pl.program_id(1) == 0)
    def _():
        _load_history(carry, past_ref, rs, seq_major)

    h = _rms(x_ref[0], gpre_ref[...]).astype(BF16)
    for c in range(D_MODEL // tc):
        lo = c * tc
        bg = _dot(h, win_ref[:, lo:lo + tc])
        cu = _dot(h, win_ref[:, D_MODEL + lo:D_MODEL + lo + tc]) * \
            _dot(h, win_ref[:, 2 * D_MODEL + lo:2 * D_MODEL + lo + tc])
        buf[0:past, :] = carry[:, lo:lo + tc]
        buf[past:past + tm, :] = cu
        carry[:, lo:lo + tc] = buf[tm:tm + past, :]
        z = _conv3_from_buf(buf, cw_ref[:, lo:lo + tc], cu, tm, past, rs)
        y_ref[0, :, lo:lo + tc] = (bg * z).astype(BF16)
    _store_history(st_ref, carry, rs, seq_major)


def _odd_in(x3d, past0, gpre, w_in, conv_w, *, tm, rs, state_index=None):
    nb, rows, _ = x3d.shape
    seq_major = state_index is not None
    past, past_spec, st_spec, st_shape = _history_specs(past0, state_index, D_MODEL, seq_major)
    tc = ODD_TC
    kern = functools.partial(_odd_in_kernel, tm=tm, past=past, rs=rs, tc=tc, seq_major=seq_major)
    full = lambda a: pl.BlockSpec(a.shape, lambda b, j: (0,) * a.ndim)
    return pl.pallas_call(
        kern, grid=(nb, rows // tm),
        in_specs=[pl.BlockSpec((1, tm, D_MODEL), lambda b, j: (b, j, 0)), past_spec,
                  full(gpre), full(w_in), full(conv_w)],
        out_specs=(pl.BlockSpec((1, tm, D_MODEL), lambda b, j: (b, j, 0)), st_spec),
        out_shape=(jax.ShapeDtypeStruct((nb, rows, D_MODEL), BF16), st_shape),
        scratch_shapes=[pltpu.VMEM((past, D_MODEL), F32), pltpu.VMEM((past + tm, tc), F32)],
        compiler_params=_cparams("parallel", "arbitrary"), name="odd_in",
    )(x3d, past0, gpre, w_in, conv_w)


def _ffn_kernel(x_ref, ya_ref, yb_ref, wo_ref, gmix_ref, past_ref, gpre_ref, wup_ref, cw_ref,
                cb_ref, wdn_ref, gpost_ref, gk_ref, o_ref, st_ref,
                carry, buf_g, buf_v, acc, h_scr, act, x1_scr, *, tm, past, rs, tf, seq_major):
    @pl.when(pl.program_id(1) == 0)
    def _():
        _load_history(carry, past_ref, rs, seq_major)

    mix = _dot(ya_ref[0], wo_ref[0]) + _dot(yb_ref[0], wo_ref[1])
    x1_scr[...] = x_ref[0] + _rms(mix, gmix_ref[...])
    h_scr[...] = _rms(x1_scr[...], gpre_ref[...]).astype(BF16)
    nf = D_FF // tf

    def up(c):
        for lo, buf in ((c * tf, buf_g), (D_FF + c * tf, buf_v)):
            buf[c % 2, 0:past, :] = carry[:, lo:lo + tf]
            buf[c % 2, past:past + tm, :] = _dot(h_scr[...], wup_ref[:, lo:lo + tf])
            carry[:, lo:lo + tf] = buf[c % 2, tm:tm + past, :]

    gelu_c = gk_ref[0:1, 0:1].astype(FFN_ELEM_DTYPE)
    gelu_ca = gk_ref[0:1, 1:2].astype(FFN_ELEM_DTYPE)

    def elem(c):
        halves = []
        for lo, buf, scale in ((c * tf, buf_g, 1.0), (D_FF + c * tf, buf_v, 0.5)):
            b = buf.at[c % 2]
            halves.append(_conv3_from_buf(b, cw_ref[:, lo:lo + tf] * scale, b[past:past + tm, :],
                                          tm, past, rs, FFN_ELEM_DTYPE)
                          + (cb_ref[:, lo:lo + tf] * scale).astype(FFN_ELEM_DTYPE))
        x, half_v = halves
        t = jnp.tanh(x * (x * x * gelu_ca + gelu_c))
        act[c % 2] = ((x * t + x) * half_v).astype(BF16)

    def down(c):
        part = _dot(act[c % 2], wdn_ref[c * tf:(c + 1) * tf, :])
        if c == 0:
            acc[...] = part
        else:
            acc[...] += part

    for c in range(nf + 2):
        if c < nf:
            up(c)
        if 1 <= c <= nf:
            elem(c - 1)
        if c >= 2:
            down(c - 2)
    o_ref[0] = x1_scr[...] + _rms(acc[...], gpost_ref[...])
    _store_history(st_ref, carry, rs, seq_major)


def _ffn(x3d, ya, yb, w_o, gmix, past0, gpre, w_up, conv_w, conv_b, w_dn, gpost, layer, *, tm, rs,
         seq_major=False):
    nb, rows, _ = x3d.shape
    past, past_spec, st_spec, st_shape = _history_specs(past0, layer, F2, seq_major)
    half = w_o.shape[0] // 2
    kern = functools.partial(_ffn_kernel, tm=tm, past=past, rs=rs, tf=FFN_TF, seq_major=seq_major)
    gelu_k = jnp.zeros((1, LANES), F32).at[0, 0].set(GELU_C).at[0, 1].set(GELU_C * GELU_A)
    w_o2 = w_o.reshape(2, half, D_MODEL)
    once = dict(pipeline_mode=pl.Buffered(1))
    full = lambda a: pl.BlockSpec(a.shape, lambda b, j: (0,) * a.ndim, **once)
    per_layer = lambda a: pl.BlockSpec((None,) + a.shape[1:], lambda b, j: (layer, 0, 0), **once)
    tok = lambda w, cblk=0: pl.BlockSpec((1, tm, w), lambda b, j: (b, j, cblk))
    return pl.pallas_call(
        kern, grid=(nb, rows // tm),
        in_specs=[tok(D_MODEL), tok(half, ya[1]), tok(half, yb[1]), full(w_o2), full(gmix),
                  past_spec,
                  full(gpre), per_layer(w_up), full(conv_w), full(conv_b), per_layer(w_dn),
                  full(gpost), full(gelu_k)],
        out_specs=(tok(D_MODEL), st_spec),
        out_shape=(jax.ShapeDtypeStruct((nb, rows, D_MODEL), F32), st_shape),
        scratch_shapes=[pltpu.VMEM((past, F2), F32), pltpu.VMEM((2, past + tm, FFN_TF), F32),
                        pltpu.VMEM((2, past + tm, FFN_TF), F32), pltpu.VMEM((tm, D_MODEL), F32),
                        pltpu.VMEM((tm, D_MODEL), BF16), pltpu.VMEM((2, tm, FFN_TF), BF16),
                        pltpu.VMEM((tm, D_MODEL), F32)],
        compiler_params=_cparams("parallel", "arbitrary"), name="conv_ffn",
    )(x3d, ya[0], yb[0], w_o2, gmix, past0, gpre, w_up, conv_w, conv_b, w_dn, gpost, gelu_k)


def _swa_bias_prompt():
    qi = np.arange(WINDOW)[:, None]
    sj = np.arange(2 * WINDOW)[None, :]
    dist = WINDOW + qi - sj
    valid = (dist >= 0) & (dist <= WINDOW)
    slopes = 2.0 ** (-8.0 * np.arange(1, HQ_A + 1) / HQ_A)
    bias = np.where(valid[None], -slopes[:, None, None] * dist[None].astype(np.float64), -np.inf)
    return jnp.asarray(bias, dtype=F32)


def _swa_prompt_kernel(sink_ref, q_ref, kvc_ref, kvp_ref, bias_ref, o_ref, *, tq):
    i = pl.program_id(1)
    kvc = kvc_ref[0]
    kvp = kvp_ref[0]
    kcat = jnp.concatenate([kvp[:, 0:LANES], kvc[:, 0:LANES]], axis=0).astype(BF16)
    vcat = jnp.concatenate([kvp[:, LANES:], kvc[:, LANES:]], axis=0).astype(BF16)
    lane = lax.broadcasted_iota(jnp.int32, (WINDOW, LANES), 1)
    col = lax.broadcasted_iota(jnp.int32, (WINDOW, 2 * WINDOW), 1)
    for j in range(tq // WINDOW):
        keys = kcat[j * WINDOW:(j + 2) * WINDOW, :]
        vals = vcat[j * WINDOW:(j + 2) * WINDOW, :]
        for g in range(G_A):
            q2 = q_ref[0, j * WINDOW:(j + 1) * WINDOW, g * LANES:(g + 1) * LANES]
            outs = []
            for kv in range(HKV_A):
                hq = kv * G_A + g
                in_head = (lane < HD_A) if kv == 0 else (lane >= HD_A)
                qm = jnp.where(in_head, q2, jnp.zeros_like(q2))
                s = _dot_nt(qm, keys) + bias_ref[hq]
                if j == 0:
                    s = jnp.where(jnp.logical_and(i == 0, col < WINDOW), -jnp.inf, s)
                sink = sink_ref[hq]
                m = jnp.maximum(jnp.max(s, axis=-1, keepdims=True), sink)
                p = jnp.exp(s - m)
                den = jnp.sum(p, axis=-1, keepdims=True) + jnp.exp(sink - m)
                outs.append(_dot(p.astype(BF16), vals) / den)
            o2 = jnp.where(lane < HD_A, outs[0], outs[1])
            o_ref[0, j * WINDOW:(j + 1) * WINDOW, g * LANES:(g + 1) * LANES] = o2.astype(BF16)


GLA_LEVELS = (0, 1, 2, 4, 8, 16, 32)
LOG2E = 1.4426950408889634


def _gla_constants():
    r = np.arange(GLA_GROUP)
    i, j = r[:, None], r[None, :]
    ranges = [(i >= j) & (i // GLA_CHUNK == j // GLA_CHUNK)]
    masks = [i == j]
    for s in GLA_LEVELS[1:]:
        mid = (i // (2 * s)) * (2 * s) + s
        upper = (i % (2 * s) >= s) & (j >= mid) & (j <= i)
        lower = (i % (2 * s) < s) & (j > i) & (j < mid)
        ranges.append(upper | lower)
        masks.append((i // (2 * s) == j // (2 * s)) & (i % (2 * s) >= s) & (j % (2 * s) < s))
    ones_bd = (np.arange(QB_W)[:, None] // DK_B) == (np.arange(VB_W)[None, :] // DV_B)
    return (jnp.asarray(np.concatenate(ranges, axis=0), dtype=BF16),
            jnp.asarray(ones_bd, dtype=BF16), jnp.asarray(np.stack(masks), dtype=F32))


def _head_norm_gate(o, r, gn):
    parts = []
    for h in range(H_B):
        oh = o[:, h * DV_B:(h + 1) * DV_B]
        parts.append(oh * lax.rsqrt(jnp.mean(oh * oh, axis=-1, keepdims=True) + EPS))
    return jnp.concatenate(parts, axis=-1) * gn * jax.nn.silu(r)


def _gla_prompt_kernel(q_ref, k_ref, v_ref, g_ref, r_ref, gn_ref, rng_ref, lvl_ref,
                       o_ref, st_ref, state, b_scr, o_scr, qk_scr, *, tm):
    @pl.when(pl.program_id(1) == 0)
    def _():
        state[...] = jnp.zeros_like(state)

    q = q_ref[0]
    k = k_ref[0]
    v = v_ref[0]
    lane = lax.broadcasted_iota(jnp.int32, (1, LANES), 1)
    head_lanes = (lane < DK_B, lane >= DK_B)
    head_cols = lax.broadcasted_iota(jnp.int32, (1, QB_W), 1) // DK_B

    g2 = g_ref[0] * LOG2E
    ghi = g2.astype(BF16)
    glo = (g2 - ghi.astype(F32)).astype(BF16)
    nlev = len(GLA_LEVELS)
    for gi in range(tm // GLA_GROUP):
        rows = slice(gi * GLA_GROUP, (gi + 1) * GLA_GROUP)
        d_all = _dot(rng_ref[...], ghi[rows]) + _dot(rng_ref[...], glo[rows])
        b_scr[rows, :] = d_all[0:GLA_GROUP]
        qk_scr[0, 0, rows, :] = q[rows].astype(BF16)
        qk_scr[0, 1, rows, :] = k[rows].astype(BF16)
        for li in range(1, nlev):
            e = jnp.exp2(d_all[li * GLA_GROUP:(li + 1) * GLA_GROUP])
            qk_scr[li, 0, rows, :] = (q[rows] * e).astype(BF16)
            qk_scr[li, 1, rows, :] = (k[rows] * e).astype(BF16)
    b = b_scr[...]

    for gi in range(tm // GLA_GROUP):
        rows = slice(gi * GLA_GROUP, (gi + 1) * GLA_GROUP)
        for pi in range(H_B // 2):
            pr = slice(pi * LANES, (pi + 1) * LANES)
            a = None
            for li in range(nlev):
                ke = qk_scr[li, 1, rows, pr]
                kem = jnp.concatenate([jnp.where(sel, ke, jnp.zeros_like(ke)) for sel in head_lanes],
                                      axis=0)
                term = _dot_nt(qk_scr[li, 0, rows, pr], kem)
                keep = lvl_ref[li] > 0.5
                keep = jnp.concatenate([keep, keep], axis=1)
                a = jnp.where(keep, term, 0.0 if a is None else a)
            a = a.astype(BF16)
            for par in range(2):
                cols = slice((2 * pi + par) * DV_B, (2 * pi + par + 1) * DV_B)
                o_scr[rows, cols] = _dot(a[:, par * GLA_GROUP:(par + 1) * GLA_GROUP], v[rows, cols])

    for c in range(tm // GLA_CHUNK):
        rows = slice(c * GLA_CHUNK, (c + 1) * GLA_CHUNK)
        bc = b[rows]
        b_end = bc[GLA_CHUNK - 1:GLA_CHUNK, :]
        qe = (q[rows] * jnp.exp2(bc)).astype(BF16)
        ke = (k[rows] * jnp.exp2(b_end - bc)).astype(BF16)
        st = state[...]
        qm = jnp.concatenate([jnp.where(head_cols == h, qe, jnp.zeros_like(qe))
                              for h in range(H_B)], axis=0)
        km = jnp.concatenate([jnp.where(head_cols == h, ke, jnp.zeros_like(ke))
                              for h in range(H_B)], axis=0)
        vm = jnp.concatenate([v[rows, h * DV_B:(h + 1) * DV_B] for h in range(H_B)], axis=0)
        oi = _dot_nt(qm, st.astype(BF16))
        for h in range(H_B):
            o_scr[rows, h * DV_B:(h + 1) * DV_B] += oi[h * GLA_CHUNK:(h + 1) * GLA_CHUNK]
        state[...] = st * jnp.exp2(b_end) + _dot_tn(vm, km)

    o_ref[0] = _head_norm_gate(o_scr[...], r_ref[0], gn_ref[...]).astype(BF16)
    st_ref[0] = state[...]


N_ATTN_IN = 13


def _attn_prompt_kernel(*refs, tm, n_cast):
    (sink_ref, qa_ref, kvc_ref, kvp_ref, bias_ref,
     q_ref, k_ref, v_ref, g_ref, r_ref, gn_ref, rng_ref, lvl_ref) = refs[:N_ATTN_IN]
    cast_in = refs[N_ATTN_IN:N_ATTN_IN + n_cast]
    oa_ref, ob_ref, st_ref = refs[N_ATTN_IN + n_cast:N_ATTN_IN + n_cast + 3]
    cast_out = refs[N_ATTN_IN + n_cast + 3:N_ATTN_IN + 2 * n_cast + 3]
    state, b_scr, o_scr, qk_scr = refs[N_ATTN_IN + 2 * n_cast + 3:]
    _gla_prompt_kernel(q_ref, k_ref, v_ref, g_ref, r_ref, gn_ref, rng_ref, lvl_ref,
                       ob_ref, st_ref, state, b_scr, o_scr, qk_scr, tm=tm)
    _swa_prompt_kernel(sink_ref, qa_ref, kvc_ref, kvp_ref, bias_ref, oa_ref, tq=tm)
    for src, dst in zip(cast_in, cast_out):
        dst[...] = src[...].astype(BF16)


def _attn_prompt(qa3d, kv3d, sinks, q3d, k3d, v3d, g3d, r3d, gn, to_bf16):
    nb, S, _ = q3d.shape
    tm = min(TM_ATTN, S)
    nj = S // tm
    nsteps = nb * nj
    bias = _swa_bias_prompt()
    rng, _, lvl = _gla_constants()
    kern = functools.partial(_attn_prompt_kernel, tm=tm, n_cast=len(to_bf16))
    blocks_per_tile = tm // WINDOW
    tok = lambda w: pl.BlockSpec((1, tm, w), lambda b, j: (b, j, 0))
    full = lambda a: pl.BlockSpec(a.shape, lambda b, j: (0,) * a.ndim)
    prev_kv = pl.BlockSpec((1, WINDOW, KV_W),
                           lambda b, j: (b, jnp.maximum(j * blocks_per_tile - 1, 0), 0))

    def slab(a):
        rows = a.shape[0] // nsteps
        assert rows * nsteps == a.shape[0] and rows % (2 * SUBLANES) == 0, a.shape
        return pl.BlockSpec((rows, a.shape[1]), lambda b, j: (b * nj + j, 0))

    outs = pl.pallas_call(
        kern, grid=(nb, nj),
        in_specs=[pl.BlockSpec(memory_space=pltpu.SMEM), tok(QA_W), tok(KV_W), prev_kv, full(bias),
                  tok(QB_W), tok(QB_W), tok(VB_W), tok(QB_W), tok(VB_W),
                  full(gn), full(rng), full(lvl)] + [slab(a) for a in to_bf16],
        out_specs=[tok(QA_W), tok(VB_W), pl.BlockSpec((1, DV_B, QB_W), lambda b, j: (b, 0, 0))]
        + [slab(a) for a in to_bf16],
        out_shape=[jax.ShapeDtypeStruct((nb, S, QA_W), BF16),
                   jax.ShapeDtypeStruct((nb, S, VB_W), BF16),
                   jax.ShapeDtypeStruct((nb, DV_B, QB_W), F32)]
        + [jax.ShapeDtypeStruct(a.shape, BF16) for a in to_bf16],
        scratch_shapes=[pltpu.VMEM((DV_B, QB_W), F32), pltpu.VMEM((tm, QB_W), F32),
                        pltpu.VMEM((tm, VB_W), F32),
                        pltpu.VMEM((len(GLA_LEVELS), 2, tm, QB_W), BF16)],
        compiler_params=_cparams("parallel", "arbitrary"), name="attn_prompt",
    )(sinks, qa3d, kv3d, kv3d, bias, q3d, k3d, v3d, g3d, r3d, gn, rng, lvl, *to_bf16)
    return outs[0], outs[1], outs[2], outs[3:]


def _swa_bias_decode(L):
    rows = HQ_A * L
    kpad = DEC_KEYS
    hq = np.arange(rows)[:, None] // L
    t = np.arange(rows)[:, None] % L
    s = np.arange(kpad)[None, :]
    dist = WINDOW + t - s
    valid = (dist >= 0) & (dist <= WINDOW) & (s < WINDOW + L)
    slopes = 2.0 ** (-8.0 * (hq + 1) / HQ_A)
    return jnp.asarray(np.where(valid, -slopes * dist, -np.inf), dtype=F32)


def _swa_decode_kernel(q_ref, kvn_ref, ck_ref, cv_ref, bias_ref, sink_ref,
                       o_ref, nk_ref, nv_ref, qs, os_, kc, vc, kn, vn, *, ns, L, sb):
    i = pl.program_id(0)
    rows_q = HQ_A * L
    lane = lax.broadcasted_iota(jnp.int32, (1, LANES), 1)

    @pl.when(i == 0)
    def _():
        q = q_ref[...].astype(F32)
        for kv in range(HKV_A):
            sel = (lane < HD_A) if kv == 0 else (lane >= HD_A)
            for g in range(G_A):
                base = (kv * G_A + g) * L * ns
                qs[base:base + L * ns, :] = jnp.where(sel, q[:, g * LANES:(g + 1) * LANES], 0.0)
        kc[...] = jnp.zeros_like(kc)
        vc[...] = jnp.zeros_like(vc)
        kn[...] = kvn_ref[:, 0:LANES]
        vn[...] = kvn_ref[:, LANES:]

    bias = bias_ref[...]
    sink = sink_ref[...]

    def body(t, carry):
        lhs = []
        for u in range(DEC_UNROLL):
            s = t * DEC_UNROLL + u
            seq = i * sb + s
            kc[u, 0:WINDOW, :] = ck_ref[s]
            vc[u, 0:WINDOW, :] = cv_ref[s]
            kc[u, WINDOW:WINDOW + L, :] = kn[pl.ds(seq, L, stride=ns), :]
            vc[u, WINDOW:WINDOW + L, :] = vn[pl.ds(seq, L, stride=ns), :]
            nk_ref[s] = kc[u, L:L + WINDOW, :]
            nv_ref[s] = vc[u, L:L + WINDOW, :]
            lhs.append(qs[pl.ds(seq, rows_q, stride=ns), :].astype(BF16))
        sc = jnp.einsum("uqd,ukd->uqk", jnp.stack(lhs), kc[...].astype(BF16),
                        preferred_element_type=F32) + bias
        m = jnp.maximum(jnp.max(sc, axis=-1, keepdims=True), sink)
        p = jnp.exp(sc - m)
        den = jnp.sum(p, axis=-1, keepdims=True) + jnp.exp(sink - m)
        res = jnp.einsum("uqk,ukd->uqd", p.astype(BF16), vc[...].astype(BF16),
                         preferred_element_type=F32) / den
        half = rows_q // 2
        for u in range(DEC_UNROLL):
            seq = i * sb + t * DEC_UNROLL + u
            os_[pl.ds(seq, half, stride=ns), :] = jnp.where(lane < HD_A, res[u, 0:half],
                                                            res[u, half:])
        return carry

    lax.fori_loop(0, sb // DEC_UNROLL, body, 0)

    @pl.when(i == pl.num_programs(0) - 1)
    def _():
        for g in range(G_A):
            o_ref[:, g * LANES:(g + 1) * LANES] = os_[g * L * ns:(g + 1) * L * ns, :].astype(BF16)


def _swa_decode(q_tm, kvn_tm, cache_k, cache_v, sinks, *, ns, L):
    sb = min(SEQ_BLOCK, ns)
    bias = _swa_bias_decode(L)
    sink_col = jnp.broadcast_to(jnp.repeat(sinks.astype(F32), L)[:, None], (HQ_A * L, 1))
    kern = functools.partial(_swa_decode_kernel, ns=ns, L=L, sb=sb)
    full = lambda a: pl.BlockSpec(a.shape, lambda i: (0,) * a.ndim)
    cache = pl.BlockSpec((sb, WINDOW, LANES), lambda i: (i, 0, 0))
    kpad = DEC_KEYS
    return pl.pallas_call(
        kern, grid=(ns // sb,),
        in_specs=[full(q_tm), full(kvn_tm), cache, cache, full(bias), full(sink_col)],
        out_specs=(pl.BlockSpec((L * ns, QA_W), lambda i: (0, 0)), cache, cache),
        out_shape=(jax.ShapeDtypeStruct((L * ns, QA_W), BF16),
                   jax.ShapeDtypeStruct(cache_k.shape, F32),
                   jax.ShapeDtypeStruct(cache_v.shape, F32)),
        scratch_shapes=[pltpu.VMEM((HQ_A * L * ns, LANES), F32),
                        pltpu.VMEM((G_A * L * ns, LANES), F32),
                        pltpu.VMEM((DEC_UNROLL, kpad, LANES), F32),
                        pltpu.VMEM((DEC_UNROLL, kpad, LANES), F32),
                        pltpu.VMEM((L * ns, LANES), F32), pltpu.VMEM((L * ns, LANES), F32)],
        compiler_params=_cparams("arbitrary"), name="swa_decode",
    )(q_tm, kvn_tm, cache_k, cache_v, bias, sink_col)


def _gla_decode_kernel(q_ref, k_ref, v_ref, g_ref, r_ref, gn_ref, ones_ref, s0_ref,
                       o_ref, s1_ref, qe2, ke2, v2, dec3, oi2, od, *, ns, L, sb):
    i = pl.program_id(0)
    lane = lax.broadcasted_iota(jnp.int32, (1, LANES), 1)
    npair = H_B // 2

    @pl.when(i == 0)
    def _():
        slab = lambda a, t: a[t * ns:(t + 1) * ns, :]
        q, k, g = q_ref[...], k_ref[...], g_ref[...]
        vf = v_ref[...].astype(F32)
        b = [slab(g, 0)]
        for t in range(1, L):
            b.append(b[-1] + slab(g, t))
        for t in range(L):
            acc = jnp.zeros((ns, VB_W), F32)
            for jj in range(t + 1):
                p = (slab(q, t) * slab(k, jj) * jnp.exp(b[t] - b[jj])).astype(BF16)
                acc = acc + _dot(p, ones_ref[...]) * slab(vf, jj)
            od[t * ns:(t + 1) * ns, :] = acc
            qe = slab(q, t) * jnp.exp(b[t])
            ke = slab(k, t) * jnp.exp(b[L - 1] - b[t])
            for par in range(2):
                sel = (lane < DK_B) if par == 0 else (lane >= DK_B)
                base = (par * L + t) * ns
                for pi in range(npair):
                    pr = slice(pi * LANES, (pi + 1) * LANES)
                    qe2[pi, base:base + ns, :] = jnp.where(sel, qe[:, pr], 0.0)
                    ke2[pi, base:base + ns, :] = jnp.where(sel, ke[:, pr], 0.0)
                    h = 2 * pi + par
                    v2[pi, base:base + ns, :] = slab(vf, t)[:, h * DV_B:(h + 1) * DV_B]
        dec3[...] = jnp.zeros_like(dec3)
        hi, mid, lo = _split3(jnp.exp(b[L - 1]))
        for pi in range(npair):
            pr = slice(pi * LANES, (pi + 1) * LANES)
            dec3[pi, 0:ns, :] = hi[:, pr].astype(F32)
            dec3[pi, ns:2 * ns, :] = mid[:, pr].astype(F32)
            dec3[pi, 2 * ns:3 * ns, :] = lo[:, pr].astype(F32)

    ones8 = jnp.ones((2 * L, LANES), BF16)

    def one_seq(s):
        seq = i * sb + s
        for pi in range(npair):
            take = lambda ref: ref[pi, pl.ds(seq, 2 * L, stride=ns), :].astype(BF16)
            st_p = s0_ref[s, pi * LANES:(pi + 1) * LANES, :]
            oi2[pi, pl.ds(seq, 2 * L, stride=ns), :] = _dot(take(qe2), st_p.astype(BF16))
            dcol = _dot_tn(take(dec3), ones8)
            upd = _dot_tn(take(ke2), take(v2))
            s1_ref[s, pi * LANES:(pi + 1) * LANES, :] = dcol * st_p + upd

    def body(t, carry):
        for u in range(DEC_UNROLL):
            one_seq(t * DEC_UNROLL + u)
        return carry

    lax.fori_loop(0, sb // DEC_UNROLL, body, 0)

    @pl.when(i == pl.num_programs(0) - 1)
    def _():
        for t in range(L):
            parts = []
            for h in range(H_B):
                base = ((h % 2) * L + t) * ns
                parts.append(oi2[h // 2, base:base + ns, :])
            o = jnp.concatenate(parts, axis=-1) + od[t * ns:(t + 1) * ns, :]
            o_ref[t * ns:(t + 1) * ns, :] = _head_norm_gate(
                o, r_ref[t * ns:(t + 1) * ns, :], gn_ref[...]).astype(BF16)


def _gla_decode(q_tm, k_tm, v_tm, g_tm, r_tm, gn, s0, *, ns, L):
    sb = min(SEQ_BLOCK, ns)
    assert 2 * L == SUBLANES
    _, ones_bd, _ = _gla_constants()
    kern = functools.partial(_gla_decode_kernel, ns=ns, L=L, sb=sb)
    full = lambda a: pl.BlockSpec(a.shape, lambda i: (0,) * a.ndim)
    st = pl.BlockSpec((sb, QB_W, DV_B), lambda i: (i, 0, 0))
    rows2 = 2 * L * ns
    return pl.pallas_call(
        kern, grid=(ns // sb,),
        in_specs=[full(q_tm), full(k_tm), full(v_tm), full(g_tm), full(r_tm), full(gn),
                  full(ones_bd), st],
        out_specs=(pl.BlockSpec((L * ns, VB_W), lambda i: (0, 0)), st),
        out_shape=(jax.ShapeDtypeStruct((L * ns, VB_W), BF16),
                   jax.ShapeDtypeStruct(s0.shape, F32)),
        scratch_shapes=[pltpu.VMEM((H_B // 2, rows2, LANES), F32) for _ in range(5)]
        + [pltpu.VMEM((L * ns, VB_W), F32)],
        compiler_params=_cparams("arbitrary"), name="gla_decode",
    )(q_tm, k_tm, v_tm, g_tm, r_tm, gn, ones_bd, s0)


def _qa_perm():
    return np.asarray([(kv * G_A + g) * HD_A + d
                       for g in range(G_A) for kv in range(HKV_A) for d in range(HD_A)])


def _prep_even(w_in, w_gate_up, b_gate, w_out):
    perm = _qa_perm()
    w_main = jnp.concatenate([w_in[:, :QA_W][:, perm], w_in[:, QA_W:MAIN_W]], axis=1).astype(BF16)
    w_g = jnp.pad(w_in[:, MAIN_W:], ((0, 0), (0, LANES - GATE_RANK))).astype(BF16)
    w_gu = jnp.pad(w_gate_up, ((0, LANES - GATE_RANK), (0, 0))).astype(BF16)
    w_o = jnp.concatenate([w_out[:QA_W][perm], w_out[QA_W:]], axis=0).astype(BF16)
    return w_main, w_g, w_gu, b_gate.reshape(1, -1), w_o


def _row(v):
    return v.reshape(1, -1)


def kernel(x_prompt, x_sample, cache_swa_k, cache_swa_v, state_gla, state_conv, state_ffn,
           norm_mix_pre, norm_mix_post, norm_ffn_pre, norm_ffn_post, w_in_even, w_gate_up, b_gate,
           attn_sinks, gla_norm, w_out_even, w_in_odd, conv_w_odd, w_out_odd, ffn_up, ffn_conv_w,
           ffn_conv_b, ffn_down):
    nb, S, _ = x_prompt.shape
    ns, L, _ = x_sample.shape
    depth = norm_mix_pre.shape[0]

    xp = x_prompt
    xs = x_sample.transpose(1, 0, 2).reshape(1, L * ns, D_MODEL)
    past_p = max(SUBLANES, 2)
    tm_p = min(TM_TOK, S)

    ks_p, vs_p, gs_p, cs_p, fs_p = [], [], [], [], []
    ks_s, vs_s, gs_s, cs_s, fs_s = [], [], [], [], []

    for l in range(depth):
        gpre, gpost = _row(norm_mix_pre[l]), _row(norm_mix_post[l])
        if l % 2 == 0:
            e = l // 2
            w_main, w_g, w_gu, b_g, w_o = _prep_even(w_in_even[e], w_gate_up[e], b_gate[e],
                                                     w_out_even[e])
            gn = _row(gla_norm[e])
            qa, kv, qb, kb, vb, rb, gb = _inproj_even(xp.reshape(nb * S, D_MODEL), gpre,
                                                      w_main, w_g, w_gu, b_g)
            r3 = lambda a: a.reshape(nb, S, a.shape[-1])
            to_bf16 = []
            if e == 0:
                to_bf16 = [ffn_up.reshape(-1, F2), ffn_down.reshape(-1, D_MODEL)]
                if w_in_odd.shape[0]:
                    to_bf16 += [w_in_odd.reshape(-1, 3 * D_MODEL), w_out_odd.reshape(-1, D_MODEL)]
            oa, ob, st_t, narrowed = _attn_prompt(r3(qa), r3(kv), attn_sinks[e], r3(qb), r3(kb),
                                                  r3(vb), r3(gb), r3(rb), gn, to_bf16)
            if e == 0:
                w_up_all = narrowed[0].reshape(ffn_up.shape)
                w_dn_all = narrowed[1].reshape(ffn_down.shape)
                if w_in_odd.shape[0]:
                    w_in_odd_bf = narrowed[2].reshape(w_in_odd.shape)
                    w_out_odd_bf = narrowed[3].reshape(w_out_odd.shape)
            mix_p = ((oa, 0), (ob, 0))
            kv_last = r3(kv)[:, S - WINDOW:, :]
            ks_p.append(kv_last[..., :LANES].reshape(nb, WINDOW, HKV_A, HD_A))
            vs_p.append(kv_last[..., LANES:].reshape(nb, WINDOW, HKV_A, HD_A))
            gs_p.append(st_t.transpose(0, 2, 1).reshape(nb, H_B, DK_B, DV_B))
            qa, kv, qb, kb, vb, rb, gb = _inproj_even(xs.reshape(L * ns, D_MODEL), gpre,
                                                      w_main, w_g, w_gu, b_g)
            oa, nk, nv = _swa_decode(qa, kv, cache_swa_k[e].reshape(ns, WINDOW, LANES),
                                     cache_swa_v[e].reshape(ns, WINDOW, LANES), attn_sinks[e],
                                     ns=ns, L=L)
            ob, s1 = _gla_decode(qb, kb, vb, gb, rb, gn, state_gla[e].reshape(ns, QB_W, DV_B),
                                 ns=ns, L=L)
            mix_s = ((oa.reshape(1, L * ns, QA_W), 0), (ob.reshape(1, L * ns, VB_W), 0))
            ks_s.append(nk.reshape(ns, WINDOW, HKV_A, HD_A))
            vs_s.append(nv.reshape(ns, WINDOW, HKV_A, HD_A))
            gs_s.append(s1.reshape(ns, H_B, DK_B, DV_B))
        else:
            o = l // 2
            w_in = w_in_odd_bf[o]
            w_o = w_out_odd_bf[o]
            y, st = _odd_in(xp, jnp.zeros((nb, past_p, D_MODEL), F32), gpre, w_in, conv_w_odd[o],
                            tm=min(TM_ODD, S), rs=1)
            mix_p = ((y, 0), (y, 1))
            cs_p.append(st[:, past_p - 2:, :])
            y, st = _odd_in(xs, state_conv, gpre, w_in, conv_w_odd[o], tm=L * ns, rs=ns,
                            state_index=o)
            mix_s = ((y, 0), (y, 1))
            cs_s.append(st)

        gmix = gpost
        gpre, gpost = _row(norm_ffn_pre[l]), _row(norm_ffn_post[l])
        cb = _row(ffn_conv_b[l])
        xp, st = _ffn(xp, mix_p[0], mix_p[1], w_o, gmix, jnp.zeros((nb, past_p, F2), F32), gpre,
                      w_up_all, ffn_conv_w[l], cb, w_dn_all, gpost, l, tm=tm_p, rs=1)
        fs_p.append(st[:, past_p - 2:, :])
        xs, st = _ffn(xs, mix_s[0], mix_s[1], w_o, gmix, state_ffn, gpre, w_up_all, ffn_conv_w[l],
                      cb, w_dn_all, gpost, l, tm=L * ns, rs=ns, seq_major=True)
        fs_s.append(st)

    y_sample = xs.reshape(L, ns, D_MODEL).transpose(1, 0, 2)
    return (xp, y_sample, jnp.stack(ks_p), jnp.stack(vs_p), jnp.stack(gs_p), jnp.stack(cs_p),
            jnp.stack(fs_p), jnp.stack(ks_s), jnp.stack(vs_s), jnp.stack(gs_s), jnp.stack(cs_s),
            jnp.stack(fs_s))
```

```python
import functools

import numpy as np
import jax
import jax.numpy as jnp
from jax import lax
from jax.experimental import pallas as pl
from jax.experimental.pallas import tpu as pltpu

F32 = jnp.float32
BF16 = jnp.bfloat16

D_MODEL = 1024
WINDOW = 128
HD_A = 64
HQ_A = 8
HKV_A = 2
G_A = HQ_A // HKV_A
H_B = 4
DK_B = 64
DV_B = 128
GATE_RANK = 16
GATE_TAU = 16.0
GLA_CHUNK = 64
D_FF = ((8 * D_MODEL // 3 + 127) // 128) * 128
F2 = 2 * D_FF
EPS = 1e-6
GELU_C = float(np.sqrt(2.0 / np.pi))
GELU_A = 0.044715

QA_W = HQ_A * HD_A
KV_W = 2 * HKV_A * HD_A
QB_W = H_B * DK_B
VB_W = H_B * DV_B
MAIN_W = QA_W + KV_W + 2 * QB_W + 2 * VB_W

LANES = 128
SUBLANES = 8
VMEM_LIMIT = 56 * 1024 * 1024

TM_TOK = 512
TM_INPROJ = 1024
TM_ODD = 1024
TM_ATTN = 1024
GLA_GROUP = 128
ODD_TC = 256
FFN_TF = 256
FFN_ELEM_DTYPE = BF16
SEQ_BLOCK = 32
DEC_UNROLL = 16
DEC_KEYS = 2 * WINDOW


def _cparams(*sem):
    return pltpu.CompilerParams(dimension_semantics=sem, vmem_limit_bytes=VMEM_LIMIT)


def _rms(x, g):
    return x * lax.rsqrt(jnp.mean(x * x, axis=-1, keepdims=True) + EPS) * g


def _dot(a, b):
    return jnp.dot(a, b, preferred_element_type=F32)


def _dot_nt(a, b):
    return lax.dot_general(a, b, (((1,), (1,)), ((), ())), preferred_element_type=F32)


def _dot_tn(a, b):
    return lax.dot_general(a, b, (((0,), (0,)), ((), ())), preferred_element_type=F32)


def _split3(x):
    hi = x.astype(BF16)
    r1 = x - hi.astype(F32)
    mid = r1.astype(BF16)
    lo = (r1 - mid.astype(F32)).astype(BF16)
    return hi, mid, lo


def _inproj_even_kernel(x_ref, gpre_ref, w_ref, wg_ref, wgu_ref, bg_ref,
                        qa_ref, kv_ref, qb_ref, kb_ref, vb_ref, rb_ref, gb_ref):
    h = _rms(x_ref[...], gpre_ref[...]).astype(BF16)

    def mm(lo, width):
        return _dot(h, w_ref[:, lo:lo + width])

    lo = 0
    qa_ref[...] = (mm(lo, QA_W) * (HD_A ** -0.5)).astype(BF16)
    lo += QA_W
    kv_ref[...] = mm(lo, KV_W)
    lo += KV_W
    qb_ref[...] = mm(lo, QB_W) * (DK_B ** -0.5)
    lo += QB_W
    kb_ref[...] = mm(lo, QB_W)
    lo += QB_W
    vb_ref[...] = mm(lo, VB_W).astype(BF16)
    lo += VB_W
    rb_ref[...] = mm(lo, VB_W)
    glr = _dot(h, wg_ref[...]).astype(BF16)
    z = _dot(glr, wgu_ref[...]) + bg_ref[...]
    gb_ref[...] = (jnp.minimum(z, 0.0) - jnp.log1p(jnp.exp(-jnp.abs(z)))) * (1.0 / GATE_TAU)


def _inproj_even(x2d, gpre, w_main, w_g, w_gu, b_g):
    T = x2d.shape[0]
    tm = min(TM_INPROJ, T)
    row = lambda w: pl.BlockSpec((tm, w), lambda i: (i, 0))
    full = lambda a: pl.BlockSpec(a.shape, lambda i: (0,) * a.ndim)
    out_shape = (
        jax.ShapeDtypeStruct((T, QA_W), BF16), jax.ShapeDtypeStruct((T, KV_W), F32),
        jax.ShapeDtypeStruct((T, QB_W), F32), jax.ShapeDtypeStruct((T, QB_W), F32),
        jax.ShapeDtypeStruct((T, VB_W), BF16), jax.ShapeDtypeStruct((T, VB_W), F32),
        jax.ShapeDtypeStruct((T, QB_W), F32))
    return pl.pallas_call(
        _inproj_even_kernel, grid=(T // tm,),
        in_specs=[row(D_MODEL), full(gpre), full(w_main), full(w_g), full(w_gu), full(b_g)],
        out_specs=(row(QA_W), row(KV_W), row(QB_W), row(QB_W), row(VB_W), row(VB_W), row(QB_W)),
        out_shape=out_shape, compiler_params=_cparams("parallel"), name="inproj_even",
    )(x2d, gpre, w_main, w_g, w_gu, b_g)


def _conv3_from_buf(buf, cw, cur, tm, past, rs, dtype=F32):
    cw = cw.astype(dtype)
    y = cw[0:1, :] * buf[past - 2 * rs:past - 2 * rs + tm, :].astype(dtype)
    y = y + cw[1:2, :] * buf[past - rs:past - rs + tm, :].astype(dtype)
    return y + cw[2:3, :] * cur.astype(dtype)


def _load_history(carry, past_ref, rs, seq_major):
    if seq_major:
        carry[0:rs, :] = past_ref[:, 0, :]
        carry[rs:2 * rs, :] = past_ref[:, 1, :]
    else:
        carry[...] = past_ref[0]


def _store_history(st_ref, carry, rs, seq_major):
    if seq_major:
        st_ref[:, 0, :] = carry[0:rs, :]
        st_ref[:, 1, :] = carry[rs:2 * rs, :]
    else:
        st_ref[0] = carry[...]


def _history_specs(past0, index, channels, seq_major):
    if seq_major:
        ns = past0.shape[1]
        return (2 * ns,
                pl.BlockSpec((None, ns, 2, channels), lambda b, j: (index, 0, 0, 0)),
                pl.BlockSpec((ns, 2, channels), lambda b, j: (0, 0, 0)),
                jax.ShapeDtypeStruct((ns, 2, channels), F32))
    past = past0.shape[1]
    spec = pl.BlockSpec((1, past, channels), lambda b, j: (b, 0, 0))
    return past, spec, spec, jax.ShapeDtypeStruct((past0.shape[0], past, channels), F32)


def _odd_in_kernel(x_ref, past_ref, gpre_ref, win_ref, cw_ref, y_ref, st_ref, carry, buf,
                   *, tm, past, rs, tc, seq_major):
    @pl.when(pl.program_id(1) == 0)
    def _():
        _load_history(carry, past_ref, rs, seq_major)

    h = _rms(x_ref[0], gpre_ref[...]).astype(BF16)
    for c in range(D_MODEL // tc):
        lo = c * tc
        bg = _dot(h, win_ref[:, lo:lo + tc])
        cu = _dot(h, win_ref[:, D_MODEL + lo:D_MODEL + lo + tc]) * \
            _dot(h, win_ref[:, 2 * D_MODEL + lo:2 * D_MODEL + lo + tc])
        buf[0:past, :] = carry[:, lo:lo + tc]
        buf[past:past + tm, :] = cu
        carry[:, lo:lo + tc] = buf[tm:tm + past, :]
        z = _conv3_from_buf(buf, cw_ref[:, lo:lo + tc], cu, tm, past, rs)
        y_ref[0, :, lo:lo + tc] = (bg * z).astype(BF16)
    _store_history(st_ref, carry, rs, seq_major)


def _odd_in(x3d, past0, gpre, w_in, conv_w, *, tm, rs, state_index=None):
    nb, rows, _ = x3d.shape
    seq_major = state_index is not None
    past, past_spec, st_spec, st_shape = _history_specs(past0, state_index, D_MODEL, seq_major)
    tc = ODD_TC
    kern = functools.partial(_odd_in_kernel, tm=tm, past=past, rs=rs, tc=tc, seq_major=seq_major)
    full = lambda a: pl.BlockSpec(a.shape, lambda b, j: (0,) * a.ndim)
    return pl.pallas_call(
        kern, grid=(nb, rows // tm),
        in_specs=[pl.BlockSpec((1, tm, D_MODEL), lambda b, j: (b, j, 0)), past_spec,
                  full(gpre), full(w_in), full(conv_w)],
        out_specs=(pl.BlockSpec((1, tm, D_MODEL), lambda b, j: (b, j, 0)), st_spec),
        out_shape=(jax.ShapeDtypeStruct((nb, rows, D_MODEL), BF16), st_shape),
        scratch_shapes=[pltpu.VMEM((past, D_MODEL), F32), pltpu.VMEM((past + tm, tc), F32)],
        compiler_params=_cparams("parallel", "arbitrary"), name="odd_in",
    )(x3d, past0, gpre, w_in, conv_w)


def _ffn_kernel(x_ref, ya_ref, yb_ref, wo_ref, gmix_ref, past_ref, gpre_ref, wup_ref, cw_ref,
                cb_ref, wdn_ref, gpost_ref, gk_ref, o_ref, st_ref,
                carry, buf_g, buf_v, acc, h_scr, act, x1_scr, *, tm, past, rs, tf, seq_major):
    @pl.when(pl.program_id(1) == 0)
    def _():
        _load_history(carry, past_ref, rs, seq_major)

    mix = _dot(ya_ref[0], wo_ref[0]) + _dot(yb_ref[0], wo_ref[1])
    x1_scr[...] = x_ref[0] + _rms(mix, gmix_ref[...])
    h_scr[...] = _rms(x1_scr[...], gpre_ref[...]).astype(BF16)
    nf = D_FF // tf

    def up(c):
        for lo, buf in ((c * tf, buf_g), (D_FF + c * tf, buf_v)):
            buf[c % 2, 0:past, :] = carry[:, lo:lo + tf]
            buf[c % 2, past:past + tm, :] = _dot(h_scr[...], wup_ref[:, lo:lo + tf])
            carry[:, lo:lo + tf] = buf[c % 2, tm:tm + past, :]

    gelu_c = gk_ref[0:1, 0:1].astype(FFN_ELEM_DTYPE)
    gelu_ca = gk_ref[0:1, 1:2].astype(FFN_ELEM_DTYPE)

    def elem(c):
        halves = []
        for lo, buf, scale in ((c * tf, buf_g, 1.0), (D_FF + c * tf, buf_v, 0.5)):
            b = buf.at[c % 2]
            halves.append(_conv3_from_buf(b, cw_ref[:, lo:lo + tf] * scale, b[past:past + tm, :],
                                          tm, past, rs, FFN_ELEM_DTYPE)
                          + (cb_ref[:, lo:lo + tf] * scale).astype(FFN_ELEM_DTYPE))
        x, half_v = halves
        t = jnp.tanh(x * (x * x * gelu_ca + gelu_c))
        act[c % 2] = ((x * t + x) * half_v).astype(BF16)

    def down(c):
        part = _dot(act[c % 2], wdn_ref[c * tf:(c + 1) * tf, :])
        if c == 0:
            acc[...] = part
        else:
            acc[...] += part

    for c in range(nf + 2):
        if c < nf:
            up(c)
        if 1 <= c <= nf:
            elem(c - 1)
        if c >= 2:
            down(c - 2)
    o_ref[0] = x1_scr[...] + _rms(acc[...], gpost_ref[...])
    _store_history(st_ref, carry, rs, seq_major)


def _ffn(x3d, ya, yb, w_o, gmix, past0, gpre, w_up, conv_w, conv_b, w_dn, gpost, layer, *, tm, rs,
         seq_major=False):
    nb, rows, _ = x3d.shape
    past, past_spec, st_spec, st_shape = _history_specs(past0, layer, F2, seq_major)
    half = w_o.shape[0] // 2
    kern = functools.partial(_ffn_kernel, tm=tm, past=past, rs=rs, tf=FFN_TF, seq_major=seq_major)
    gelu_k = jnp.zeros((1, LANES), F32).at[0, 0].set(GELU_C).at[0, 1].set(GELU_C * GELU_A)
    w_o2 = w_o.reshape(2, half, D_MODEL)
    once = dict(pipeline_mode=pl.Buffered(1))
    full = lambda a: pl.BlockSpec(a.shape, lambda b, j: (0,) * a.ndim, **once)
    per_layer = lambda a: pl.BlockSpec((None,) + a.shape[1:], lambda b, j: (layer, 0, 0), **once)
    tok = lambda w, cblk=0: pl.BlockSpec((1, tm, w), lambda b, j: (b, j, cblk))
    return pl.pallas_call(
        kern, grid=(nb, rows // tm),
        in_specs=[tok(D_MODEL), tok(half, ya[1]), tok(half, yb[1]), full(w_o2), full(gmix),
                  past_spec,
                  full(gpre), per_layer(w_up), full(conv_w), full(conv_b), per_layer(w_dn),
                  full(gpost), full(gelu_k)],
        out_specs=(tok(D_MODEL), st_spec),
        out_shape=(jax.ShapeDtypeStruct((nb, rows, D_MODEL), F32), st_shape),
        scratch_shapes=[pltpu.VMEM((past, F2), F32), pltpu.VMEM((2, past + tm, FFN_TF), F32),
                        pltpu.VMEM((2, past + tm, FFN_TF), F32), pltpu.VMEM((tm, D_MODEL), F32),
                        pltpu.VMEM((tm, D_MODEL), BF16), pltpu.VMEM((2, tm, FFN_TF), BF16),
                        pltpu.VMEM((tm, D_MODEL), F32)],
        compiler_params=_cparams("parallel", "arbitrary"), name="conv_ffn",
    )(x3d, ya[0], yb[0], w_o2, gmix, past0, gpre, w_up, conv_w, conv_b, w_dn, gpost, gelu_k)


def _swa_bias_prompt():
    qi = np.arange(WINDOW)[:, None]
    sj = np.arange(2 * WINDOW)[None, :]
    dist = WINDOW + qi - sj
    valid = (dist >= 0) & (dist <= WINDOW)
    slopes = 2.0 ** (-8.0 * np.arange(1, HQ_A + 1) / HQ_A)
    bias = np.where(valid[None], -slopes[:, None, None] * dist[None].astype(np.float64), -np.inf)
    return jnp.asarray(bias, dtype=F32)


def _swa_prompt_kernel(sink_ref, q_ref, kvc_ref, kvp_ref, bias_ref, o_ref, *, tq):
    i = pl.program_id(1)
    kvc = kvc_ref[0]
    kvp = kvp_ref[0]
    kcat = jnp.concatenate([kvp[:, 0:LANES], kvc[:, 0:LANES]], axis=0).astype(BF16)
    vcat = jnp.concatenate([kvp[:, LANES:], kvc[:, LANES:]], axis=0).astype(BF16)
    lane = lax.broadcasted_iota(jnp.int32, (WINDOW, LANES), 1)
    col = lax.broadcasted_iota(jnp.int32, (WINDOW, 2 * WINDOW), 1)
    for j in range(tq // WINDOW):
        keys = kcat[j * WINDOW:(j + 2) * WINDOW, :]
        vals = vcat[j * WINDOW:(j + 2) * WINDOW, :]
        for g in range(G_A):
            q2 = q_ref[0, j * WINDOW:(j + 1) * WINDOW, g * LANES:(g + 1) * LANES]
            outs = []
            for kv in range(HKV_A):
                hq = kv * G_A + g
                in_head = (lane < HD_A) if kv == 0 else (lane >= HD_A)
                qm = jnp.where(in_head, q2, jnp.zeros_like(q2))
                s = _dot_nt(qm, keys) + bias_ref[hq]
                if j == 0:
                    s = jnp.where(jnp.logical_and(i == 0, col < WINDOW), -jnp.inf, s)
                sink = sink_ref[hq]
                m = jnp.maximum(jnp.max(s, axis=-1, keepdims=True), sink)
                p = jnp.exp(s - m)
                den = jnp.sum(p, axis=-1, keepdims=True) + jnp.exp(sink - m)
                outs.append(_dot(p.astype(BF16), vals) / den)
            o2 = jnp.where(lane < HD_A, outs[0], outs[1])
            o_ref[0, j * WINDOW:(j + 1) * WINDOW, g * LANES:(g + 1) * LANES] = o2.astype(BF16)


GLA_LEVELS = (0, 1, 2, 4, 8, 16, 32)
LOG2E = 1.4426950408889634


def _gla_constants():
    r = np.arange(GLA_GROUP)
    i, j = r[:, None], r[None, :]
    ranges = [(i >= j) & (i // GLA_CHUNK == j // GLA_CHUNK)]
    masks = [i == j]
    for s in GLA_LEVELS[1:]:
        mid = (i // (2 * s)) * (2 * s) + s
        upper = (i % (2 * s) >= s) & (j >= mid) & (j <= i)
        lower = (i % (2 * s) < s) & (j > i) & (j < mid)
        ranges.append(upper | lower)
        masks.append((i // (2 * s) == j // (2 * s)) & (i % (2 * s) >= s) & (j % (2 * s) < s))
    ones_bd = (np.arange(QB_W)[:, None] // DK_B) == (np.arange(VB_W)[None, :] // DV_B)
    return (jnp.asarray(np.concatenate(ranges, axis=0), dtype=BF16),
            jnp.asarray(ones_bd, dtype=BF16), jnp.asarray(np.stack(masks), dtype=F32))


def _head_norm_gate(o, r, gn):
    parts = []
    for h in range(H_B):
        oh = o[:, h * DV_B:(h + 1) * DV_B]
        parts.append(oh * lax.rsqrt(jnp.mean(oh * oh, axis=-1, keepdims=True) + EPS))
    return jnp.concatenate(parts, axis=-1) * gn * jax.nn.silu(r)


def _gla_prompt_kernel(q_ref, k_ref, v_ref, g_ref, r_ref, gn_ref, rng_ref, lvl_ref,
                       o_ref, st_ref, state, b_scr, o_scr, qk_scr, *, tm):
    @pl.when(pl.program_id(1) == 0)
    def _():
        state[...] = jnp.zeros_like(state)

    q = q_ref[0]
    k = k_ref[0]
    v = v_ref[0]
    lane = lax.broadcasted_iota(jnp.int32, (1, LANES), 1)
    head_lanes = (lane < DK_B, lane >= DK_B)
    head_cols = lax.broadcasted_iota(jnp.int32, (1, QB_W), 1) // DK_B

    g2 = g_ref[0] * LOG2E
    ghi = g2.astype(BF16)
    glo = (g2 - ghi.astype(F32)).astype(BF16)
    nlev = len(GLA_LEVELS)
    for gi in range(tm // GLA_GROUP):
        rows = slice(gi * GLA_GROUP, (gi + 1) * GLA_GROUP)
        d_all = _dot(rng_ref[...], ghi[rows]) + _dot(rng_ref[...], glo[rows])
        b_scr[rows, :] = d_all[0:GLA_GROUP]
        qk_scr[0, 0, rows, :] = q[rows].astype(BF16)
        qk_scr[0, 1, rows, :] = k[rows].astype(BF16)
        for li in range(1, nlev):
            e = jnp.exp2(d_all[li * GLA_GROUP:(li + 1) * GLA_GROUP])
            qk_scr[li, 0, rows, :] = (q[rows] * e).astype(BF16)
            qk_scr[li, 1, rows, :] = (k[rows] * e).astype(BF16)
    b = b_scr[...]

    for gi in range(tm // GLA_GROUP):
        rows = slice(gi * GLA_GROUP, (gi + 1) * GLA_GROUP)
        for pi in range(H_B // 2):
            pr = slice(pi * LANES, (pi + 1) * LANES)
            a = None
            for li in range(nlev):
                ke = qk_scr[li, 1, rows, pr]
                kem = jnp.concatenate([jnp.where(sel, ke, jnp.zeros_like(ke)) for sel in head_lanes],
                                      axis=0)
                term = _dot_nt(qk_scr[li, 0, rows, pr], kem)
                keep = lvl_ref[li] > 0.5
                keep = jnp.concatenate([keep, keep], axis=1)
                a = jnp.where(keep, term, 0.0 if a is None else a)
            a = a.astype(BF16)
            for par in range(2):
                cols = slice((2 * pi + par) * DV_B, (2 * pi + par + 1) * DV_B)
                o_scr[rows, cols] = _dot(a[:, par * GLA_GROUP:(par + 1) * GLA_GROUP], v[rows, cols])

    for c in range(tm // GLA_CHUNK):
        rows = slice(c * GLA_CHUNK, (c + 1) * GLA_CHUNK)
        bc = b[rows]
        b_end = bc[GLA_CHUNK - 1:GLA_CHUNK, :]
        qe = (q[rows] * jnp.exp2(bc)).astype(BF16)
        ke = (k[rows] * jnp.exp2(b_end - bc)).astype(BF16)
        st = state[...]
        qm = jnp.concatenate([jnp.where(head_cols == h, qe, jnp.zeros_like(qe))
                              for h in range(H_B)], axis=0)
        km = jnp.concatenate([jnp.where(head_cols == h, ke, jnp.zeros_like(ke))
                              for h in range(H_B)], axis=0)
        vm = jnp.concatenate([v[rows, h * DV_B:(h + 1) * DV_B] for h in range(H_B)], axis=0)
        oi = _dot_nt(qm, st.astype(BF16))
        for h in range(H_B):
            o_scr[rows, h * DV_B:(h + 1) * DV_B] += oi[h * GLA_CHUNK:(h + 1) * GLA_CHUNK]
        state[...] = st * jnp.exp2(b_end) + _dot_tn(vm, km)

    o_ref[0] = _head_norm_gate(o_scr[...], r_ref[0], gn_ref[...]).astype(BF16)
    st_ref[0] = state[...]


N_ATTN_IN = 13


def _attn_prompt_kernel(*refs, tm, n_cast):
    (sink_ref, qa_ref, kvc_ref, kvp_ref, bias_ref,
     q_ref, k_ref, v_ref, g_ref, r_ref, gn_ref, rng_ref, lvl_ref) = refs[:N_ATTN_IN]
    cast_in = refs[N_ATTN_IN:N_ATTN_IN + n_cast]
    oa_ref, ob_ref, st_ref = refs[N_ATTN_IN + n_cast:N_ATTN_IN + n_cast + 3]
    cast_out = refs[N_ATTN_IN + n_cast + 3:N_ATTN_IN + 2 * n_cast + 3]
    state, b_scr, o_scr, qk_scr = refs[N_ATTN_IN + 2 * n_cast + 3:]
    _gla_prompt_kernel(q_ref, k_ref, v_ref, g_ref, r_ref, gn_ref, rng_ref, lvl_ref,
                       ob_ref, st_ref, state, b_scr, o_scr, qk_scr, tm=tm)
    _swa_prompt_kernel(sink_ref, qa_ref, kvc_ref, kvp_ref, bias_ref, oa_ref, tq=tm)
    for src, dst in zip(cast_in, cast_out):
        dst[...] = src[...].astype(BF16)


def _attn_prompt(qa3d, kv3d, sinks, q3d, k3d, v3d, g3d, r3d, gn, to_bf16):
    nb, S, _ = q3d.shape
    tm = min(TM_ATTN, S)
    nj = S // tm
    nsteps = nb * nj
    bias = _swa_bias_prompt()
    rng, _, lvl = _gla_constants()
    kern = functools.partial(_attn_prompt_kernel, tm=tm, n_cast=len(to_bf16))
    blocks_per_tile = tm // WINDOW
    tok = lambda w: pl.BlockSpec((1, tm, w), lambda b, j: (b, j, 0))
    full = lambda a: pl.BlockSpec(a.shape, lambda b, j: (0,) * a.ndim)
    prev_kv = pl.BlockSpec((1, WINDOW, KV_W),
                           lambda b, j: (b, jnp.maximum(j * blocks_per_tile - 1, 0), 0))

    def slab(a):
        rows = a.shape[0] // nsteps
        assert rows * nsteps == a.shape[0] and rows % (2 * SUBLANES) == 0, a.shape
        return pl.BlockSpec((rows, a.shape[1]), lambda b, j: (b * nj + j, 0))

    outs = pl.pallas_call(
        kern, grid=(nb, nj),
        in_specs=[pl.BlockSpec(memory_space=pltpu.SMEM), tok(QA_W), tok(KV_W), prev_kv, full(bias),
                  tok(QB_W), tok(QB_W), tok(VB_W), tok(QB_W), tok(VB_W),
                  full(gn), full(rng), full(lvl)] + [slab(a) for a in to_bf16],
        out_specs=[tok(QA_W), tok(VB_W), pl.BlockSpec((1, DV_B, QB_W), lambda b, j: (b, 0, 0))]
        + [slab(a) for a in to_bf16],
        out_shape=[jax.ShapeDtypeStruct((nb, S, QA_W), BF16),
                   jax.ShapeDtypeStruct((nb, S, VB_W), BF16),
                   jax.ShapeDtypeStruct((nb, DV_B, QB_W), F32)]
        + [jax.ShapeDtypeStruct(a.shape, BF16) for a in to_bf16],
        scratch_shapes=[pltpu.VMEM((DV_B, QB_W), F32), pltpu.VMEM((tm, QB_W), F32),
                        pltpu.VMEM((tm, VB_W), F32),
                        pltpu.VMEM((len(GLA_LEVELS), 2, tm, QB_W), BF16)],
        compiler_params=_cparams("parallel", "arbitrary"), name="attn_prompt",
    )(sinks, qa3d, kv3d, kv3d, bias, q3d, k3d, v3d, g3d, r3d, gn, rng, lvl, *to_bf16)
    return outs[0], outs[1], outs[2], outs[3:]


def _swa_bias_decode(L):
    rows = HQ_A * L
    kpad = DEC_KEYS
    hq = np.arange(rows)[:, None] // L
    t = np.arange(rows)[:, None] % L
    s = np.arange(kpad)[None, :]
    dist = WINDOW + t - s
    valid = (dist >= 0) & (dist <= WINDOW) & (s < WINDOW + L)
    slopes = 2.0 ** (-8.0 * (hq + 1) / HQ_A)
    return jnp.asarray(np.where(valid, -slopes * dist, -np.inf), dtype=F32)


def _swa_decode_kernel(q_ref, kvn_ref, ck_ref, cv_ref, bias_ref, sink_ref,
                       o_ref, nk_ref, nv_ref, qs, os_, kc, vc, kn, vn, *, ns, L, sb):
    i = pl.program_id(0)
    rows_q = HQ_A * L
    lane = lax.broadcasted_iota(jnp.int32, (1, LANES), 1)

    @pl.when(i == 0)
    def _():
        q = q_ref[...].astype(F32)
        for kv in range(HKV_A):
            sel = (lane < HD_A) if kv == 0 else (lane >= HD_A)
            for g in range(G_A):
                base = (kv * G_A + g) * L * ns
                qs[base:base + L * ns, :] = jnp.where(sel, q[:, g * LANES:(g + 1) * LANES], 0.0)
        kc[...] = jnp.zeros_like(kc)
        vc[...] = jnp.zeros_like(vc)
        kn[...] = kvn_ref[:, 0:LANES]
        vn[...] = kvn_ref[:, LANES:]

    bias = bias_ref[...]
    sink = sink_ref[...]

    def body(t, carry):
        lhs = []
        for u in range(DEC_UNROLL):
            s = t * DEC_UNROLL + u
            seq = i * sb + s
            kc[u, 0:WINDOW, :] = ck_ref[s]
            vc[u, 0:WINDOW, :] = cv_ref[s]
            kc[u, WINDOW:WINDOW + L, :] = kn[pl.ds(seq, L, stride=ns), :]
            vc[u, WINDOW:WINDOW + L, :] = vn[pl.ds(seq, L, stride=ns), :]
            nk_ref[s] = kc[u, L:L + WINDOW, :]
            nv_ref[s] = vc[u, L:L + WINDOW, :]
            lhs.append(qs[pl.ds(seq, rows_q, stride=ns), :].astype(BF16))
        sc = jnp.einsum("uqd,ukd->uqk", jnp.stack(lhs), kc[...].astype(BF16),
                        preferred_element_type=F32) + bias
        m = jnp.maximum(jnp.max(sc, axis=-1, keepdims=True), sink)
        p = jnp.exp(sc - m)
        den = jnp.sum(p, axis=-1, keepdims=True) + jnp.exp(sink - m)
        res = jnp.einsum("uqk,ukd->uqd", p.astype(BF16), vc[...].astype(BF16),
                         preferred_element_type=F32) / den
        half = rows_q // 2
        for u in range(DEC_UNROLL):
            seq = i * sb + t * DEC_UNROLL + u
            os_[pl.ds(seq, half, stride=ns), :] = jnp.where(lane < HD_A, res[u, 0:half],
                                                            res[u, half:])
        return carry

    lax.fori_loop(0, sb // DEC_UNROLL, body, 0)

    @pl.when(i == pl.num_programs(0) - 1)
    def _():
        for g in range(G_A):
            o_ref[:, g * LANES:(g + 1) * LANES] = os_[g * L * ns:(g + 1) * L * ns, :].astype(BF16)


def _swa_decode(q_tm, kvn_tm, cache_k, cache_v, sinks, *, ns, L):
    sb = min(SEQ_BLOCK, ns)
    bias = _swa_bias_decode(L)
    sink_col = jnp.broadcast_to(jnp.repeat(sinks.astype(F32), L)[:, None], (HQ_A * L, 1))
    kern = functools.partial(_swa_decode_kernel, ns=ns, L=L, sb=sb)
    full = lambda a: pl.BlockSpec(a.shape, lambda i: (0,) * a.ndim)
    cache = pl.BlockSpec((sb, WINDOW, LANES), lambda i: (i, 0, 0))
    kpad = DEC_KEYS
    return pl.pallas_call(
        kern, grid=(ns // sb,),
        in_specs=[full(q_tm), full(kvn_tm), cache, cache, full(bias), full(sink_col)],
        out_specs=(pl.BlockSpec((L * ns, QA_W), lambda i: (0, 0)), cache, cache),
        out_shape=(jax.ShapeDtypeStruct((L * ns, QA_W), BF16),
                   jax.ShapeDtypeStruct(cache_k.shape, F32),
                   jax.ShapeDtypeStruct(cache_v.shape, F32)),
        scratch_shapes=[pltpu.VMEM((HQ_A * L * ns, LANES), F32),
                        pltpu.VMEM((G_A * L * ns, LANES), F32),
                        pltpu.VMEM((DEC_UNROLL, kpad, LANES), F32),
                        pltpu.VMEM((DEC_UNROLL, kpad, LANES), F32),
                        pltpu.VMEM((L * ns, LANES), F32), pltpu.VMEM((L * ns, LANES), F32)],
        compiler_params=_cparams("arbitrary"), name="swa_decode",
    )(q_tm, kvn_tm, cache_k, cache_v, bias, sink_col)


def _gla_decode_kernel(q_ref, k_ref, v_ref, g_ref, r_ref, gn_ref, ones_ref, s0_ref,
                       o_ref, s1_ref, qe2, ke2, v2, dec3, oi2, od, *, ns, L, sb):
    i = pl.program_id(0)
    lane = lax.broadcasted_iota(jnp.int32, (1, LANES), 1)
    npair = H_B // 2

    @pl.when(i == 0)
    def _():
        slab = lambda a, t: a[t * ns:(t + 1) * ns, :]
        q, k, g = q_ref[...], k_ref[...], g_ref[...]
        vf = v_ref[...].astype(F32)
        b = [slab(g, 0)]
        for t in range(1, L):
            b.append(b[-1] + slab(g, t))
        for t in range(L):
            acc = jnp.zeros((ns, VB_W), F32)
            for jj in range(t + 1):
                p = (slab(q, t) * slab(k, jj) * jnp.exp(b[t] - b[jj])).astype(BF16)
                acc = acc + _dot(p, ones_ref[...]) * slab(vf, jj)
            od[t * ns:(t + 1) * ns, :] = acc
            qe = slab(q, t) * jnp.exp(b[t])
            ke = slab(k, t) * jnp.exp(b[L - 1] - b[t])
            for par in range(2):
                sel = (lane < DK_B) if par == 0 else (lane >= DK_B)
                base = (par * L + t) * ns
                for pi in range(npair):
                    pr = slice(pi * LANES, (pi + 1) * LANES)
                    qe2[pi, base:base + ns, :] = jnp.where(sel, qe[:, pr], 0.0)
                    ke2[pi, base:base + ns, :] = jnp.where(sel, ke[:, pr], 0.0)
                    h = 2 * pi + par
                    v2[pi, base:base + ns, :] = slab(vf, t)[:, h * DV_B:(h + 1) * DV_B]
        dec3[...] = jnp.zeros_like(dec3)
        hi, mid, lo = _split3(jnp.exp(b[L - 1]))
        for pi in range(npair):
            pr = slice(pi * LANES, (pi + 1) * LANES)
            dec3[pi, 0:ns, :] = hi[:, pr].astype(F32)
            dec3[pi, ns:2 * ns, :] = mid[:, pr].astype(F32)
            dec3[pi, 2 * ns:3 * ns, :] = lo[:, pr].astype(F32)

    ones8 = jnp.ones((2 * L, LANES), BF16)

    def one_seq(s):
        seq = i * sb + s
        for pi in range(npair):
            take = lambda ref: ref[pi, pl.ds(seq, 2 * L, stride=ns), :].astype(BF16)
            st_p = s0_ref[s, pi * LANES:(pi + 1) * LANES, :]
            oi2[pi, pl.ds(seq, 2 * L, stride=ns), :] = _dot(take(qe2), st_p.astype(BF16))
            dcol = _dot_tn(take(dec3), ones8)
            upd = _dot_tn(take(ke2), take(v2))
            s1_ref[s, pi * LANES:(pi + 1) * LANES, :] = dcol * st_p + upd

    def body(t, carry):
        for u in range(DEC_UNROLL):
            one_seq(t * DEC_UNROLL + u)
        return carry

    lax.fori_loop(0, sb // DEC_UNROLL, body, 0)

    @pl.when(i == pl.num_programs(0) - 1)
    def _():
        for t in range(L):
            parts = []
            for h in range(H_B):
                base = ((h % 2) * L + t) * ns
                parts.append(oi2[h // 2, base:base + ns, :])
            o = jnp.concatenate(parts, axis=-1) + od[t * ns:(t + 1) * ns, :]
            o_ref[t * ns:(t + 1) * ns, :] = _head_norm_gate(
                o, r_ref[t * ns:(t + 1) * ns, :], gn_ref[...]).astype(BF16)


def _gla_decode(q_tm, k_tm, v_tm, g_tm, r_tm, gn, s0, *, ns, L):
    sb = min(SEQ_BLOCK, ns)
    assert 2 * L == SUBLANES
    _, ones_bd, _ = _gla_constants()
    kern = functools.partial(_gla_decode_kernel, ns=ns, L=L, sb=sb)
    full = lambda a: pl.BlockSpec(a.shape, lambda i: (0,) * a.ndim)
    st = pl.BlockSpec((sb, QB_W, DV_B), lambda i: (i, 0, 0))
    rows2 = 2 * L * ns
    return pl.pallas_call(
        kern, grid=(ns // sb,),
        in_specs=[full(q_tm), full(k_tm), full(v_tm), full(g_tm), full(r_tm), full(gn),
                  full(ones_bd), st],
        out_specs=(pl.BlockSpec((L * ns, VB_W), lambda i: (0, 0)), st),
        out_shape=(jax.ShapeDtypeStruct((L * ns, VB_W), BF16),
                   jax.ShapeDtypeStruct(s0.shape, F32)),
        scratch_shapes=[pltpu.VMEM((H_B // 2, rows2, LANES), F32) for _ in range(5)]
        + [pltpu.VMEM((L * ns, VB_W), F32)],
        compiler_params=_cparams("arbitrary"), name="gla_decode",
    )(q_tm, k_tm, v_tm, g_tm, r_tm, gn, ones_bd, s0)


def _qa_perm():
    return np.asarray([(kv * G_A + g) * HD_A + d
                       for g in range(G_A) for kv in range(HKV_A) for d in range(HD_A)])


def _prep_even(w_in, w_gate_up, b_gate, w_out):
    perm = _qa_perm()
    w_main = jnp.concatenate([w_in[:, :QA_W][:, perm], w_in[:, QA_W:MAIN_W]], axis=1).astype(BF16)
    w_g = jnp.pad(w_in[:, MAIN_W:], ((0, 0), (0, LANES - GATE_RANK))).astype(BF16)
    w_gu = jnp.pad(w_gate_up, ((0, LANES - GATE_RANK), (0, 0))).astype(BF16)
    w_o = jnp.concatenate([w_out[:QA_W][perm], w_out[QA_W:]], axis=0).astype(BF16)
    return w_main, w_g, w_gu, b_gate.reshape(1, -1), w_o


def _row(v):
    return v.reshape(1, -1)


def kernel(x_prompt, x_sample, cache_swa_k, cache_swa_v, state_gla, state_conv, state_ffn,
           norm_mix_pre, norm_mix_post, norm_ffn_pre, norm_ffn_post, w_in_even, w_gate_up, b_gate,
           attn_sinks, gla_norm, w_out_even, w_in_odd, conv_w_odd, w_out_odd, ffn_up, ffn_conv_w,
           ffn_conv_b, ffn_down):
    nb, S, _ = x_prompt.shape
    ns, L, _ = x_sample.shape
    depth = norm_mix_pre.shape[0]

    xp = x_prompt
    xs = x_sample.transpose(1, 0, 2).reshape(1, L * ns, D_MODEL)
    past_p = max(SUBLANES, 2)
    tm_p = min(TM_TOK, S)

    ks_p, vs_p, gs_p, cs_p, fs_p = [], [], [], [], []
    ks_s, vs_s, gs_s, cs_s, fs_s = [], [], [], [], []

    for l in range(depth):
        gpre, gpost = _row(norm_mix_pre[l]), _row(norm_mix_post[l])
        if l % 2 == 0:
            e = l // 2
            w_main, w_g, w_gu, b_g, w_o = _prep_even(w_in_even[e], w_gate_up[e], b_gate[e],
                                                     w_out_even[e])
            gn = _row(gla_norm[e])
            qa, kv, qb, kb, vb, rb, gb = _inproj_even(xp.reshape(nb * S, D_MODEL), gpre,
                                                      w_main, w_g, w_gu, b_g)
            r3 = lambda a: a.reshape(nb, S, a.shape[-1])
            to_bf16 = []
            if e == 0:
                to_bf16 = [ffn_up.reshape(-1, F2), ffn_down.reshape(-1, D_MODEL)]
                if w_in_odd.shape[0]:
                    to_bf16 += [w_in_odd.reshape(-1, 3 * D_MODEL), w_out_odd.reshape(-1, D_MODEL)]
            oa, ob, st_t, narrowed = _attn_prompt(r3(qa), r3(kv), attn_sinks[e], r3(qb), r3(kb),
                                                  r3(vb), r3(gb), r3(rb), gn, to_bf16)
            if e == 0:
                w_up_all = narrowed[0].reshape(ffn_up.shape)
                w_dn_all = narrowed[1].reshape(ffn_down.shape)
                if w_in_odd.shape[0]:
                    w_in_odd_bf = narrowed[2].reshape(w_in_odd.shape)
                    w_out_odd_bf = narrowed[3].reshape(w_out_odd.shape)
            mix_p = ((oa, 0), (ob, 0))
            kv_last = r3(kv)[:, S - WINDOW:, :]
            ks_p.append(kv_last[..., :LANES].reshape(nb, WINDOW, HKV_A, HD_A))
            vs_p.append(kv_last[..., LANES:].reshape(nb, WINDOW, HKV_A, HD_A))
            gs_p.append(st_t.transpose(0, 2, 1).reshape(nb, H_B, DK_B, DV_B))
            qa, kv, qb, kb, vb, rb, gb = _inproj_even(xs.reshape(L * ns, D_MODEL), gpre,
                                                      w_main, w_g, w_gu, b_g)
            oa, nk, nv = _swa_decode(qa, kv, cache_swa_k[e].reshape(ns, WINDOW, LANES),
                                     cache_swa_v[e].reshape(ns, WINDOW, LANES), attn_sinks[e],
                                     ns=ns, L=L)
            ob, s1 = _gla_decode(qb, kb, vb, gb, rb, gn, state_gla[e].reshape(ns, QB_W, DV_B),
                                 ns=ns, L=L)
            mix_s = ((oa.reshape(1, L * ns, QA_W), 0), (ob.reshape(1, L * ns, VB_W), 0))
            ks_s.append(nk.reshape(ns, WINDOW, HKV_A, HD_A))
            vs_s.append(nv.reshape(ns, WINDOW, HKV_A, HD_A))
            gs_s.append(s1.reshape(ns, H_B, DK_B, DV_B))
        else:
            o = l // 2
            w_in = w_in_odd_bf[o]
            w_o = w_out_odd_bf[o]
            y, st = _odd_in(xp, jnp.zeros((nb, past_p, D_MODEL), F32), gpre, w_in, conv_w_odd[o],
                            tm=min(TM_ODD, S), rs=1)
            mix_p = ((y, 0), (y, 1))
            cs_p.append(st[:, past_p - 2:, :])
            y, st = _odd_in(xs, state_conv, gpre, w_in, conv_w_odd[o], tm=L * ns, rs=ns,
                            state_index=o)
            mix_s = ((y, 0), (y, 1))
            cs_s.append(st)

        gmix = gpost
        gpre, gpost = _row(norm_ffn_pre[l]), _row(norm_ffn_post[l])
        cb = _row(ffn_conv_b[l])
        xp, st = _ffn(xp, mix_p[0], mix_p[1], w_o, gmix, jnp.zeros((nb, past_p, F2), F32), gpre,
                      w_up_all, ffn_conv_w[l], cb, w_dn_all, gpost, l, tm=tm_p, rs=1)
        fs_p.append(st[:, past_p - 2:, :])
        xs, st = _ffn(xs, mix_s[0], mix_s[1], w_o, gmix, state_ffn, gpre, w_up_all, ffn_conv_w[l],
                      cb, w_dn_all, gpost, l, tm=L * ns, rs=ns, seq_major=True)
        fs_s.append(st)

    y_sample = xs.reshape(L, ns, D_MODEL).transpose(1, 0, 2)
    return (xp, y_sample, jnp.stack(ks_p), jnp.stack(vs_p), jnp.stack(gs_p), jnp.stack(cs_p),
            jnp.stack(fs_p), jnp.stack(ks_s), jnp.stack(vs_s), jnp.stack(gs_s), jnp.stack(cs_s),
            jnp.stack(fs_s))
```

```python
import functools

import numpy as np
import jax
import jax.numpy as jnp
from jax import lax
from jax.experimental import pallas as pl
from jax.experimental.pallas import tpu as pltpu

F32 = jnp.float32
BF16 = jnp.bfloat16

D_MODEL = 1024
WINDOW = 128
HD_A = 64
HQ_A = 8
HKV_A = 2
G_A = HQ_A // HKV_A
H_B = 4
DK_B = 64
DV_B = 128
GATE_RANK = 16
GATE_TAU = 16.0
GLA_CHUNK = 64
D_FF = ((8 * D_MODEL // 3 + 127) // 128) * 128
F2 = 2 * D_FF
EPS = 1e-6
GELU_C = float(np.sqrt(2.0 / np.pi))
GELU_A = 0.044715

QA_W = HQ_A * HD_A
KV_W = 2 * HKV_A * HD_A
QB_W = H_B * DK_B
VB_W = H_B * DV_B
MAIN_W = QA_W + KV_W + 2 * QB_W + 2 * VB_W

LANES = 128
SUBLANES = 8
VMEM_LIMIT = 56 * 1024 * 1024

TM_TOK = 512
TM_INPROJ = 1024
TM_ODD = 1024
TM_ATTN = 1024
GLA_GROUP = 128
ODD_TC = 256
FFN_TF = 256
FFN_ELEM_DTYPE = BF16
SEQ_BLOCK = 32
DEC_UNROLL = 16


def _cparams(*sem):
    return pltpu.CompilerParams(dimension_semantics=sem, vmem_limit_bytes=VMEM_LIMIT)


def _rms(x, g):
    return x * lax.rsqrt(jnp.mean(x * x, axis=-1, keepdims=True) + EPS) * g


def _dot(a, b):
    return jnp.dot(a, b, preferred_element_type=F32)


def _dot_nt(a, b):
    return lax.dot_general(a, b, (((1,), (1,)), ((), ())), preferred_element_type=F32)


def _dot_tn(a, b):
    return lax.dot_general(a, b, (((0,), (0,)), ((), ())), preferred_element_type=F32)


def _split3(x):
    hi = x.astype(BF16)
    r1 = x - hi.astype(F32)
    mid = r1.astype(BF16)
    lo = (r1 - mid.astype(F32)).astype(BF16)
    return hi, mid, lo


def _inproj_even_kernel(x_ref, gpre_ref, w_ref, wg_ref, wgu_ref, bg_ref,
                        qa_ref, kv_ref, qb_ref, kb_ref, vb_ref, rb_ref, gb_ref):
    h = _rms(x_ref[...], gpre_ref[...]).astype(BF16)

    def mm(lo, width):
        return _dot(h, w_ref[:, lo:lo + width])

    lo = 0
    qa_ref[...] = (mm(lo, QA_W) * (HD_A ** -0.5)).astype(BF16)
    lo += QA_W
    kv_ref[...] = mm(lo, KV_W)
    lo += KV_W
    qb_ref[...] = mm(lo, QB_W) * (DK_B ** -0.5)
    lo += QB_W
    kb_ref[...] = mm(lo, QB_W)
    lo += QB_W
    vb_ref[...] = mm(lo, VB_W).astype(BF16)
    lo += VB_W
    rb_ref[...] = mm(lo, VB_W)
    glr = _dot(h, wg_ref[...]).astype(BF16)
    z = _dot(glr, wgu_ref[...]) + bg_ref[...]
    gb_ref[...] = (jnp.minimum(z, 0.0) - jnp.log1p(jnp.exp(-jnp.abs(z)))) * (1.0 / GATE_TAU)


def _inproj_even(x2d, gpre, w_main, w_g, w_gu, b_g):
    T = x2d.shape[0]
    tm = min(TM_INPROJ, T)
    row = lambda w: pl.BlockSpec((tm, w), lambda i: (i, 0))
    full = lambda a: pl.BlockSpec(a.shape, lambda i: (0,) * a.ndim)
    out_shape = (
        jax.ShapeDtypeStruct((T, QA_W), BF16), jax.ShapeDtypeStruct((T, KV_W), F32),
        jax.ShapeDtypeStruct((T, QB_W), F32), jax.ShapeDtypeStruct((T, QB_W), F32),
        jax.ShapeDtypeStruct((T, VB_W), BF16), jax.ShapeDtypeStruct((T, VB_W), F32),
        jax.ShapeDtypeStruct((T, QB_W), F32))
    return pl.pallas_call(
        _inproj_even_kernel, grid=(T // tm,),
        in_specs=[row(D_MODEL), full(gpre), full(w_main), full(w_g), full(w_gu), full(b_g)],
        out_specs=(row(QA_W), row(KV_W), row(QB_W), row(QB_W), row(VB_W), row(VB_W), row(QB_W)),
        out_shape=out_shape, compiler_params=_cparams("parallel"), name="inproj_even",
    )(x2d, gpre, w_main, w_g, w_gu, b_g)


def _conv3_from_buf(buf, cw, cur, tm, past, rs, dtype=F32):
    cw = cw.astype(dtype)
    y = cw[0:1, :] * buf[past - 2 * rs:past - 2 * rs + tm, :].astype(dtype)
    y = y + cw[1:2, :] * buf[past - rs:past - rs + tm, :].astype(dtype)
    return y + cw[2:3, :] * cur.astype(dtype)


def _load_history(carry, past_ref, rs, seq_major):
    if seq_major:
        carry[0:rs, :] = past_ref[:, 0, :]
        carry[rs:2 * rs, :] = past_ref[:, 1, :]
    else:
        carry[...] = past_ref[0]


def _store_history(st_ref, carry, rs, seq_major):
    if seq_major:
        st_ref[:, 0, :] = carry[0:rs, :]
        st_ref[:, 1, :] = carry[rs:2 * rs, :]
    else:
        st_ref[0] = carry[...]


def _history_specs(past0, index, channels, seq_major):
    if seq_major:
        ns = past0.shape[1]
        return (2 * ns,
                pl.BlockSpec((None, ns, 2, channels), lambda b, j: (index, 0, 0, 0)),
                pl.BlockSpec((ns, 2, channels), lambda b, j: (0, 0, 0)),
                jax.ShapeDtypeStruct((ns, 2, channels), F32))
    past = past0.shape[1]
    spec = pl.BlockSpec((1, past, channels), lambda b, j: (b, 0, 0))
    return past, spec, spec, jax.ShapeDtypeStruct((past0.shape[0], past, channels), F32)


def _odd_in_kernel(x_ref, past_ref, gpre_ref, win_ref, cw_ref, y_ref, st_ref, carry, buf,
                   *, tm, past, rs, tc, seq_major):
    @pl.when(pl.program_id(1) == 0)
    def _():
        _load_history(carry, past_ref, rs, seq_major)

    h = _rms(x_ref[0], gpre_ref[...]).astype(BF16)
    for c in range(D_MODEL // tc):
        lo = c * tc
        bg = _dot(h, win_ref[:, lo:lo + tc])
        cu = _dot(h, win_ref[:, D_MODEL + lo:D_MODEL + lo + tc]) * \
            _dot(h, win_ref[:, 2 * D_MODEL + lo:2 * D_MODEL + lo + tc])
        buf[0:past, :] = carry[:, lo:lo + tc]
        buf[past:past + tm, :] = cu
        carry[:, lo:lo + tc] = buf[tm:tm + past, :]
        z = _conv3_from_buf(buf, cw_ref[:, lo:lo + tc], cu, tm, past, rs)
        y_ref[0, :, lo:lo + tc] = (bg * z).astype(BF16)
    _store_history(st_ref, carry, rs, seq_major)


def _odd_in(x3d, past0, gpre, w_in, conv_w, *, tm, rs, state_index=None):
    nb, rows, _ = x3d.shape
    seq_major = state_index is not None
    past, past_spec, st_spec, st_shape = _history_specs(past0, state_index, D_MODEL, seq_major)
    tc = ODD_TC
    kern = functools.partial(_odd_in_kernel, tm=tm, past=past, rs=rs, tc=tc, seq_major=seq_major)
    full = lambda a: pl.BlockSpec(a.shape, lambda b, j: (0,) * a.ndim)
    return pl.pallas_call(
        kern, grid=(nb, rows // tm),
        in_specs=[pl.BlockSpec((1, tm, D_MODEL), lambda b, j: (b, j, 0)), past_spec,
                  full(gpre), full(w_in), full(conv_w)],
        out_specs=(pl.BlockSpec((1, tm, D_MODEL), lambda b, j: (b, j, 0)), st_spec),
        out_shape=(jax.ShapeDtypeStruct((nb, rows, D_MODEL), BF16), st_shape),
        scratch_shapes=[pltpu.VMEM((past, D_MODEL), F32), pltpu.VMEM((past + tm, tc), F32)],
        compiler_params=_cparams("parallel", "arbitrary"), name="odd_in",
    )(x3d, past0, gpre, w_in, conv_w)


def _ffn_kernel(x_ref, ya_ref, yb_ref, wo_ref, gmix_ref, past_ref, gpre_ref, wup_ref, cw_ref,
                cb_ref, wdn_ref, gpost_ref, gk_ref, o_ref, st_ref,
                carry, buf_g, buf_v, acc, h_scr, act, x1_scr, *, tm, past, rs, tf, seq_major):
    @pl.when(pl.program_id(1) == 0)
    def _():
        _load_history(carry, past_ref, rs, seq_major)

    mix = _dot(ya_ref[0], wo_ref[0]) + _dot(yb_ref[0], wo_ref[1])
    x1_scr[...] = x_ref[0] + _rms(mix, gmix_ref[...])
    h_scr[...] = _rms(x1_scr[...], gpre_ref[...]).astype(BF16)
    nf = D_FF // tf

    def up(c):
        for lo, buf in ((c * tf, buf_g), (D_FF + c * tf, buf_v)):
            buf[c % 2, 0:past, :] = carry[:, lo:lo + tf]
            buf[c % 2, past:past + tm, :] = _dot(h_scr[...], wup_ref[:, lo:lo + tf])
            carry[:, lo:lo + tf] = buf[c % 2, tm:tm + past, :]

    gelu_c = gk_ref[0:1, 0:1].astype(FFN_ELEM_DTYPE)
    gelu_ca = gk_ref[0:1, 1:2].astype(FFN_ELEM_DTYPE)

    def elem(c):
        halves = []
        for lo, buf, scale in ((c * tf, buf_g, 1.0), (D_FF + c * tf, buf_v, 0.5)):
            b = buf.at[c % 2]
            halves.append(_conv3_from_buf(b, cw_ref[:, lo:lo + tf] * scale, b[past:past + tm, :],
                                          tm, past, rs, FFN_ELEM_DTYPE)
                          + (cb_ref[:, lo:lo + tf] * scale).astype(FFN_ELEM_DTYPE))
        x, half_v = halves
        t = jnp.tanh(x * (x * x * gelu_ca + gelu_c))
        act[c % 2] = ((x * t + x) * half_v).astype(BF16)

    def down(c):
        part = _dot(act[c % 2], wdn_ref[c * tf:(c + 1) * tf, :])
        if c == 0:
            acc[...] = part
        else:
            acc[...] += part

    for c in range(nf + 2):
        if c < nf:
            up(c)
        if 1 <= c <= nf:
            elem(c - 1)
        if c >= 2:
            down(c - 2)
    o_ref[0] = x1_scr[...] + _rms(acc[...], gpost_ref[...])
    _store_history(st_ref, carry, rs, seq_major)


def _ffn(x3d, ya, yb, w_o, gmix, past0, gpre, w_up, conv_w, conv_b, w_dn, gpost, layer, *, tm, rs,
         seq_major=False):
    nb, rows, _ = x3d.shape
    past, past_spec, st_spec, st_shape = _history_specs(past0, layer, F2, seq_major)
    half = w_o.shape[0] // 2
    kern = functools.partial(_ffn_kernel, tm=tm, past=past, rs=rs, tf=FFN_TF, seq_major=seq_major)
    gelu_k = jnp.zeros((1, LANES), F32).at[0, 0].set(GELU_C).at[0, 1].set(GELU_C * GELU_A)
    w_o2 = w_o.reshape(2, half, D_MODEL)
    once = dict(pipeline_mode=pl.Buffered(1))
    full = lambda a: pl.BlockSpec(a.shape, lambda b, j: (0,) * a.ndim, **once)
    per_layer = lambda a: pl.BlockSpec((None,) + a.shape[1:], lambda b, j: (layer, 0, 0), **once)
    tok = lambda w, cblk=0: pl.BlockSpec((1, tm, w), lambda b, j: (b, j, cblk))
    return pl.pallas_call(
        kern, grid=(nb, rows // tm),
        in_specs=[tok(D_MODEL), tok(half, ya[1]), tok(half, yb[1]), full(w_o2), full(gmix),
                  past_spec,
                  full(gpre), per_layer(w_up), full(conv_w), full(conv_b), per_layer(w_dn),
                  full(gpost), full(gelu_k)],
        out_specs=(tok(D_MODEL), st_spec),
        out_shape=(jax.ShapeDtypeStruct((nb, rows, D_MODEL), F32), st_shape),
        scratch_shapes=[pltpu.VMEM((past, F2), F32), pltpu.VMEM((2, past + tm, FFN_TF), F32),
                        pltpu.VMEM((2, past + tm, FFN_TF), F32), pltpu.VMEM((tm, D_MODEL), F32),
                        pltpu.VMEM((tm, D_MODEL), BF16), pltpu.VMEM((2, tm, FFN_TF), BF16),
                        pltpu.VMEM((tm, D_MODEL), F32)],
        compiler_params=_cparams("parallel", "arbitrary"), name="conv_ffn",
    )(x3d, ya[0], yb[0], w_o2, gmix, past0, gpre, w_up, conv_w, conv_b, w_dn, gpost, gelu_k)


def _swa_bias_prompt():
    qi = np.arange(WINDOW)[:, None]
    sj = np.arange(2 * WINDOW)[None, :]
    dist = WINDOW + qi - sj
    valid = (dist >= 0) & (dist <= WINDOW)
    slopes = 2.0 ** (-8.0 * np.arange(1, HQ_A + 1) / HQ_A)
    bias = np.where(valid[None], -slopes[:, None, None] * dist[None].astype(np.float64), -np.inf)
    return jnp.asarray(bias, dtype=F32)


def _swa_prompt_kernel(sink_ref, q_ref, kvc_ref, kvp_ref, bias_ref, o_ref, *, tq):
    i = pl.program_id(1)
    kvc = kvc_ref[0]
    kvp = kvp_ref[0]
    kcat = jnp.concatenate([kvp[:, 0:LANES], kvc[:, 0:LANES]], axis=0).astype(BF16)
    vcat = jnp.concatenate([kvp[:, LANES:], kvc[:, LANES:]], axis=0).astype(BF16)
    lane = lax.broadcasted_iota(jnp.int32, (WINDOW, LANES), 1)
    col = lax.broadcasted_iota(jnp.int32, (WINDOW, 2 * WINDOW), 1)
    for j in range(tq // WINDOW):
        keys = kcat[j * WINDOW:(j + 2) * WINDOW, :]
        vals = vcat[j * WINDOW:(j + 2) * WINDOW, :]
        for g in range(G_A):
            q2 = q_ref[0, j * WINDOW:(j + 1) * WINDOW, g * LANES:(g + 1) * LANES]
            outs = []
            for kv in range(HKV_A):
                hq = kv * G_A + g
                in_head = (lane < HD_A) if kv == 0 else (lane >= HD_A)
                qm = jnp.where(in_head, q2, jnp.zeros_like(q2))
                s = _dot_nt(qm, keys) + bias_ref[hq]
                if j == 0:
                    s = jnp.where(jnp.logical_and(i == 0, col < WINDOW), -jnp.inf, s)
                sink = sink_ref[hq]
                m = jnp.maximum(jnp.max(s, axis=-1, keepdims=True), sink)
                p = jnp.exp(s - m)
                den = jnp.sum(p, axis=-1, keepdims=True) + jnp.exp(sink - m)
                outs.append(_dot(p.astype(BF16), vals) / den)
            o2 = jnp.where(lane < HD_A, outs[0], outs[1])
            o_ref[0, j * WINDOW:(j + 1) * WINDOW, g * LANES:(g + 1) * LANES] = o2.astype(BF16)


GLA_LEVELS = (0, 1, 2, 4, 8, 16, 32)
LOG2E = 1.4426950408889634


def _gla_constants():
    r = np.arange(GLA_GROUP)
    i, j = r[:, None], r[None, :]
    ranges = [(i >= j) & (i // GLA_CHUNK == j // GLA_CHUNK)]
    masks = [i == j]
    for s in GLA_LEVELS[1:]:
        mid = (i // (2 * s)) * (2 * s) + s
        upper = (i % (2 * s) >= s) & (j >= mid) & (j <= i)
        lower = (i % (2 * s) < s) & (j > i) & (j < mid)
        ranges.append(upper | lower)
        masks.append((i // (2 * s) == j // (2 * s)) & (i % (2 * s) >= s) & (j % (2 * s) < s))
    ones_bd = (np.arange(QB_W)[:, None] // DK_B) == (np.arange(VB_W)[None, :] // DV_B)
    return (jnp.asarray(np.concatenate(ranges, axis=0), dtype=BF16),
            jnp.asarray(ones_bd, dtype=BF16), jnp.asarray(np.stack(masks), dtype=F32))


def _head_norm_gate(o, r, gn):
    parts = []
    for h in range(H_B):
        oh = o[:, h * DV_B:(h + 1) * DV_B]
        parts.append(oh * lax.rsqrt(jnp.mean(oh * oh, axis=-1, keepdims=True) + EPS))
    return jnp.concatenate(parts, axis=-1) * gn * jax.nn.silu(r)


def _gla_prompt_kernel(q_ref, k_ref, v_ref, g_ref, r_ref, gn_ref, rng_ref, lvl_ref,
                       o_ref, st_ref, state, b_scr, o_scr, qk_scr, *, tm):
    @pl.when(pl.program_id(1) == 0)
    def _():
        state[...] = jnp.zeros_like(state)

    q = q_ref[0]
    k = k_ref[0]
    v = v_ref[0]
    lane = lax.broadcasted_iota(jnp.int32, (1, LANES), 1)
    head_lanes = (lane < DK_B, lane >= DK_B)
    head_cols = lax.broadcasted_iota(jnp.int32, (1, QB_W), 1) // DK_B

    g2 = g_ref[0] * LOG2E
    ghi = g2.astype(BF16)
    glo = (g2 - ghi.astype(F32)).astype(BF16)
    nlev = len(GLA_LEVELS)
    for gi in range(tm // GLA_GROUP):
        rows = slice(gi * GLA_GROUP, (gi + 1) * GLA_GROUP)
        d_all = _dot(rng_ref[...], ghi[rows]) + _dot(rng_ref[...], glo[rows])
        b_scr[rows, :] = d_all[0:GLA_GROUP]
        qk_scr[0, 0, rows, :] = q[rows].astype(BF16)
        qk_scr[0, 1, rows, :] = k[rows].astype(BF16)
        for li in range(1, nlev):
            e = jnp.exp2(d_all[li * GLA_GROUP:(li + 1) * GLA_GROUP])
            qk_scr[li, 0, rows, :] = (q[rows] * e).astype(BF16)
            qk_scr[li, 1, rows, :] = (k[rows] * e).astype(BF16)
    b = b_scr[...]

    for gi in range(tm // GLA_GROUP):
        rows = slice(gi * GLA_GROUP, (gi + 1) * GLA_GROUP)
        for pi in range(H_B // 2):
            pr = slice(pi * LANES, (pi + 1) * LANES)
            a = None
            for li in range(nlev):
                ke = qk_scr[li, 1, rows, pr]
                kem = jnp.concatenate([jnp.where(sel, ke, jnp.zeros_like(ke)) for sel in head_lanes],
                                      axis=0)
                term = _dot_nt(qk_scr[li, 0, rows, pr], kem)
                keep = lvl_ref[li] > 0.5
                keep = jnp.concatenate([keep, keep], axis=1)
                a = jnp.where(keep, term, 0.0 if a is None else a)
            a = a.astype(BF16)
            for par in range(2):
                cols = slice((2 * pi + par) * DV_B, (2 * pi + par + 1) * DV_B)
                o_scr[rows, cols] = _dot(a[:, par * GLA_GROUP:(par + 1) * GLA_GROUP], v[rows, cols])

    for c in range(tm // GLA_CHUNK):
        rows = slice(c * GLA_CHUNK, (c + 1) * GLA_CHUNK)
        bc = b[rows]
        b_end = bc[GLA_CHUNK - 1:GLA_CHUNK, :]
        qe = (q[rows] * jnp.exp2(bc)).astype(BF16)
        ke = (k[rows] * jnp.exp2(b_end - bc)).astype(BF16)
        st = state[...]
        qm = jnp.concatenate([jnp.where(head_cols == h, qe, jnp.zeros_like(qe))
                              for h in range(H_B)], axis=0)
        km = jnp.concatenate([jnp.where(head_cols == h, ke, jnp.zeros_like(ke))
                              for h in range(H_B)], axis=0)
        vm = jnp.concatenate([v[rows, h * DV_B:(h + 1) * DV_B] for h in range(H_B)], axis=0)
        oi = _dot_nt(qm, st.astype(BF16))
        for h in range(H_B):
            o_scr[rows, h * DV_B:(h + 1) * DV_B] += oi[h * GLA_CHUNK:(h + 1) * GLA_CHUNK]
        state[...] = st * jnp.exp2(b_end) + _dot_tn(vm, km)

    o_ref[0] = _head_norm_gate(o_scr[...], r_ref[0], gn_ref[...]).astype(BF16)
    st_ref[0] = state[...]


N_ATTN_IN = 13


def _attn_prompt_kernel(*refs, tm, n_cast):
    (sink_ref, qa_ref, kvc_ref, kvp_ref, bias_ref,
     q_ref, k_ref, v_ref, g_ref, r_ref, gn_ref, rng_ref, lvl_ref) = refs[:N_ATTN_IN]
    cast_in = refs[N_ATTN_IN:N_ATTN_IN + n_cast]
    oa_ref, ob_ref, st_ref = refs[N_ATTN_IN + n_cast:N_ATTN_IN + n_cast + 3]
    cast_out = refs[N_ATTN_IN + n_cast + 3:N_ATTN_IN + 2 * n_cast + 3]
    state, b_scr, o_scr, qk_scr = refs[N_ATTN_IN + 2 * n_cast + 3:]
    _gla_prompt_kernel(q_ref, k_ref, v_ref, g_ref, r_ref, gn_ref, rng_ref, lvl_ref,
                       ob_ref, st_ref, state, b_scr, o_scr, qk_scr, tm=tm)
    _swa_prompt_kernel(sink_ref, qa_ref, kvc_ref, kvp_ref, bias_ref, oa_ref, tq=tm)
    for src, dst in zip(cast_in, cast_out):
        dst[...] = src[...].astype(BF16)


def _attn_prompt(qa3d, kv3d, sinks, q3d, k3d, v3d, g3d, r3d, gn, to_bf16):
    nb, S, _ = q3d.shape
    tm = min(TM_ATTN, S)
    nj = S // tm
    nsteps = nb * nj
    bias = _swa_bias_prompt()
    rng, _, lvl = _gla_constants()
    kern = functools.partial(_attn_prompt_kernel, tm=tm, n_cast=len(to_bf16))
    blocks_per_tile = tm // WINDOW
    tok = lambda w: pl.BlockSpec((1, tm, w), lambda b, j: (b, j, 0))
    full = lambda a: pl.BlockSpec(a.shape, lambda b, j: (0,) * a.ndim)
    prev_kv = pl.BlockSpec((1, WINDOW, KV_W),
                           lambda b, j: (b, jnp.maximum(j * blocks_per_tile - 1, 0), 0))

    def slab(a):
        rows = a.shape[0] // nsteps
        assert rows * nsteps == a.shape[0] and rows % (2 * SUBLANES) == 0, a.shape
        return pl.BlockSpec((rows, a.shape[1]), lambda b, j: (b * nj + j, 0))

    outs = pl.pallas_call(
        kern, grid=(nb, nj),
        in_specs=[pl.BlockSpec(memory_space=pltpu.SMEM), tok(QA_W), tok(KV_W), prev_kv, full(bias),
                  tok(QB_W), tok(QB_W), tok(VB_W), tok(QB_W), tok(VB_W),
                  full(gn), full(rng), full(lvl)] + [slab(a) for a in to_bf16],
        out_specs=[tok(QA_W), tok(VB_W), pl.BlockSpec((1, DV_B, QB_W), lambda b, j: (b, 0, 0))]
        + [slab(a) for a in to_bf16],
        out_shape=[jax.ShapeDtypeStruct((nb, S, QA_W), BF16),
                   jax.ShapeDtypeStruct((nb, S, VB_W), BF16),
                   jax.ShapeDtypeStruct((nb, DV_B, QB_W), F32)]
        + [jax.ShapeDtypeStruct(a.shape, BF16) for a in to_bf16],
        scratch_shapes=[pltpu.VMEM((DV_B, QB_W), F32), pltpu.VMEM((tm, QB_W), F32),
                        pltpu.VMEM((tm, VB_W), F32),
                        pltpu.VMEM((len(GLA_LEVELS), 2, tm, QB_W), BF16)],
        compiler_params=_cparams("parallel", "arbitrary"), name="attn_prompt",
    )(sinks, qa3d, kv3d, kv3d, bias, q3d, k3d, v3d, g3d, r3d, gn, rng, lvl, *to_bf16)
    return outs[0], outs[1], outs[2], outs[3:]


def _swa_decode_constants(L):
    rows = HQ_A * L
    hq = np.arange(rows)[:, None] // L
    t = np.arange(rows)[:, None] % L
    s = np.arange(WINDOW + SUBLANES)[None, :]
    dist = WINDOW + t - s
    valid = (dist >= 0) & (dist <= WINDOW) & (s < WINDOW + L)
    slopes = 2.0 ** (-8.0 * (hq + 1) / HQ_A)
    bias = np.where(valid, -slopes * dist, -np.inf)
    place = np.zeros((4 * SUBLANES, WINDOW), np.float32)
    for part in range(3):
        for tt in range(L):
            place[part * SUBLANES + tt, WINDOW - L + tt] = 1.0
    return (jnp.asarray(bias[:, :WINDOW], dtype=F32), jnp.asarray(bias[:, WINDOW:], dtype=F32),
            jnp.asarray(place, dtype=BF16))


def _swa_decode_kernel(q_ref, kvn_ref, kt_ref, vt_ref, bias_c_ref, bias_n_ref, place_ref, sink_ref,
                       o_ref, nkt_ref, nvt_ref, qs, os_, kn, vn, *, ns, L, sb):
    i = pl.program_id(0)
    rows_q = HQ_A * L
    lane = lax.broadcasted_iota(jnp.int32, (1, LANES), 1)

    @pl.when(i == 0)
    def _():
        q = q_ref[...].astype(F32)
        for kv in range(HKV_A):
            sel = (lane < HD_A) if kv == 0 else (lane >= HD_A)
            for g in range(G_A):
                base = (kv * G_A + g) * L * ns
                qs[base:base + L * ns, :] = jnp.where(sel, q[:, g * LANES:(g + 1) * LANES], 0.0)
        kn[...] = jnp.zeros_like(kn)
        vn[...] = jnp.zeros_like(vn)
        kn[0:L * ns, :] = kvn_ref[:, 0:LANES]
        vn[0:L * ns, :] = kvn_ref[:, LANES:]

    bias_c = bias_c_ref[...]
    bias_n = bias_n_ref[...]
    sink = sink_ref[...]
    place = place_ref[...]
    zero8 = jnp.zeros((SUBLANES, LANES), F32)

    def shifted_cache(old, new8):
        hi = new8.astype(BF16).astype(F32)
        mid = (new8 - hi).astype(BF16).astype(F32)
        lo = new8 - hi - mid
        parts = jnp.concatenate([hi, mid, lo, zero8], axis=0).astype(BF16)
        placed = _dot_tn(parts, place)
        return jnp.where(lane >= WINDOW - L, placed, pltpu.roll(old, WINDOW - L, axis=1))

    def body(t, carry):
        lhs, k_new, v_new = [], [], []
        for u in range(DEC_UNROLL):
            s = t * DEC_UNROLL + u
            seq = i * sb + s
            k8 = kn[pl.ds(seq, SUBLANES, stride=ns), :]
            v8 = vn[pl.ds(seq, SUBLANES, stride=ns), :]
            nkt_ref[s] = shifted_cache(kt_ref[s], k8)
            nvt_ref[s] = shifted_cache(vt_ref[s], v8)
            lhs.append(qs[pl.ds(seq, rows_q, stride=ns), :].astype(BF16))
            k_new.append(k8.astype(BF16))
            v_new.append(v8.astype(BF16))
        lhs, k_new, v_new = jnp.stack(lhs), jnp.stack(k_new), jnp.stack(v_new)
        blk = pl.ds(pl.multiple_of(t * DEC_UNROLL, DEC_UNROLL), DEC_UNROLL)
        kt = kt_ref[blk].astype(BF16)
        vt = vt_ref[blk].astype(BF16)
        sc_c = jnp.einsum("uqd,udw->uqw", lhs, kt, preferred_element_type=F32) + bias_c
        sc_n = jnp.einsum("uqd,utd->uqt", lhs, k_new, preferred_element_type=F32) + bias_n
        m = jnp.maximum(jnp.maximum(jnp.max(sc_c, axis=-1, keepdims=True),
                                    jnp.max(sc_n, axis=-1, keepdims=True)), sink)
        p_c = jnp.exp(sc_c - m)
        p_n = jnp.exp(sc_n - m)
        den = (jnp.sum(p_c, axis=-1, keepdims=True) + jnp.sum(p_n, axis=-1, keepdims=True)
               + jnp.exp(sink - m))
        res = (jnp.einsum("uqw,udw->uqd", p_c.astype(BF16), vt, preferred_element_type=F32)
               + jnp.einsum("uqt,utd->uqd", p_n.astype(BF16), v_new,
                            preferred_element_type=F32)) / den
        half = rows_q // 2
        for u in range(DEC_UNROLL):
            seq = i * sb + t * DEC_UNROLL + u
            os_[pl.ds(seq, half, stride=ns), :] = jnp.where(lane < HD_A, res[u, 0:half],
                                                            res[u, half:])
        return carry

    lax.fori_loop(0, sb // DEC_UNROLL, body, 0)

    @pl.when(i == pl.num_programs(0) - 1)
    def _():
        for g in range(G_A):
            o_ref[:, g * LANES:(g + 1) * LANES] = os_[g * L * ns:(g + 1) * L * ns, :].astype(BF16)


def _swa_decode(q_tm, kvn_tm, cache_kt, cache_vt, sinks, *, ns, L):
    sb = min(SEQ_BLOCK, ns)
    assert sb % DEC_UNROLL == 0 and L <= SUBLANES
    bias_c, bias_n, place = _swa_decode_constants(L)
    sink_col = jnp.broadcast_to(jnp.repeat(sinks.astype(F32), L)[:, None], (HQ_A * L, 1))
    kern = functools.partial(_swa_decode_kernel, ns=ns, L=L, sb=sb)
    full = lambda a: pl.BlockSpec(a.shape, lambda i: (0,) * a.ndim)
    cache = pl.BlockSpec((sb, LANES, WINDOW), lambda i: (i, 0, 0))
    return pl.pallas_call(
        kern, grid=(ns // sb,),
        in_specs=[full(q_tm), full(kvn_tm), cache, cache, full(bias_c), full(bias_n), full(place),
                  full(sink_col)],
        out_specs=(pl.BlockSpec((L * ns, QA_W), lambda i: (0, 0)), cache, cache),
        out_shape=(jax.ShapeDtypeStruct((L * ns, QA_W), BF16),
                   jax.ShapeDtypeStruct(cache_kt.shape, F32),
                   jax.ShapeDtypeStruct(cache_vt.shape, F32)),
        scratch_shapes=[pltpu.VMEM((HQ_A * L * ns, LANES), F32),
                        pltpu.VMEM((G_A * L * ns, LANES), F32),
                        pltpu.VMEM((SUBLANES * ns, LANES), F32),
                        pltpu.VMEM((SUBLANES * ns, LANES), F32)],
        compiler_params=_cparams("arbitrary"), name="swa_decode",
    )(q_tm, kvn_tm, cache_kt, cache_vt, bias_c, bias_n, place, sink_col)


def _gla_decode_kernel(q_ref, k_ref, v_ref, g_ref, r_ref, gn_ref, ones_ref, s0_ref,
                       o_ref, s1_ref, qe2, ke2, v2, dec3, oi2, od, *, ns, L, sb):
    i = pl.program_id(0)
    lane = lax.broadcasted_iota(jnp.int32, (1, LANES), 1)
    npair = H_B // 2

    @pl.when(i == 0)
    def _():
        slab = lambda a, t: a[t * ns:(t + 1) * ns, :]
        q, k, g = q_ref[...], k_ref[...], g_ref[...]
        vf = v_ref[...].astype(F32)
        b = [slab(g, 0)]
        for t in range(1, L):
            b.append(b[-1] + slab(g, t))
        for t in range(L):
            acc = jnp.zeros((ns, VB_W), F32)
            for jj in range(t + 1):
                p = (slab(q, t) * slab(k, jj) * jnp.exp(b[t] - b[jj])).astype(BF16)
                acc = acc + _dot(p, ones_ref[...]) * slab(vf, jj)
            od[t * ns:(t + 1) * ns, :] = acc
            qe = slab(q, t) * jnp.exp(b[t])
            ke = slab(k, t) * jnp.exp(b[L - 1] - b[t])
            for par in range(2):
                sel = (lane < DK_B) if par == 0 else (lane >= DK_B)
                base = (par * L + t) * ns
                for pi in range(npair):
                    pr = slice(pi * LANES, (pi + 1) * LANES)
                    qe2[pi, base:base + ns, :] = jnp.where(sel, qe[:, pr], 0.0)
                    ke2[pi, base:base + ns, :] = jnp.where(sel, ke[:, pr], 0.0)
                    h = 2 * pi + par
                    v2[pi, base:base + ns, :] = slab(vf, t)[:, h * DV_B:(h + 1) * DV_B]
        dec3[...] = jnp.zeros_like(dec3)
        hi, mid, lo = _split3(jnp.exp(b[L - 1]))
        for pi in range(npair):
            pr = slice(pi * LANES, (pi + 1) * LANES)
            dec3[pi, 0:ns, :] = hi[:, pr].astype(F32)
            dec3[pi, ns:2 * ns, :] = mid[:, pr].astype(F32)
            dec3[pi, 2 * ns:3 * ns, :] = lo[:, pr].astype(F32)

    ones8 = jnp.ones((2 * L, LANES), BF16)

    def one_seq(s):
        seq = i * sb + s
        for pi in range(npair):
            take = lambda ref: ref[pi, pl.ds(seq, 2 * L, stride=ns), :].astype(BF16)
            st_p = s0_ref[s, pi * LANES:(pi + 1) * LANES, :]
            oi2[pi, pl.ds(seq, 2 * L, stride=ns), :] = _dot(take(qe2), st_p.astype(BF16))
            dcol = _dot_tn(take(dec3), ones8)
            upd = _dot_tn(take(ke2), take(v2))
            s1_ref[s, pi * LANES:(pi + 1) * LANES, :] = dcol * st_p + upd

    def body(t, carry):
        for u in range(DEC_UNROLL):
            one_seq(t * DEC_UNROLL + u)
        return carry

    lax.fori_loop(0, sb // DEC_UNROLL, body, 0)

    @pl.when(i == pl.num_programs(0) - 1)
    def _():
        for t in range(L):
            parts = []
            for h in range(H_B):
                base = ((h % 2) * L + t) * ns
                parts.append(oi2[h // 2, base:base + ns, :])
            o = jnp.concatenate(parts, axis=-1) + od[t * ns:(t + 1) * ns, :]
            o_ref[t * ns:(t + 1) * ns, :] = _head_norm_gate(
                o, r_ref[t * ns:(t + 1) * ns, :], gn_ref[...]).astype(BF16)


def _gla_decode(q_tm, k_tm, v_tm, g_tm, r_tm, gn, s0, *, ns, L):
    sb = min(SEQ_BLOCK, ns)
    assert 2 * L == SUBLANES
    _, ones_bd, _ = _gla_constants()
    kern = functools.partial(_gla_decode_kernel, ns=ns, L=L, sb=sb)
    full = lambda a: pl.BlockSpec(a.shape, lambda i: (0,) * a.ndim)
    st = pl.BlockSpec((sb, QB_W, DV_B), lambda i: (i, 0, 0))
    rows2 = 2 * L * ns
    return pl.pallas_call(
        kern, grid=(ns // sb,),
        in_specs=[full(q_tm), full(k_tm), full(v_tm), full(g_tm), full(r_tm), full(gn),
                  full(ones_bd), st],
        out_specs=(pl.BlockSpec((L * ns, VB_W), lambda i: (0, 0)), st),
        out_shape=(jax.ShapeDtypeStruct((L * ns, VB_W), BF16),
                   jax.ShapeDtypeStruct(s0.shape, F32)),
        scratch_shapes=[pltpu.VMEM((H_B // 2, rows2, LANES), F32) for _ in range(5)]
        + [pltpu.VMEM((L * ns, VB_W), F32)],
        compiler_params=_cparams("arbitrary"), name="gla_decode",
    )(q_tm, k_tm, v_tm, g_tm, r_tm, gn, ones_bd, s0)


def _qa_perm():
    return np.asarray([(kv * G_A + g) * HD_A + d
                       for g in range(G_A) for kv in range(HKV_A) for d in range(HD_A)])


def _prep_even(w_in, w_gate_up, b_gate, w_out):
    perm = _qa_perm()
    w_main = jnp.concatenate([w_in[:, :QA_W][:, perm], w_in[:, QA_W:MAIN_W]], axis=1).astype(BF16)
    w_g = jnp.pad(w_in[:, MAIN_W:], ((0, 0), (0, LANES - GATE_RANK))).astype(BF16)
    w_gu = jnp.pad(w_gate_up, ((0, LANES - GATE_RANK), (0, 0))).astype(BF16)
    w_o = jnp.concatenate([w_out[:QA_W][perm], w_out[QA_W:]], axis=0).astype(BF16)
    return w_main, w_g, w_gu, b_gate.reshape(1, -1), w_o


def _row(v):
    return v.reshape(1, -1)


def kernel(x_prompt, x_sample, cache_swa_k, cache_swa_v, state_gla, state_conv, state_ffn,
           norm_mix_pre, norm_mix_post, norm_ffn_pre, norm_ffn_post, w_in_even, w_gate_up, b_gate,
           attn_sinks, gla_norm, w_out_even, w_in_odd, conv_w_odd, w_out_odd, ffn_up, ffn_conv_w,
           ffn_conv_b, ffn_down):
    nb, S, _ = x_prompt.shape
    ns, L, _ = x_sample.shape
    depth = norm_mix_pre.shape[0]

    xp = x_prompt
    xs = x_sample.transpose(1, 0, 2).reshape(1, L * ns, D_MODEL)
    past_p = max(SUBLANES, 2)
    tm_p = min(TM_TOK, S)

    ks_p, vs_p, gs_p, cs_p, fs_p = [], [], [], [], []
    ks_s, vs_s, gs_s, cs_s, fs_s = [], [], [], [], []

    for l in range(depth):
        gpre, gpost = _row(norm_mix_pre[l]), _row(norm_mix_post[l])
        if l % 2 == 0:
            e = l // 2
            w_main, w_g, w_gu, b_g, w_o = _prep_even(w_in_even[e], w_gate_up[e], b_gate[e],
                                                     w_out_even[e])
            gn = _row(gla_norm[e])
            qa, kv, qb, kb, vb, rb, gb = _inproj_even(xp.reshape(nb * S, D_MODEL), gpre,
                                                      w_main, w_g, w_gu, b_g)
            r3 = lambda a: a.reshape(nb, S, a.shape[-1])
            to_bf16 = []
            if e == 0:
                to_bf16 = [ffn_up.reshape(-1, F2), ffn_down.reshape(-1, D_MODEL)]
                if w_in_odd.shape[0]:
                    to_bf16 += [w_in_odd.reshape(-1, 3 * D_MODEL), w_out_odd.reshape(-1, D_MODEL)]
            oa, ob, st_t, narrowed = _attn_prompt(r3(qa), r3(kv), attn_sinks[e], r3(qb), r3(kb),
                                                  r3(vb), r3(gb), r3(rb), gn, to_bf16)
            if e == 0:
                w_up_all = narrowed[0].reshape(ffn_up.shape)
                w_dn_all = narrowed[1].reshape(ffn_down.shape)
                if w_in_odd.shape[0]:
                    w_in_odd_bf = narrowed[2].reshape(w_in_odd.shape)
                    w_out_odd_bf = narrowed[3].reshape(w_out_odd.shape)
            mix_p = ((oa, 0), (ob, 0))
            kv_last = r3(kv)[:, S - WINDOW:, :]
            ks_p.append(kv_last[..., :LANES].reshape(nb, WINDOW, HKV_A, HD_A))
            vs_p.append(kv_last[..., LANES:].reshape(nb, WINDOW, HKV_A, HD_A))
            gs_p.append(st_t.transpose(0, 2, 1).reshape(nb, H_B, DK_B, DV_B))
            qa, kv, qb, kb, vb, rb, gb = _inproj_even(xs.reshape(L * ns, D_MODEL), gpre,
                                                      w_main, w_g, w_gu, b_g)
            feat_major = lambda c: c.transpose(0, 2, 3, 1).reshape(ns, LANES, WINDOW)
            win_major = lambda c: c.reshape(ns, HKV_A, HD_A, WINDOW).transpose(0, 3, 1, 2)
            oa, nkt, nvt = _swa_decode(qa, kv, feat_major(cache_swa_k[e]),
                                       feat_major(cache_swa_v[e]), attn_sinks[e], ns=ns, L=L)
            nk, nv = win_major(nkt), win_major(nvt)
            ob, s1 = _gla_decode(qb, kb, vb, gb, rb, gn, state_gla[e].reshape(ns, QB_W, DV_B),
                                 ns=ns, L=L)
            mix_s = ((oa.reshape(1, L * ns, QA_W), 0), (ob.reshape(1, L * ns, VB_W), 0))
            ks_s.append(nk.reshape(ns, WINDOW, HKV_A, HD_A))
            vs_s.append(nv.reshape(ns, WINDOW, HKV_A, HD_A))
            gs_s.append(s1.reshape(ns, H_B, DK_B, DV_B))
        else:
            o = l // 2
            w_in = w_in_odd_bf[o]
            w_o = w_out_odd_bf[o]
            y, st = _odd_in(xp, jnp.zeros((nb, past_p, D_MODEL), F32), gpre, w_in, conv_w_odd[o],
                            tm=min(TM_ODD, S), rs=1)
            mix_p = ((y, 0), (y, 1))
            cs_p.append(st[:, past_p - 2:, :])
            y, st = _odd_in(xs, state_conv, gpre, w_in, conv_w_odd[o], tm=L * ns, rs=ns,
                            state_index=o)
            mix_s = ((y, 0), (y, 1))
            cs_s.append(st)

        gmix = gpost
        gpre, gpost = _row(norm_ffn_pre[l]), _row(norm_ffn_post[l])
        cb = _row(ffn_conv_b[l])
        xp, st = _ffn(xp, mix_p[0], mix_p[1], w_o, gmix, jnp.zeros((nb, past_p, F2), F32), gpre,
                      w_up_all, ffn_conv_w[l], cb, w_dn_all, gpost, l, tm=tm_p, rs=1)
        fs_p.append(st[:, past_p - 2:, :])
        xs, st = _ffn(xs, mix_s[0], mix_s[1], w_o, gmix, state_ffn, gpre, w_up_all, ffn_conv_w[l],
                      cb, w_dn_all, gpost, l, tm=L * ns, rs=ns, seq_major=True)
        fs_s.append(st)

    y_sample = xs.reshape(L, ns, D_MODEL).transpose(1, 0, 2)
    return (xp, y_sample, jnp.stack(ks_p), jnp.stack(vs_p), jnp.stack(gs_p), jnp.stack(cs_p),
            jnp.stack(fs_p), jnp.stack(ks_s), jnp.stack(vs_s), jnp.stack(gs_s), jnp.stack(cs_s),
            jnp.stack(fs_s))
```

```python
import functools

import numpy as np
import jax
import jax.numpy as jnp
from jax import lax
from jax.experimental import pallas as pl
from jax.experimental.pallas import tpu as pltpu

F32 = jnp.float32
BF16 = jnp.bfloat16

D_MODEL = 1024
WINDOW = 128
HD_A = 64
HQ_A = 8
HKV_A = 2
G_A = HQ_A // HKV_A
H_B = 4
DK_B = 64
DV_B = 128
GATE_RANK = 16
GATE_TAU = 16.0
GLA_CHUNK = 64
D_FF = ((8 * D_MODEL // 3 + 127) // 128) * 128
F2 = 2 * D_FF
EPS = 1e-6
GELU_C = float(np.sqrt(2.0 / np.pi))
GELU_A = 0.044715

QA_W = HQ_A * HD_A
KV_W = 2 * HKV_A * HD_A
QB_W = H_B * DK_B
VB_W = H_B * DV_B
MAIN_W = QA_W + KV_W + 2 * QB_W + 2 * VB_W

LANES = 128
SUBLANES = 8
VMEM_LIMIT = 56 * 1024 * 1024

TM_TOK = 512
TM_INPROJ = 1024
TM_ODD = 1024
TM_ATTN = 1024
GLA_GROUP = 128
ODD_TC = 256
FFN_TF = 256
FFN_ELEM_DTYPE = BF16
SEQ_BLOCK = 32
DEC_UNROLL = 16


def _cparams(*sem):
    return pltpu.CompilerParams(dimension_semantics=sem, vmem_limit_bytes=VMEM_LIMIT)


def _rms(x, g):
    return x * lax.rsqrt(jnp.mean(x * x, axis=-1, keepdims=True) + EPS) * g


def _dot(a, b):
    return jnp.dot(a, b, preferred_element_type=F32)


def _dot_nt(a, b):
    return lax.dot_general(a, b, (((1,), (1,)), ((), ())), preferred_element_type=F32)


def _dot_tn(a, b):
    return lax.dot_general(a, b, (((0,), (0,)), ((), ())), preferred_element_type=F32)


def _split3(x):
    hi = x.astype(BF16)
    r1 = x - hi.astype(F32)
    mid = r1.astype(BF16)
    lo = (r1 - mid.astype(F32)).astype(BF16)
    return hi, mid, lo


def _inproj_even_kernel(x_ref, gpre_ref, w_ref, wg_ref, wgu_ref, bg_ref,
                        qa_ref, kv_ref, qb_ref, kb_ref, vb_ref, rb_ref, gb_ref):
    h = _rms(x_ref[...], gpre_ref[...]).astype(BF16)

    def mm(lo, width):
        return _dot(h, w_ref[:, lo:lo + width])

    lo = 0
    qa_ref[...] = (mm(lo, QA_W) * (HD_A ** -0.5)).astype(BF16)
    lo += QA_W
    kv_ref[...] = mm(lo, KV_W)
    lo += KV_W
    qb_ref[...] = mm(lo, QB_W) * (DK_B ** -0.5)
    lo += QB_W
    kb_ref[...] = mm(lo, QB_W)
    lo += QB_W
    vb_ref[...] = mm(lo, VB_W).astype(BF16)
    lo += VB_W
    rb_ref[...] = mm(lo, VB_W)
    glr = _dot(h, wg_ref[...]).astype(BF16)
    z = _dot(glr, wgu_ref[...]) + bg_ref[...]
    gb_ref[...] = (jnp.minimum(z, 0.0) - jnp.log1p(jnp.exp(-jnp.abs(z)))) * (1.0 / GATE_TAU)


def _inproj_even(x2d, gpre, w_main, w_g, w_gu, b_g):
    T = x2d.shape[0]
    tm = min(TM_INPROJ, T)
    row = lambda w: pl.BlockSpec((tm, w), lambda i: (i, 0))
    full = lambda a: pl.BlockSpec(a.shape, lambda i: (0,) * a.ndim)
    out_shape = (
        jax.ShapeDtypeStruct((T, QA_W), BF16), jax.ShapeDtypeStruct((T, KV_W), F32),
        jax.ShapeDtypeStruct((T, QB_W), F32), jax.ShapeDtypeStruct((T, QB_W), F32),
        jax.ShapeDtypeStruct((T, VB_W), BF16), jax.ShapeDtypeStruct((T, VB_W), F32),
        jax.ShapeDtypeStruct((T, QB_W), F32))
    return pl.pallas_call(
        _inproj_even_kernel, grid=(T // tm,),
        in_specs=[row(D_MODEL), full(gpre), full(w_main), full(w_g), full(w_gu), full(b_g)],
        out_specs=(row(QA_W), row(KV_W), row(QB_W), row(QB_W), row(VB_W), row(VB_W), row(QB_W)),
        out_shape=out_shape, compiler_params=_cparams("parallel"), name="inproj_even",
    )(x2d, gpre, w_main, w_g, w_gu, b_g)


def _conv3_from_buf(buf, cw, cur, tm, past, rs, dtype=F32):
    cw = cw.astype(dtype)
    y = cw[0:1, :] * buf[past - 2 * rs:past - 2 * rs + tm, :].astype(dtype)
    y = y + cw[1:2, :] * buf[past - rs:past - rs + tm, :].astype(dtype)
    return y + cw[2:3, :] * cur.astype(dtype)


def _load_history(carry, past_ref, rs, seq_major):
    if seq_major:
        carry[0:rs, :] = past_ref[:, 0, :]
        carry[rs:2 * rs, :] = past_ref[:, 1, :]
    else:
        carry[...] = past_ref[0]


def _store_history(st_ref, carry, rs, seq_major):
    if seq_major:
        st_ref[:, 0, :] = carry[0:rs, :]
        st_ref[:, 1, :] = carry[rs:2 * rs, :]
    else:
        st_ref[0] = carry[...]


def _history_specs(past0, index, channels, seq_major):
    if seq_major:
        ns = past0.shape[1]
        return (2 * ns,
                pl.BlockSpec((None, ns, 2, channels), lambda b, j: (index, 0, 0, 0)),
                pl.BlockSpec((ns, 2, channels), lambda b, j: (0, 0, 0)),
                jax.ShapeDtypeStruct((ns, 2, channels), F32))
    past = past0.shape[1]
    spec = pl.BlockSpec((1, past, channels), lambda b, j: (b, 0, 0))
    return past, spec, spec, jax.ShapeDtypeStruct((past0.shape[0], past, channels), F32)


def _odd_in_kernel(x_ref, past_ref, gpre_ref, win_ref, cw_ref, y_ref, st_ref, carry, buf,
                   *, tm, past, rs, tc, seq_major):
    @pl.when(pl.program_id(1) == 0)
    def _():
        _load_history(carry, past_ref, rs, seq_major)

    h = _rms(x_ref[0], gpre_ref[...]).astype(BF16)
    for c in range(D_MODEL // tc):
        lo = c * tc
        bg = _dot(h, win_ref[:, lo:lo + tc])
        cu = _dot(h, win_ref[:, D_MODEL + lo:D_MODEL + lo + tc]) * \
            _dot(h, win_ref[:, 2 * D_MODEL + lo:2 * D_MODEL + lo + tc])
        buf[0:past, :] = carry[:, lo:lo + tc]
        buf[past:past + tm, :] = cu
        carry[:, lo:lo + tc] = buf[tm:tm + past, :]
        z = _conv3_from_buf(buf, cw_ref[:, lo:lo + tc], cu, tm, past, rs)
        y_ref[0, :, lo:lo + tc] = (bg * z).astype(BF16)
    _store_history(st_ref, carry, rs, seq_major)


def _odd_in(x3d, past0, gpre, w_in, conv_w, *, tm, rs, state_index=None):
    nb, rows, _ = x3d.shape
    seq_major = state_index is not None
    past, past_spec, st_spec, st_shape = _history_specs(past0, state_index, D_MODEL, seq_major)
    tc = ODD_TC
    kern = functools.partial(_odd_in_kernel, tm=tm, past=past, rs=rs, tc=tc, seq_major=seq_major)
    full = lambda a: pl.BlockSpec(a.shape, lambda b, j: (0,) * a.ndim)
    return pl.pallas_call(
        kern, grid=(nb, rows // tm),
        in_specs=[pl.BlockSpec((1, tm, D_MODEL), lambda b, j: (b, j, 0)), past_spec,
                  full(gpre), full(w_in), full(conv_w)],
        out_specs=(pl.BlockSpec((1, tm, D_MODEL), lambda b, j: (b, j, 0)), st_spec),
        out_shape=(jax.ShapeDtypeStruct((nb, rows, D_MODEL), BF16), st_shape),
        scratch_shapes=[pltpu.VMEM((past, D_MODEL), F32), pltpu.VMEM((past + tm, tc), F32)],
        compiler_params=_cparams("parallel", "arbitrary"), name="odd_in",
    )(x3d, past0, gpre, w_in, conv_w)


def _ffn_kernel(x_ref, ya_ref, yb_ref, wo_ref, gmix_ref, past_ref, gpre_ref, wup_ref, cw_ref,
                cb_ref, wdn_ref, gpost_ref, gk_ref, o_ref, st_ref,
                carry, buf_g, buf_v, acc, h_scr, act, x1_scr, *, tm, past, rs, tf, seq_major):
    @pl.when(pl.program_id(1) == 0)
    def _():
        _load_history(carry, past_ref, rs, seq_major)

    mix = _dot(ya_ref[0], wo_ref[0]) + _dot(yb_ref[0], wo_ref[1])
    x1_scr[...] = x_ref[0] + _rms(mix, gmix_ref[...])
    h_scr[...] = _rms(x1_scr[...], gpre_ref[...]).astype(BF16)
    nf = D_FF // tf

    def up(c):
        for lo, buf in ((c * tf, buf_g), (D_FF + c * tf, buf_v)):
            buf[c % 2, 0:past, :] = carry[:, lo:lo + tf]
            buf[c % 2, past:past + tm, :] = _dot(h_scr[...], wup_ref[:, lo:lo + tf])
            carry[:, lo:lo + tf] = buf[c % 2, tm:tm + past, :]

    gelu_c = gk_ref[0:1, 0:1].astype(FFN_ELEM_DTYPE)
    gelu_ca = gk_ref[0:1, 1:2].astype(FFN_ELEM_DTYPE)

    def elem(c):
        halves = []
        for lo, buf, scale in ((c * tf, buf_g, 1.0), (D_FF + c * tf, buf_v, 0.5)):
            b = buf.at[c % 2]
            halves.append(_conv3_from_buf(b, cw_ref[:, lo:lo + tf] * scale, b[past:past + tm, :],
                                          tm, past, rs, FFN_ELEM_DTYPE)
                          + (cb_ref[:, lo:lo + tf] * scale).astype(FFN_ELEM_DTYPE))
        x, half_v = halves
        t = jnp.tanh(x * (x * x * gelu_ca + gelu_c))
        act[c % 2] = ((x * t + x) * half_v).astype(BF16)

    def down(c):
        part = _dot(act[c % 2], wdn_ref[c * tf:(c + 1) * tf, :])
        if c == 0:
            acc[...] = part
        else:
            acc[...] += part

    for c in range(nf + 2):
        if c < nf:
            up(c)
        if 1 <= c <= nf:
            elem(c - 1)
        if c >= 2:
            down(c - 2)
    o_ref[0] = x1_scr[...] + _rms(acc[...], gpost_ref[...])
    _store_history(st_ref, carry, rs, seq_major)


N_FFN_IN = 13


def _ffn(x3d, ya, yb, w_o, gmix, past0, gpre, w_up, conv_w, conv_b, w_dn, gpost, layer, *, tm, rs,
         state_out=None):
    nb, rows, _ = x3d.shape
    seq_major = state_out is not None
    past, past_spec, st_spec, st_shape = _history_specs(past0, layer, F2, seq_major)
    half = w_o.shape[0] // 2
    body = functools.partial(_ffn_kernel, tm=tm, past=past, rs=rs, tf=FFN_TF, seq_major=seq_major)
    extra_in, extra_specs, aliases = [], [], {}
    kern = body
    if seq_major:
        st_spec = pl.BlockSpec((None,) + st_shape.shape, lambda b, j: (layer, 0, 0, 0))
        st_shape = jax.ShapeDtypeStruct(state_out.shape, F32)
        extra_in, extra_specs = [state_out], [pl.BlockSpec(memory_space=pl.ANY)]
        aliases = {N_FFN_IN: 1}
        kern = lambda *refs: body(*refs[:N_FFN_IN], *refs[N_FFN_IN + 1:])
    gelu_k = jnp.zeros((1, LANES), F32).at[0, 0].set(GELU_C).at[0, 1].set(GELU_C * GELU_A)
    w_o2 = w_o.reshape(2, half, D_MODEL)
    once = dict(pipeline_mode=pl.Buffered(1))
    full = lambda a: pl.BlockSpec(a.shape, lambda b, j: (0,) * a.ndim, **once)
    per_layer = lambda a: pl.BlockSpec((None,) + a.shape[1:], lambda b, j: (layer, 0, 0), **once)
    tok = lambda w, cblk=0: pl.BlockSpec((1, tm, w), lambda b, j: (b, j, cblk))
    return pl.pallas_call(
        kern, grid=(nb, rows // tm),
        in_specs=[tok(D_MODEL), tok(half, ya[1]), tok(half, yb[1]), full(w_o2), full(gmix),
                  past_spec,
                  full(gpre), per_layer(w_up), full(conv_w), full(conv_b), per_layer(w_dn),
                  full(gpost), full(gelu_k)] + extra_specs,
        input_output_aliases=aliases,
        out_specs=(tok(D_MODEL), st_spec),
        out_shape=(jax.ShapeDtypeStruct((nb, rows, D_MODEL), F32), st_shape),
        scratch_shapes=[pltpu.VMEM((past, F2), F32), pltpu.VMEM((2, past + tm, FFN_TF), F32),
                        pltpu.VMEM((2, past + tm, FFN_TF), F32), pltpu.VMEM((tm, D_MODEL), F32),
                        pltpu.VMEM((tm, D_MODEL), BF16), pltpu.VMEM((2, tm, FFN_TF), BF16),
                        pltpu.VMEM((tm, D_MODEL), F32)],
        compiler_params=_cparams("parallel", "arbitrary"), name="conv_ffn",
    )(x3d, ya[0], yb[0], w_o2, gmix, past0, gpre, w_up, conv_w, conv_b, w_dn, gpost, gelu_k,
      *extra_in)


def _swa_bias_prompt():
    qi = np.arange(WINDOW)[:, None]
    sj = np.arange(2 * WINDOW)[None, :]
    dist = WINDOW + qi - sj
    valid = (dist >= 0) & (dist <= WINDOW)
    slopes = 2.0 ** (-8.0 * np.arange(1, HQ_A + 1) / HQ_A)
    bias = np.where(valid[None], -slopes[:, None, None] * dist[None].astype(np.float64), -np.inf)
    return jnp.asarray(bias, dtype=F32)


def _swa_prompt_kernel(sink_ref, q_ref, kvc_ref, kvp_ref, bias_ref, o_ref, *, tq):
    i = pl.program_id(1)
    kvc = kvc_ref[0]
    kvp = kvp_ref[0]
    kcat = jnp.concatenate([kvp[:, 0:LANES], kvc[:, 0:LANES]], axis=0).astype(BF16)
    vcat = jnp.concatenate([kvp[:, LANES:], kvc[:, LANES:]], axis=0).astype(BF16)
    lane = lax.broadcasted_iota(jnp.int32, (WINDOW, LANES), 1)
    col = lax.broadcasted_iota(jnp.int32, (WINDOW, 2 * WINDOW), 1)
    for j in range(tq // WINDOW):
        keys = kcat[j * WINDOW:(j + 2) * WINDOW, :]
        vals = vcat[j * WINDOW:(j + 2) * WINDOW, :]
        for g in range(G_A):
            q2 = q_ref[0, j * WINDOW:(j + 1) * WINDOW, g * LANES:(g + 1) * LANES]
            outs = []
            for kv in range(HKV_A):
                hq = kv * G_A + g
                in_head = (lane < HD_A) if kv == 0 else (lane >= HD_A)
                qm = jnp.where(in_head, q2, jnp.zeros_like(q2))
                s = _dot_nt(qm, keys) + bias_ref[hq]
                if j == 0:
                    s = jnp.where(jnp.logical_and(i == 0, col < WINDOW), -jnp.inf, s)
                sink = sink_ref[hq]
                m = jnp.maximum(jnp.max(s, axis=-1, keepdims=True), sink)
                p = jnp.exp(s - m)
                den = jnp.sum(p, axis=-1, keepdims=True) + jnp.exp(sink - m)
                outs.append(_dot(p.astype(BF16), vals) / den)
            o2 = jnp.where(lane < HD_A, outs[0], outs[1])
            o_ref[0, j * WINDOW:(j + 1) * WINDOW, g * LANES:(g + 1) * LANES] = o2.astype(BF16)


GLA_LEVELS = (0, 1, 2, 4, 8, 16, 32)
LOG2E = 1.4426950408889634


def _gla_constants():
    r = np.arange(GLA_GROUP)
    i, j = r[:, None], r[None, :]
    ranges = [(i >= j) & (i // GLA_CHUNK == j // GLA_CHUNK)]
    masks = [i == j]
    for s in GLA_LEVELS[1:]:
        mid = (i // (2 * s)) * (2 * s) + s
        upper = (i % (2 * s) >= s) & (j >= mid) & (j <= i)
        lower = (i % (2 * s) < s) & (j > i) & (j < mid)
        ranges.append(upper | lower)
        masks.append((i // (2 * s) == j // (2 * s)) & (i % (2 * s) >= s) & (j % (2 * s) < s))
    ones_bd = (np.arange(QB_W)[:, None] // DK_B) == (np.arange(VB_W)[None, :] // DV_B)
    return (jnp.asarray(np.concatenate(ranges, axis=0), dtype=BF16),
            jnp.asarray(ones_bd, dtype=BF16), jnp.asarray(np.stack(masks), dtype=F32))


def _head_norm_gate(o, r, gn):
    parts = []
    for h in range(H_B):
        oh = o[:, h * DV_B:(h + 1) * DV_B]
        parts.append(oh * lax.rsqrt(jnp.mean(oh * oh, axis=-1, keepdims=True) + EPS))
    return jnp.concatenate(parts, axis=-1) * gn * jax.nn.silu(r)


def _gla_prompt_kernel(q_ref, k_ref, v_ref, g_ref, r_ref, gn_ref, rng_ref, lvl_ref,
                       o_ref, st_ref, state, b_scr, o_scr, qk_scr, *, tm):
    @pl.when(pl.program_id(1) == 0)
    def _():
        state[...] = jnp.zeros_like(state)

    q = q_ref[0]
    k = k_ref[0]
    v = v_ref[0]
    lane = lax.broadcasted_iota(jnp.int32, (1, LANES), 1)
    head_lanes = (lane < DK_B, lane >= DK_B)
    head_cols = lax.broadcasted_iota(jnp.int32, (1, QB_W), 1) // DK_B

    g2 = g_ref[0] * LOG2E
    ghi = g2.astype(BF16)
    glo = (g2 - ghi.astype(F32)).astype(BF16)
    nlev = len(GLA_LEVELS)
    for gi in range(tm // GLA_GROUP):
        rows = slice(gi * GLA_GROUP, (gi + 1) * GLA_GROUP)
        d_all = _dot(rng_ref[...], ghi[rows]) + _dot(rng_ref[...], glo[rows])
        b_scr[rows, :] = d_all[0:GLA_GROUP]
        qk_scr[0, 0, rows, :] = q[rows].astype(BF16)
        qk_scr[0, 1, rows, :] = k[rows].astype(BF16)
        for li in range(1, nlev):
            e = jnp.exp2(d_all[li * GLA_GROUP:(li + 1) * GLA_GROUP])
            qk_scr[li, 0, rows, :] = (q[rows] * e).astype(BF16)
            qk_scr[li, 1, rows, :] = (k[rows] * e).astype(BF16)
    b = b_scr[...]

    for gi in range(tm // GLA_GROUP):
        rows = slice(gi * GLA_GROUP, (gi + 1) * GLA_GROUP)
        for pi in range(H_B // 2):
            pr = slice(pi * LANES, (pi + 1) * LANES)
            a = None
            for li in range(nlev):
                ke = qk_scr[li, 1, rows, pr]
                kem = jnp.concatenate([jnp.where(sel, ke, jnp.zeros_like(ke)) for sel in head_lanes],
                                      axis=0)
                term = _dot_nt(qk_scr[li, 0, rows, pr], kem)
                keep = lvl_ref[li] > 0.5
                keep = jnp.concatenate([keep, keep], axis=1)
                a = jnp.where(keep, term, 0.0 if a is None else a)
            a = a.astype(BF16)
            for par in range(2):
                cols = slice((2 * pi + par) * DV_B, (2 * pi + par + 1) * DV_B)
                o_scr[rows, cols] = _dot(a[:, par * GLA_GROUP:(par + 1) * GLA_GROUP], v[rows, cols])

    for c in range(tm // GLA_CHUNK):
        rows = slice(c * GLA_CHUNK, (c + 1) * GLA_CHUNK)
        bc = b[rows]
        b_end = bc[GLA_CHUNK - 1:GLA_CHUNK, :]
        qe = (q[rows] * jnp.exp2(bc)).astype(BF16)
        ke = (k[rows] * jnp.exp2(b_end - bc)).astype(BF16)
        st = state[...]
        qm = jnp.concatenate([jnp.where(head_cols == h, qe, jnp.zeros_like(qe))
                              for h in range(H_B)], axis=0)
        km = jnp.concatenate([jnp.where(head_cols == h, ke, jnp.zeros_like(ke))
                              for h in range(H_B)], axis=0)
        vm = jnp.concatenate([v[rows, h * DV_B:(h + 1) * DV_B] for h in range(H_B)], axis=0)
        oi = _dot_nt(qm, st.astype(BF16))
        for h in range(H_B):
            o_scr[rows, h * DV_B:(h + 1) * DV_B] += oi[h * GLA_CHUNK:(h + 1) * GLA_CHUNK]
        state[...] = st * jnp.exp2(b_end) + _dot_tn(vm, km)

    o_ref[0] = _head_norm_gate(o_scr[...], r_ref[0], gn_ref[...]).astype(BF16)
    st_ref[0] = state[...]


N_ATTN_IN = 13


def _attn_prompt_kernel(*refs, tm, n_cast):
    (sink_ref, qa_ref, kvc_ref, kvp_ref, bias_ref,
     q_ref, k_ref, v_ref, g_ref, r_ref, gn_ref, rng_ref, lvl_ref) = refs[:N_ATTN_IN]
    cast_in = refs[N_ATTN_IN:N_ATTN_IN + n_cast]
    oa_ref, ob_ref, st_ref = refs[N_ATTN_IN + n_cast:N_ATTN_IN + n_cast + 3]
    cast_out = refs[N_ATTN_IN + n_cast + 3:N_ATTN_IN + 2 * n_cast + 3]
    state, b_scr, o_scr, qk_scr = refs[N_ATTN_IN + 2 * n_cast + 3:]
    _gla_prompt_kernel(q_ref, k_ref, v_ref, g_ref, r_ref, gn_ref, rng_ref, lvl_ref,
                       ob_ref, st_ref, state, b_scr, o_scr, qk_scr, tm=tm)
    _swa_prompt_kernel(sink_ref, qa_ref, kvc_ref, kvp_ref, bias_ref, oa_ref, tq=tm)
    for src, dst in zip(cast_in, cast_out):
        dst[...] = src[...].astype(BF16)


def _attn_prompt(qa3d, kv3d, sinks, q3d, k3d, v3d, g3d, r3d, gn, to_bf16):
    nb, S, _ = q3d.shape
    tm = min(TM_ATTN, S)
    nj = S // tm
    nsteps = nb * nj
    bias = _swa_bias_prompt()
    rng, _, lvl = _gla_constants()
    kern = functools.partial(_attn_prompt_kernel, tm=tm, n_cast=len(to_bf16))
    blocks_per_tile = tm // WINDOW
    tok = lambda w: pl.BlockSpec((1, tm, w), lambda b, j: (b, j, 0))
    full = lambda a: pl.BlockSpec(a.shape, lambda b, j: (0,) * a.ndim)
    prev_kv = pl.BlockSpec((1, WINDOW, KV_W),
                           lambda b, j: (b, jnp.maximum(j * blocks_per_tile - 1, 0), 0))

    def slab(a):
        rows = a.shape[0] // nsteps
        assert rows * nsteps == a.shape[0] and rows % (2 * SUBLANES) == 0, a.shape
        return pl.BlockSpec((rows, a.shape[1]), lambda b, j: (b * nj + j, 0))

    outs = pl.pallas_call(
        kern, grid=(nb, nj),
        in_specs=[pl.BlockSpec(memory_space=pltpu.SMEM), tok(QA_W), tok(KV_W), prev_kv, full(bias),
                  tok(QB_W), tok(QB_W), tok(VB_W), tok(QB_W), tok(VB_W),
                  full(gn), full(rng), full(lvl)] + [slab(a) for a in to_bf16],
        out_specs=[tok(QA_W), tok(VB_W), pl.BlockSpec((1, DV_B, QB_W), lambda b, j: (b, 0, 0))]
        + [slab(a) for a in to_bf16],
        out_shape=[jax.ShapeDtypeStruct((nb, S, QA_W), BF16),
                   jax.ShapeDtypeStruct((nb, S, VB_W), BF16),
                   jax.ShapeDtypeStruct((nb, DV_B, QB_W), F32)]
        + [jax.ShapeDtypeStruct(a.shape, BF16) for a in to_bf16],
        scratch_shapes=[pltpu.VMEM((DV_B, QB_W), F32), pltpu.VMEM((tm, QB_W), F32),
                        pltpu.VMEM((tm, VB_W), F32),
                        pltpu.VMEM((len(GLA_LEVELS), 2, tm, QB_W), BF16)],
        compiler_params=_cparams("parallel", "arbitrary"), name="attn_prompt",
    )(sinks, qa3d, kv3d, kv3d, bias, q3d, k3d, v3d, g3d, r3d, gn, rng, lvl, *to_bf16)
    return outs[0], outs[1], outs[2], outs[3:]


def _swa_decode_constants(L):
    rows = HQ_A * L
    hq = np.arange(rows)[:, None] // L
    t = np.arange(rows)[:, None] % L
    s = np.arange(WINDOW + SUBLANES)[None, :]
    dist = WINDOW + t - s
    valid = (dist >= 0) & (dist <= WINDOW) & (s < WINDOW + L)
    slopes = 2.0 ** (-8.0 * (hq + 1) / HQ_A)
    bias = np.where(valid, -slopes * dist, -np.inf)
    place = np.zeros((4 * SUBLANES, WINDOW), np.float32)
    for part in range(3):
        for tt in range(L):
            place[part * SUBLANES + tt, WINDOW - L + tt] = 1.0
    return (jnp.asarray(bias[:, :WINDOW], dtype=F32), jnp.asarray(bias[:, WINDOW:], dtype=F32),
            jnp.asarray(place, dtype=BF16))


def _swa_decode_kernel(q_ref, kvn_ref, kt_ref, vt_ref, bias_c_ref, bias_n_ref, place_ref, sink_ref,
                       o_ref, nkt_ref, nvt_ref, qs, os_, kn, vn, *, ns, L, sb):
    i = pl.program_id(0)
    rows_q = HQ_A * L
    lane = lax.broadcasted_iota(jnp.int32, (1, LANES), 1)

    @pl.when(i == 0)
    def _():
        q = q_ref[...].astype(F32)
        for kv in range(HKV_A):
            sel = (lane < HD_A) if kv == 0 else (lane >= HD_A)
            for g in range(G_A):
                base = (kv * G_A + g) * L * ns
                qs[base:base + L * ns, :] = jnp.where(sel, q[:, g * LANES:(g + 1) * LANES], 0.0)
        kn[...] = jnp.zeros_like(kn)
        vn[...] = jnp.zeros_like(vn)
        kn[0:L * ns, :] = kvn_ref[:, 0:LANES]
        vn[0:L * ns, :] = kvn_ref[:, LANES:]

    bias_c = bias_c_ref[...]
    bias_n = bias_n_ref[...]
    sink = sink_ref[...]
    place = place_ref[...]
    zero8 = jnp.zeros((SUBLANES, LANES), F32)

    def shifted_cache(old, new8):
        hi = new8.astype(BF16).astype(F32)
        mid = (new8 - hi).astype(BF16).astype(F32)
        lo = new8 - hi - mid
        parts = jnp.concatenate([hi, mid, lo, zero8], axis=0).astype(BF16)
        placed = _dot_tn(parts, place)
        return jnp.where(lane >= WINDOW - L, placed, pltpu.roll(old, WINDOW - L, axis=1))

    def body(t, carry):
        lhs, k_new, v_new = [], [], []
        for u in range(DEC_UNROLL):
            s = t * DEC_UNROLL + u
            seq = i * sb + s
            k8 = kn[pl.ds(seq, SUBLANES, stride=ns), :]
            v8 = vn[pl.ds(seq, SUBLANES, stride=ns), :]
            nkt_ref[s] = shifted_cache(kt_ref[s], k8)
            nvt_ref[s] = shifted_cache(vt_ref[s], v8)
            lhs.append(qs[pl.ds(seq, rows_q, stride=ns), :].astype(BF16))
            k_new.append(k8.astype(BF16))
            v_new.append(v8.astype(BF16))
        lhs, k_new, v_new = jnp.stack(lhs), jnp.stack(k_new), jnp.stack(v_new)
        blk = pl.ds(pl.multiple_of(t * DEC_UNROLL, DEC_UNROLL), DEC_UNROLL)
        kt = kt_ref[blk].astype(BF16)
        vt = vt_ref[blk].astype(BF16)
        sc_c = jnp.einsum("uqd,udw->uqw", lhs, kt, preferred_element_type=F32) + bias_c
        sc_n = jnp.einsum("uqd,utd->uqt", lhs, k_new, preferred_element_type=F32) + bias_n
        m = jnp.maximum(jnp.maximum(jnp.max(sc_c, axis=-1, keepdims=True),
                                    jnp.max(sc_n, axis=-1, keepdims=True)), sink)
        p_c = jnp.exp(sc_c - m)
        p_n = jnp.exp(sc_n - m)
        den = (jnp.sum(p_c, axis=-1, keepdims=True) + jnp.sum(p_n, axis=-1, keepdims=True)
               + jnp.exp(sink - m))
        res = (jnp.einsum("uqw,udw->uqd", p_c.astype(BF16), vt, preferred_element_type=F32)
               + jnp.einsum("uqt,utd->uqd", p_n.astype(BF16), v_new,
                            preferred_element_type=F32)) / den
        half = rows_q // 2
        for u in range(DEC_UNROLL):
            seq = i * sb + t * DEC_UNROLL + u
            os_[pl.ds(seq, half, stride=ns), :] = jnp.where(lane < HD_A, res[u, 0:half],
                                                            res[u, half:])
        return carry

    lax.fori_loop(0, sb // DEC_UNROLL, body, 0)

    @pl.when(i == pl.num_programs(0) - 1)
    def _():
        for g in range(G_A):
            o_ref[:, g * LANES:(g + 1) * LANES] = os_[g * L * ns:(g + 1) * L * ns, :].astype(BF16)


def _swa_decode(q_tm, kvn_tm, cache_kt, cache_vt, sinks, *, ns, L):
    sb = min(SEQ_BLOCK, ns)
    assert sb % DEC_UNROLL == 0 and L <= SUBLANES
    bias_c, bias_n, place = _swa_decode_constants(L)
    sink_col = jnp.broadcast_to(jnp.repeat(sinks.astype(F32), L)[:, None], (HQ_A * L, 1))
    kern = functools.partial(_swa_decode_kernel, ns=ns, L=L, sb=sb)
    full = lambda a: pl.BlockSpec(a.shape, lambda i: (0,) * a.ndim)
    cache = pl.BlockSpec((sb, LANES, WINDOW), lambda i: (i, 0, 0))
    return pl.pallas_call(
        kern, grid=(ns // sb,),
        in_specs=[full(q_tm), full(kvn_tm), cache, cache, full(bias_c), full(bias_n), full(place),
                  full(sink_col)],
        out_specs=(pl.BlockSpec((L * ns, QA_W), lambda i: (0, 0)), cache, cache),
        out_shape=(jax.ShapeDtypeStruct((L * ns, QA_W), BF16),
                   jax.ShapeDtypeStruct(cache_kt.shape, F32),
                   jax.ShapeDtypeStruct(cache_vt.shape, F32)),
        scratch_shapes=[pltpu.VMEM((HQ_A * L * ns, LANES), F32),
                        pltpu.VMEM((G_A * L * ns, LANES), F32),
                        pltpu.VMEM((SUBLANES * ns, LANES), F32),
                        pltpu.VMEM((SUBLANES * ns, LANES), F32)],
        compiler_params=_cparams("arbitrary"), name="swa_decode",
    )(q_tm, kvn_tm, cache_kt, cache_vt, bias_c, bias_n, place, sink_col)


def _gla_decode_kernel(q_ref, k_ref, v_ref, g_ref, r_ref, gn_ref, ones_ref, s0_ref,
                       o_ref, s1_ref, qe2, ke2, v2, dec3, oi2, od, *, ns, L, sb):
    i = pl.program_id(0)
    lane = lax.broadcasted_iota(jnp.int32, (1, LANES), 1)
    npair = H_B // 2

    @pl.when(i == 0)
    def _():
        slab = lambda a, t: a[t * ns:(t + 1) * ns, :]
        q, k, g = q_ref[...], k_ref[...], g_ref[...]
        vf = v_ref[...].astype(F32)
        b = [slab(g, 0)]
        for t in range(1, L):
            b.append(b[-1] + slab(g, t))
        for t in range(L):
            acc = jnp.zeros((ns, VB_W), F32)
            for jj in range(t + 1):
                p = (slab(q, t) * slab(k, jj) * jnp.exp(b[t] - b[jj])).astype(BF16)
                acc = acc + _dot(p, ones_ref[...]) * slab(vf, jj)
            od[t * ns:(t + 1) * ns, :] = acc
            qe = slab(q, t) * jnp.exp(b[t])
            ke = slab(k, t) * jnp.exp(b[L - 1] - b[t])
            for par in range(2):
                sel = (lane < DK_B) if par == 0 else (lane >= DK_B)
                base = (par * L + t) * ns
                for pi in range(npair):
                    pr = slice(pi * LANES, (pi + 1) * LANES)
                    qe2[pi, base:base + ns, :] = jnp.where(sel, qe[:, pr], 0.0)
                    ke2[pi, base:base + ns, :] = jnp.where(sel, ke[:, pr], 0.0)
                    h = 2 * pi + par
                    v2[pi, base:base + ns, :] = slab(vf, t)[:, h * DV_B:(h + 1) * DV_B]
        dec3[...] = jnp.zeros_like(dec3)
        hi, mid, lo = _split3(jnp.exp(b[L - 1]))
        for pi in range(npair):
            pr = slice(pi * LANES, (pi + 1) * LANES)
            dec3[pi, 0:ns, :] = hi[:, pr].astype(F32)
            dec3[pi, ns:2 * ns, :] = mid[:, pr].astype(F32)
            dec3[pi, 2 * ns:3 * ns, :] = lo[:, pr].astype(F32)

    ones8 = jnp.ones((2 * L, LANES), BF16)

    def one_seq(s):
        seq = i * sb + s
        for pi in range(npair):
            take = lambda ref: ref[pi, pl.ds(seq, 2 * L, stride=ns), :].astype(BF16)
            st_p = s0_ref[s, pi * LANES:(pi + 1) * LANES, :]
            oi2[pi, pl.ds(seq, 2 * L, stride=ns), :] = _dot(take(qe2), st_p.astype(BF16))
            dcol = _dot_tn(take(dec3), ones8)
            upd = _dot_tn(take(ke2), take(v2))
            s1_ref[s, pi * LANES:(pi + 1) * LANES, :] = dcol * st_p + upd

    def body(t, carry):
        for u in range(DEC_UNROLL):
            one_seq(t * DEC_UNROLL + u)
        return carry

    lax.fori_loop(0, sb // DEC_UNROLL, body, 0)

    @pl.when(i == pl.num_programs(0) - 1)
    def _():
        for t in range(L):
            parts = []
            for h in range(H_B):
                base = ((h % 2) * L + t) * ns
                parts.append(oi2[h // 2, base:base + ns, :])
            o = jnp.concatenate(parts, axis=-1) + od[t * ns:(t + 1) * ns, :]
            o_ref[t * ns:(t + 1) * ns, :] = _head_norm_gate(
                o, r_ref[t * ns:(t + 1) * ns, :], gn_ref[...]).astype(BF16)


def _gla_decode(q_tm, k_tm, v_tm, g_tm, r_tm, gn, s0, *, ns, L):
    sb = min(SEQ_BLOCK, ns)
    assert 2 * L == SUBLANES
    _, ones_bd, _ = _gla_constants()
    kern = functools.partial(_gla_decode_kernel, ns=ns, L=L, sb=sb)
    full = lambda a: pl.BlockSpec(a.shape, lambda i: (0,) * a.ndim)
    st = pl.BlockSpec((sb, QB_W, DV_B), lambda i: (i, 0, 0))
    rows2 = 2 * L * ns
    return pl.pallas_call(
        kern, grid=(ns // sb,),
        in_specs=[full(q_tm), full(k_tm), full(v_tm), full(g_tm), full(r_tm), full(gn),
                  full(ones_bd), st],
        out_specs=(pl.BlockSpec((L * ns, VB_W), lambda i: (0, 0)), st),
        out_shape=(jax.ShapeDtypeStruct((L * ns, VB_W), BF16),
                   jax.ShapeDtypeStruct(s0.shape, F32)),
        scratch_shapes=[pltpu.VMEM((H_B // 2, rows2, LANES), F32) for _ in range(5)]
        + [pltpu.VMEM((L * ns, VB_W), F32)],
        compiler_params=_cparams("arbitrary"), name="gla_decode",
    )(q_tm, k_tm, v_tm, g_tm, r_tm, gn, ones_bd, s0)


def _qa_perm():
    return np.asarray([(kv * G_A + g) * HD_A + d
                       for g in range(G_A) for kv in range(HKV_A) for d in range(HD_A)])


def _prep_even(w_in, w_gate_up, b_gate, w_out):
    perm = _qa_perm()
    w_main = jnp.concatenate([w_in[:, :QA_W][:, perm], w_in[:, QA_W:MAIN_W]], axis=1).astype(BF16)
    w_g = jnp.pad(w_in[:, MAIN_W:], ((0, 0), (0, LANES - GATE_RANK))).astype(BF16)
    w_gu = jnp.pad(w_gate_up, ((0, LANES - GATE_RANK), (0, 0))).astype(BF16)
    w_o = jnp.concatenate([w_out[:QA_W][perm], w_out[QA_W:]], axis=0).astype(BF16)
    return w_main, w_g, w_gu, b_gate.reshape(1, -1), w_o


def _row(v):
    return v.reshape(1, -1)


def kernel(x_prompt, x_sample, cache_swa_k, cache_swa_v, state_gla, state_conv, state_ffn,
           norm_mix_pre, norm_mix_post, norm_ffn_pre, norm_ffn_post, w_in_even, w_gate_up, b_gate,
           attn_sinks, gla_norm, w_out_even, w_in_odd, conv_w_odd, w_out_odd, ffn_up, ffn_conv_w,
           ffn_conv_b, ffn_down):
    nb, S, _ = x_prompt.shape
    ns, L, _ = x_sample.shape
    depth = norm_mix_pre.shape[0]

    xp = x_prompt
    xs = x_sample.transpose(1, 0, 2).reshape(1, L * ns, D_MODEL)
    past_p = max(SUBLANES, 2)
    tm_p = min(TM_TOK, S)

    ks_p, vs_p, gs_p, cs_p, fs_p = [], [], [], [], []
    ks_s, vs_s, gs_s, cs_s = [], [], [], []
    ffn_state_s = jnp.zeros(state_ffn.shape, F32)

    for l in range(depth):
        gpre, gpost = _row(norm_mix_pre[l]), _row(norm_mix_post[l])
        if l % 2 == 0:
            e = l // 2
            w_main, w_g, w_gu, b_g, w_o = _prep_even(w_in_even[e], w_gate_up[e], b_gate[e],
                                                     w_out_even[e])
            gn = _row(gla_norm[e])
            qa, kv, qb, kb, vb, rb, gb = _inproj_even(xp.reshape(nb * S, D_MODEL), gpre,
                                                      w_main, w_g, w_gu, b_g)
            r3 = lambda a: a.reshape(nb, S, a.shape[-1])
            to_bf16 = []
            if e == 0:
                to_bf16 = [ffn_up.reshape(-1, F2), ffn_down.reshape(-1, D_MODEL)]
                if w_in_odd.shape[0]:
                    to_bf16 += [w_in_odd.reshape(-1, 3 * D_MODEL), w_out_odd.reshape(-1, D_MODEL)]
            oa, ob, st_t, narrowed = _attn_prompt(r3(qa), r3(kv), attn_sinks[e], r3(qb), r3(kb),
                                                  r3(vb), r3(gb), r3(rb), gn, to_bf16)
            if e == 0:
                w_up_all = narrowed[0].reshape(ffn_up.shape)
                w_dn_all = narrowed[1].reshape(ffn_down.shape)
                if w_in_odd.shape[0]:
                    w_in_odd_bf = narrowed[2].reshape(w_in_odd.shape)
                    w_out_odd_bf = narrowed[3].reshape(w_out_odd.shape)
            mix_p = ((oa, 0), (ob, 0))
            kv_last = r3(kv)[:, S - WINDOW:, :]
            ks_p.append(kv_last[..., :LANES].reshape(nb, WINDOW, HKV_A, HD_A))
            vs_p.append(kv_last[..., LANES:].reshape(nb, WINDOW, HKV_A, HD_A))
            gs_p.append(st_t.transpose(0, 2, 1).reshape(nb, H_B, DK_B, DV_B))
            qa, kv, qb, kb, vb, rb, gb = _inproj_even(xs.reshape(L * ns, D_MODEL), gpre,
                                                      w_main, w_g, w_gu, b_g)
            feat_major = lambda c: c.transpose(0, 2, 3, 1).reshape(ns, LANES, WINDOW)
            win_major = lambda c: c.reshape(ns, HKV_A, HD_A, WINDOW).transpose(0, 3, 1, 2)
            oa, nkt, nvt = _swa_decode(qa, kv, feat_major(cache_swa_k[e]),
                                       feat_major(cache_swa_v[e]), attn_sinks[e], ns=ns, L=L)
            nk, nv = win_major(nkt), win_major(nvt)
            ob, s1 = _gla_decode(qb, kb, vb, gb, rb, gn, state_gla[e].reshape(ns, QB_W, DV_B),
                                 ns=ns, L=L)
            mix_s = ((oa.reshape(1, L * ns, QA_W), 0), (ob.reshape(1, L * ns, VB_W), 0))
            ks_s.append(nk.reshape(ns, WINDOW, HKV_A, HD_A))
            vs_s.append(nv.reshape(ns, WINDOW, HKV_A, HD_A))
            gs_s.append(s1.reshape(ns, H_B, DK_B, DV_B))
        else:
            o = l // 2
            w_in = w_in_odd_bf[o]
            w_o = w_out_odd_bf[o]
            y, st = _odd_in(xp, jnp.zeros((nb, past_p, D_MODEL), F32), gpre, w_in, conv_w_odd[o],
                            tm=min(TM_ODD, S), rs=1)
            mix_p = ((y, 0), (y, 1))
            cs_p.append(st[:, past_p - 2:, :])
            y, st = _odd_in(xs, state_conv, gpre, w_in, conv_w_odd[o], tm=L * ns, rs=ns,
                            state_index=o)
            mix_s = ((y, 0), (y, 1))
            cs_s.append(st)

        gmix = gpost
        gpre, gpost = _row(norm_ffn_pre[l]), _row(norm_ffn_post[l])
        cb = _row(ffn_conv_b[l])
        xp, st = _ffn(xp, mix_p[0], mix_p[1], w_o, gmix, jnp.zeros((nb, past_p, F2), F32), gpre,
                      w_up_all, ffn_conv_w[l], cb, w_dn_all, gpost, l, tm=tm_p, rs=1)
        fs_p.append(st[:, past_p - 2:, :])
        xs, ffn_state_s = _ffn(xs, mix_s[0], mix_s[1], w_o, gmix, state_ffn, gpre, w_up_all,
                               ffn_conv_w[l], cb, w_dn_all, gpost, l, tm=L * ns, rs=ns,
                               state_out=ffn_state_s)

    y_sample = xs.reshape(L, ns, D_MODEL).transpose(1, 0, 2)
    return (xp, y_sample, jnp.stack(ks_p), jnp.stack(vs_p), jnp.stack(gs_p), jnp.stack(cs_p),
            jnp.stack(fs_p), jnp.stack(ks_s), jnp.stack(vs_s), jnp.stack(gs_s), jnp.stack(cs_s),
            ffn_state_s)
```

```python
import functools

import numpy as np
import jax
import jax.numpy as jnp
from jax import lax
from jax.experimental import pallas as pl
from jax.experimental.pallas import tpu as pltpu

F32 = jnp.float32
BF16 = jnp.bfloat16

D_MODEL = 1024
WINDOW = 128
HD_A = 64
HQ_A = 8
HKV_A = 2
G_A = HQ_A // HKV_A
H_B = 4
DK_B = 64
DV_B = 128
GATE_RANK = 16
GATE_TAU = 16.0
GLA_CHUNK = 64
D_FF = ((8 * D_MODEL // 3 + 127) // 128) * 128
F2 = 2 * D_FF
EPS = 1e-6
GELU_C = float(np.sqrt(2.0 / np.pi))
GELU_A = 0.044715

QA_W = HQ_A * HD_A
KV_W = 2 * HKV_A * HD_A
QB_W = H_B * DK_B
VB_W = H_B * DV_B
MAIN_W = QA_W + KV_W + 2 * QB_W + 2 * VB_W

LANES = 128
SUBLANES = 8
VMEM_LIMIT = 56 * 1024 * 1024

TM_TOK = 512
TM_INPROJ = 1024
TM_ODD = 1024
TM_ATTN = 1024
GLA_GROUP = 128
ODD_TC = 256
FFN_TF = 256
FFN_ELEM_DTYPE = BF16
SEQ_BLOCK = 32
DEC_UNROLL = 16


def _cparams(*sem):
    return pltpu.CompilerParams(dimension_semantics=sem, vmem_limit_bytes=VMEM_LIMIT)


def _rms(x, g):
    return x * lax.rsqrt(jnp.mean(x * x, axis=-1, keepdims=True) + EPS) * g


def _dot(a, b):
    return jnp.dot(a, b, preferred_element_type=F32)


def _dot_nt(a, b):
    return lax.dot_general(a, b, (((1,), (1,)), ((), ())), preferred_element_type=F32)


def _dot_tn(a, b):
    return lax.dot_general(a, b, (((0,), (0,)), ((), ())), preferred_element_type=F32)


def _split3(x):
    hi = x.astype(BF16)
    r1 = x - hi.astype(F32)
    mid = r1.astype(BF16)
    lo = (r1 - mid.astype(F32)).astype(BF16)
    return hi, mid, lo


def _inproj_even_kernel(x_ref, gpre_ref, w_ref, wg_ref, wgu_ref, bg_ref,
                        qa_ref, kv_ref, qb_ref, kb_ref, vb_ref, rb_ref, gb_ref):
    h = _rms(x_ref[...], gpre_ref[...]).astype(BF16)

    def mm(lo, width):
        return _dot(h, w_ref[:, lo:lo + width])

    lo = 0
    qa_ref[...] = (mm(lo, QA_W) * (HD_A ** -0.5)).astype(BF16)
    lo += QA_W
    kv_ref[...] = mm(lo, KV_W)
    lo += KV_W
    qb_ref[...] = mm(lo, QB_W) * (DK_B ** -0.5)
    lo += QB_W
    kb_ref[...] = mm(lo, QB_W)
    lo += QB_W
    vb_ref[...] = mm(lo, VB_W).astype(BF16)
    lo += VB_W
    rb_ref[...] = mm(lo, VB_W)
    glr = _dot(h, wg_ref[...]).astype(BF16)
    z = _dot(glr, wgu_ref[...]) + bg_ref[...]
    gb_ref[...] = (jnp.minimum(z, 0.0) - jnp.log1p(jnp.exp(-jnp.abs(z)))) * (1.0 / GATE_TAU)


def _inproj_even(x2d, gpre, w_main, w_g, w_gu, b_g):
    T = x2d.shape[0]
    tm = min(TM_INPROJ, T)
    row = lambda w: pl.BlockSpec((tm, w), lambda i: (i, 0))
    full = lambda a: pl.BlockSpec(a.shape, lambda i: (0,) * a.ndim)
    out_shape = (
        jax.ShapeDtypeStruct((T, QA_W), BF16), jax.ShapeDtypeStruct((T, KV_W), F32),
        jax.ShapeDtypeStruct((T, QB_W), F32), jax.ShapeDtypeStruct((T, QB_W), F32),
        jax.ShapeDtypeStruct((T, VB_W), BF16), jax.ShapeDtypeStruct((T, VB_W), F32),
        jax.ShapeDtypeStruct((T, QB_W), F32))
    return pl.pallas_call(
        _inproj_even_kernel, grid=(T // tm,),
        in_specs=[row(D_MODEL), full(gpre), full(w_main), full(w_g), full(w_gu), full(b_g)],
        out_specs=(row(QA_W), row(KV_W), row(QB_W), row(QB_W), row(VB_W), row(VB_W), row(QB_W)),
        out_shape=out_shape, compiler_params=_cparams("parallel"), name="inproj_even",
    )(x2d, gpre, w_main, w_g, w_gu, b_g)


def _conv3_from_buf(buf, cw, cur, tm, past, rs, dtype=F32):
    cw = cw.astype(dtype)
    y = cw[0:1, :] * buf[past - 2 * rs:past - 2 * rs + tm, :].astype(dtype)
    y = y + cw[1:2, :] * buf[past - rs:past - rs + tm, :].astype(dtype)
    return y + cw[2:3, :] * cur.astype(dtype)


def _load_history(carry, past_ref, rs, seq_major):
    if seq_major:
        carry[0:rs, :] = past_ref[:, 0, :]
        carry[rs:2 * rs, :] = past_ref[:, 1, :]
    else:
        carry[...] = past_ref[0]


def _store_history(st_ref, carry, rs, seq_major):
    if seq_major:
        st_ref[:, 0, :] = carry[0:rs, :]
        st_ref[:, 1, :] = carry[rs:2 * rs, :]
    else:
        st_ref[0] = carry[...]


def _history_specs(past0, index, channels, seq_major):
    if seq_major:
        ns = past0.shape[1]
        return (2 * ns,
                pl.BlockSpec((None, ns, 2, channels), lambda b, j: (index, 0, 0, 0)),
                pl.BlockSpec((ns, 2, channels), lambda b, j: (0, 0, 0)),
                jax.ShapeDtypeStruct((ns, 2, channels), F32))
    past = past0.shape[1]
    spec = pl.BlockSpec((1, past, channels), lambda b, j: (b, 0, 0))
    return past, spec, spec, jax.ShapeDtypeStruct((past0.shape[0], past, channels), F32)


def _odd_in_kernel(x_ref, past_ref, gpre_ref, win_ref, cw_ref, y_ref, st_ref, carry, buf,
                   *, tm, past, rs, tc, seq_major):
    @pl.when(pl.program_id(1) == 0)
    def _():
        _load_history(carry, past_ref, rs, seq_major)

    h = _rms(x_ref[0], gpre_ref[...]).astype(BF16)
    for c in range(D_MODEL // tc):
        lo = c * tc
        bg = _dot(h, win_ref[:, lo:lo + tc])
        cu = _dot(h, win_ref[:, D_MODEL + lo:D_MODEL + lo + tc]) * \
            _dot(h, win_ref[:, 2 * D_MODEL + lo:2 * D_MODEL + lo + tc])
        buf[0:past, :] = carry[:, lo:lo + tc]
        buf[past:past + tm, :] = cu
        carry[:, lo:lo + tc] = buf[tm:tm + past, :]
        z = _conv3_from_buf(buf, cw_ref[:, lo:lo + tc], cu, tm, past, rs)
        y_ref[0, :, lo:lo + tc] = (bg * z).astype(BF16)
    _store_history(st_ref, carry, rs, seq_major)


def _odd_in(x3d, past0, gpre, w_in, conv_w, *, tm, rs, state_index=None):
    nb, rows, _ = x3d.shape
    seq_major = state_index is not None
    past, past_spec, st_spec, st_shape = _history_specs(past0, state_index, D_MODEL, seq_major)
    tc = ODD_TC
    kern = functools.partial(_odd_in_kernel, tm=tm, past=past, rs=rs, tc=tc, seq_major=seq_major)
    full = lambda a: pl.BlockSpec(a.shape, lambda b, j: (0,) * a.ndim)
    return pl.pallas_call(
        kern, grid=(nb, rows // tm),
        in_specs=[pl.BlockSpec((1, tm, D_MODEL), lambda b, j: (b, j, 0)), past_spec,
                  full(gpre), full(w_in), full(conv_w)],
        out_specs=(pl.BlockSpec((1, tm, D_MODEL), lambda b, j: (b, j, 0)), st_spec),
        out_shape=(jax.ShapeDtypeStruct((nb, rows, D_MODEL), BF16), st_shape),
        scratch_shapes=[pltpu.VMEM((past, D_MODEL), F32), pltpu.VMEM((past + tm, tc), F32)],
        compiler_params=_cparams("parallel", "arbitrary"), name="odd_in",
    )(x3d, past0, gpre, w_in, conv_w)


def _ffn_kernel(x_ref, ya_ref, yb_ref, wo_ref, gmix_ref, past_ref, gpre_ref, wup_ref, cw_ref,
                cb_ref, wdn_ref, gpost_ref, gk_ref, o_ref, st_ref,
                carry, buf_g, buf_v, acc, h_scr, act, x1_scr, *, tm, past, rs, tf, seq_major):
    @pl.when(pl.program_id(1) == 0)
    def _():
        _load_history(carry, past_ref, rs, seq_major)

    mix = _dot(ya_ref[0], wo_ref[0]) + _dot(yb_ref[0], wo_ref[1])
    x1_scr[...] = x_ref[0] + _rms(mix, gmix_ref[...])
    h_scr[...] = _rms(x1_scr[...], gpre_ref[...]).astype(BF16)
    nf = D_FF // tf

    def up(c):
        for lo, buf in ((c * tf, buf_g), (D_FF + c * tf, buf_v)):
            buf[c % 2, 0:past, :] = carry[:, lo:lo + tf]
            buf[c % 2, past:past + tm, :] = _dot(h_scr[...], wup_ref[:, lo:lo + tf])
            carry[:, lo:lo + tf] = buf[c % 2, tm:tm + past, :]

    gelu_c = gk_ref[0:1, 0:1].astype(FFN_ELEM_DTYPE)
    gelu_ca = gk_ref[0:1, 1:2].astype(FFN_ELEM_DTYPE)

    def elem(c):
        halves = []
        for lo, buf, scale in ((c * tf, buf_g, 1.0), (D_FF + c * tf, buf_v, 0.5)):
            b = buf.at[c % 2]
            halves.append(_conv3_from_buf(b, cw_ref[:, lo:lo + tf] * scale, b[past:past + tm, :],
                                          tm, past, rs, FFN_ELEM_DTYPE)
                          + (cb_ref[:, lo:lo + tf] * scale).astype(FFN_ELEM_DTYPE))
        x, half_v = halves
        t = jnp.tanh(x * (x * x * gelu_ca + gelu_c))
        act[c % 2] = ((x * t + x) * half_v).astype(BF16)

    def down(c):
        part = _dot(act[c % 2], wdn_ref[c * tf:(c + 1) * tf, :])
        if c == 0:
            acc[...] = part
        else:
            acc[...] += part

    for c in range(nf + 2):
        if c < nf:
            up(c)
        if 1 <= c <= nf:
            elem(c - 1)
        if c >= 2:
            down(c - 2)
    o_ref[0] = x1_scr[...] + _rms(acc[...], gpost_ref[...])
    _store_history(st_ref, carry, rs, seq_major)


def _ffn(x3d, ya, yb, w_o, gmix, past0, gpre, w_up, conv_w, conv_b, w_dn, gpost, layer, *, tm, rs,
         seq_major=False):
    nb, rows, _ = x3d.shape
    past, past_spec, st_spec, st_shape = _history_specs(past0, layer, F2, seq_major)
    half = w_o.shape[0] // 2
    kern = functools.partial(_ffn_kernel, tm=tm, past=past, rs=rs, tf=FFN_TF, seq_major=seq_major)
    gelu_k = jnp.zeros((1, LANES), F32).at[0, 0].set(GELU_C).at[0, 1].set(GELU_C * GELU_A)
    w_o2 = w_o.reshape(2, half, D_MODEL)
    once = dict(pipeline_mode=pl.Buffered(1))
    full = lambda a: pl.BlockSpec(a.shape, lambda b, j: (0,) * a.ndim, **once)
    per_layer = lambda a: pl.BlockSpec((None,) + a.shape[1:], lambda b, j: (layer, 0, 0), **once)
    tok = lambda w, cblk=0: pl.BlockSpec((1, tm, w), lambda b, j: (b, j, cblk))
    return pl.pallas_call(
        kern, grid=(nb, rows // tm),
        in_specs=[tok(D_MODEL), tok(half, ya[1]), tok(half, yb[1]), full(w_o2), full(gmix),
                  past_spec,
                  full(gpre), per_layer(w_up), full(conv_w), full(conv_b), per_layer(w_dn),
                  full(gpost), full(gelu_k)],
        out_specs=(tok(D_MODEL), st_spec),
        out_shape=(jax.ShapeDtypeStruct((nb, rows, D_MODEL), F32), st_shape),
        scratch_shapes=[pltpu.VMEM((past, F2), F32), pltpu.VMEM((2, past + tm, FFN_TF), F32),
                        pltpu.VMEM((2, past + tm, FFN_TF), F32), pltpu.VMEM((tm, D_MODEL), F32),
                        pltpu.VMEM((tm, D_MODEL), BF16), pltpu.VMEM((2, tm, FFN_TF), BF16),
                        pltpu.VMEM((tm, D_MODEL), F32)],
        compiler_params=_cparams("parallel", "arbitrary"), name="conv_ffn",
    )(x3d, ya[0], yb[0], w_o2, gmix, past0, gpre, w_up, conv_w, conv_b, w_dn, gpost, gelu_k)


def _swa_bias_prompt():
    qi = np.arange(WINDOW)[:, None]
    sj = np.arange(2 * WINDOW)[None, :]
    dist = WINDOW + qi - sj
    valid = (dist >= 0) & (dist <= WINDOW)
    slopes = 2.0 ** (-8.0 * np.arange(1, HQ_A + 1) / HQ_A)
    bias = np.where(valid[None], -slopes[:, None, None] * dist[None].astype(np.float64), -np.inf)
    return jnp.asarray(bias, dtype=F32)


def _swa_prompt_kernel(sink_ref, q_ref, kvc_ref, kvp_ref, bias_ref, o_ref, *, tq):
    i = pl.program_id(1)
    kvc = kvc_ref[0]
    kvp = kvp_ref[0]
    kcat = jnp.concatenate([kvp[:, 0:LANES], kvc[:, 0:LANES]], axis=0).astype(BF16)
    vcat = jnp.concatenate([kvp[:, LANES:], kvc[:, LANES:]], axis=0).astype(BF16)
    lane = lax.broadcasted_iota(jnp.int32, (WINDOW, LANES), 1)
    col = lax.broadcasted_iota(jnp.int32, (WINDOW, 2 * WINDOW), 1)
    for j in range(tq // WINDOW):
        keys = kcat[j * WINDOW:(j + 2) * WINDOW, :]
        vals = vcat[j * WINDOW:(j + 2) * WINDOW, :]
        for g in range(G_A):
            q2 = q_ref[0, j * WINDOW:(j + 1) * WINDOW, g * LANES:(g + 1) * LANES]
            outs = []
            for kv in range(HKV_A):
                hq = kv * G_A + g
                in_head = (lane < HD_A) if kv == 0 else (lane >= HD_A)
                qm = jnp.where(in_head, q2, jnp.zeros_like(q2))
                s = _dot_nt(qm, keys) + bias_ref[hq]
                if j == 0:
                    s = jnp.where(jnp.logical_and(i == 0, col < WINDOW), -jnp.inf, s)
                sink = sink_ref[hq]
                m = jnp.maximum(jnp.max(s, axis=-1, keepdims=True), sink)
                p = jnp.exp(s - m)
                den = jnp.sum(p, axis=-1, keepdims=True) + jnp.exp(sink - m)
                outs.append(_dot(p.astype(BF16), vals) / den)
            o2 = jnp.where(lane < HD_A, outs[0], outs[1])
            o_ref[0, j * WINDOW:(j + 1) * WINDOW, g * LANES:(g + 1) * LANES] = o2.astype(BF16)


GLA_LEVELS = (0, 1, 2, 4, 8, 16, 32)
LOG2E = 1.4426950408889634


def _gla_constants():
    r = np.arange(GLA_GROUP)
    i, j = r[:, None], r[None, :]
    ranges = [(i >= j) & (i // GLA_CHUNK == j // GLA_CHUNK)]
    masks = [i == j]
    for s in GLA_LEVELS[1:]:
        mid = (i // (2 * s)) * (2 * s) + s
        upper = (i % (2 * s) >= s) & (j >= mid) & (j <= i)
        lower = (i % (2 * s) < s) & (j > i) & (j < mid)
        ranges.append(upper | lower)
        masks.append((i // (2 * s) == j // (2 * s)) & (i % (2 * s) >= s) & (j % (2 * s) < s))
    ones_bd = (np.arange(QB_W)[:, None] // DK_B) == (np.arange(VB_W)[None, :] // DV_B)
    stacked = np.concatenate(ranges, axis=0)
    return (jnp.asarray(np.concatenate([stacked, stacked], axis=1), dtype=BF16),
            jnp.asarray(ones_bd, dtype=BF16), jnp.asarray(np.stack(masks), dtype=F32))


def _head_norm_gate(o, r, gn):
    parts = []
    for h in range(H_B):
        oh = o[:, h * DV_B:(h + 1) * DV_B]
        parts.append(oh * lax.rsqrt(jnp.mean(oh * oh, axis=-1, keepdims=True) + EPS))
    return jnp.concatenate(parts, axis=-1) * gn * jax.nn.silu(r)


def _gla_prompt_kernel(q_ref, k_ref, v_ref, g_ref, r_ref, gn_ref, rng_ref, lvl_ref,
                       o_ref, st_ref, state, b_scr, o_scr, qk_scr, *, tm):
    @pl.when(pl.program_id(1) == 0)
    def _():
        state[...] = jnp.zeros_like(state)

    q, k, v, b = q_ref.at[0], k_ref.at[0], v_ref.at[0], b_scr
    lane = lax.broadcasted_iota(jnp.int32, (1, LANES), 1)
    head_lanes = (lane < DK_B, lane >= DK_B)
    head_cols = lax.broadcasted_iota(jnp.int32, (1, QB_W), 1) // DK_B

    g2 = g_ref[0] * LOG2E
    ghi = g2.astype(BF16)
    glo = (g2 - ghi.astype(F32)).astype(BF16)
    nlev = len(GLA_LEVELS)
    for gi in range(tm // GLA_GROUP):
        rows = slice(gi * GLA_GROUP, (gi + 1) * GLA_GROUP)
        d_all = _dot(rng_ref[...], jnp.concatenate([ghi[rows], glo[rows]], axis=0))
        b_scr[rows, :] = d_all[0:GLA_GROUP]
        qk_scr[0, 0, rows, :] = q[rows].astype(BF16)
        qk_scr[0, 1, rows, :] = k[rows].astype(BF16)
        for li in range(1, nlev):
            e = jnp.exp2(d_all[li * GLA_GROUP:(li + 1) * GLA_GROUP])
            qk_scr[li, 0, rows, :] = (q[rows] * e).astype(BF16)
            qk_scr[li, 1, rows, :] = (k[rows] * e).astype(BF16)

    for gi in range(tm // GLA_GROUP):
        rows = slice(gi * GLA_GROUP, (gi + 1) * GLA_GROUP)
        for pi in range(H_B // 2):
            pr = slice(pi * LANES, (pi + 1) * LANES)
            a = None
            for li in range(nlev):
                ke = qk_scr[li, 1, rows, pr]
                kem = jnp.concatenate([jnp.where(sel, ke, jnp.zeros_like(ke)) for sel in head_lanes],
                                      axis=0)
                term = _dot_nt(qk_scr[li, 0, rows, pr], kem)
                keep = lvl_ref[li] > 0.5
                keep = jnp.concatenate([keep, keep], axis=1)
                a = jnp.where(keep, term, 0.0 if a is None else a)
            a = a.astype(BF16)
            for par in range(2):
                cols = slice((2 * pi + par) * DV_B, (2 * pi + par + 1) * DV_B)
                o_scr[rows, cols] = _dot(a[:, par * GLA_GROUP:(par + 1) * GLA_GROUP], v[rows, cols])

    for c in range(tm // GLA_CHUNK):
        rows = slice(c * GLA_CHUNK, (c + 1) * GLA_CHUNK)
        bc = b[rows]
        b_end = bc[GLA_CHUNK - 1:GLA_CHUNK, :]
        qe = (q[rows] * jnp.exp2(bc)).astype(BF16)
        ke = (k[rows] * jnp.exp2(b_end - bc)).astype(BF16)
        st = state[...]
        qm = jnp.concatenate([jnp.where(head_cols == h, qe, jnp.zeros_like(qe))
                              for h in range(H_B)], axis=0)
        km = jnp.concatenate([jnp.where(head_cols == h, ke, jnp.zeros_like(ke))
                              for h in range(H_B)], axis=0)
        vm = jnp.concatenate([v[rows, h * DV_B:(h + 1) * DV_B] for h in range(H_B)], axis=0)
        oi = _dot_nt(qm, st.astype(BF16))
        for h in range(H_B):
            o_scr[rows, h * DV_B:(h + 1) * DV_B] += oi[h * GLA_CHUNK:(h + 1) * GLA_CHUNK]
        state[...] = st * jnp.exp2(b_end) + _dot_tn(vm, km)

    o_ref[0] = _head_norm_gate(o_scr[...], r_ref[0], gn_ref[...]).astype(BF16)
    st_ref[0] = state[...]


N_ATTN_IN = 13


def _attn_prompt_kernel(*refs, tm, n_cast):
    (sink_ref, qa_ref, kvc_ref, kvp_ref, bias_ref,
     q_ref, k_ref, v_ref, g_ref, r_ref, gn_ref, rng_ref, lvl_ref) = refs[:N_ATTN_IN]
    cast_in = refs[N_ATTN_IN:N_ATTN_IN + n_cast]
    oa_ref, ob_ref, st_ref = refs[N_ATTN_IN + n_cast:N_ATTN_IN + n_cast + 3]
    cast_out = refs[N_ATTN_IN + n_cast + 3:N_ATTN_IN + 2 * n_cast + 3]
    state, b_scr, o_scr, qk_scr = refs[N_ATTN_IN + 2 * n_cast + 3:]
    _gla_prompt_kernel(q_ref, k_ref, v_ref, g_ref, r_ref, gn_ref, rng_ref, lvl_ref,
                       ob_ref, st_ref, state, b_scr, o_scr, qk_scr, tm=tm)
    _swa_prompt_kernel(sink_ref, qa_ref, kvc_ref, kvp_ref, bias_ref, oa_ref, tq=tm)
    for src, dst in zip(cast_in, cast_out):
        dst[...] = src[...].astype(BF16)


def _attn_prompt(qa3d, kv3d, sinks, q3d, k3d, v3d, g3d, r3d, gn, to_bf16):
    nb, S, _ = q3d.shape
    tm = min(TM_ATTN, S)
    nj = S // tm
    nsteps = nb * nj
    bias = _swa_bias_prompt()
    rng, _, lvl = _gla_constants()
    kern = functools.partial(_attn_prompt_kernel, tm=tm, n_cast=len(to_bf16))
    blocks_per_tile = tm // WINDOW
    tok = lambda w: pl.BlockSpec((1, tm, w), lambda b, j: (b, j, 0))
    full = lambda a: pl.BlockSpec(a.shape, lambda b, j: (0,) * a.ndim)
    prev_kv = pl.BlockSpec((1, WINDOW, KV_W),
                           lambda b, j: (b, jnp.maximum(j * blocks_per_tile - 1, 0), 0))

    def slab(a):
        rows = a.shape[0] // nsteps
        assert rows * nsteps == a.shape[0] and rows % (2 * SUBLANES) == 0, a.shape
        return pl.BlockSpec((rows, a.shape[1]), lambda b, j: (b * nj + j, 0))

    outs = pl.pallas_call(
        kern, grid=(nb, nj),
        in_specs=[pl.BlockSpec(memory_space=pltpu.SMEM), tok(QA_W), tok(KV_W), prev_kv, full(bias),
                  tok(QB_W), tok(QB_W), tok(VB_W), tok(QB_W), tok(VB_W),
                  full(gn), full(rng), full(lvl)] + [slab(a) for a in to_bf16],
        out_specs=[tok(QA_W), tok(VB_W), pl.BlockSpec((1, DV_B, QB_W), lambda b, j: (b, 0, 0))]
        + [slab(a) for a in to_bf16],
        out_shape=[jax.ShapeDtypeStruct((nb, S, QA_W), BF16),
                   jax.ShapeDtypeStruct((nb, S, VB_W), BF16),
                   jax.ShapeDtypeStruct((nb, DV_B, QB_W), F32)]
        + [jax.ShapeDtypeStruct(a.shape, BF16) for a in to_bf16],
        scratch_shapes=[pltpu.VMEM((DV_B, QB_W), F32), pltpu.VMEM((tm, QB_W), F32),
                        pltpu.VMEM((tm, VB_W), F32),
                        pltpu.VMEM((len(GLA_LEVELS), 2, tm, QB_W), BF16)],
        compiler_params=_cparams("parallel", "arbitrary"), name="attn_prompt",
    )(sinks, qa3d, kv3d, kv3d, bias, q3d, k3d, v3d, g3d, r3d, gn, rng, lvl, *to_bf16)
    return outs[0], outs[1], outs[2], outs[3:]


def _swa_decode_constants(L):
    rows = HQ_A * L
    hq = np.arange(rows)[:, None] // L
    t = np.arange(rows)[:, None] % L
    s = np.arange(WINDOW + SUBLANES)[None, :]
    dist = WINDOW + t - s
    valid = (dist >= 0) & (dist <= WINDOW) & (s < WINDOW + L)
    slopes = 2.0 ** (-8.0 * (hq + 1) / HQ_A)
    bias = np.where(valid, -slopes * dist, -np.inf)
    place = np.zeros((4 * SUBLANES, WINDOW), np.float32)
    for part in range(3):
        for tt in range(L):
            place[part * SUBLANES + tt, WINDOW - L + tt] = 1.0
    return (jnp.asarray(bias[:, :WINDOW], dtype=F32), jnp.asarray(bias[:, WINDOW:], dtype=F32),
            jnp.asarray(place, dtype=BF16))


def _swa_decode_kernel(q_ref, kvn_ref, kt_ref, vt_ref, bias_c_ref, bias_n_ref, place_ref, sink_ref,
                       o_ref, nkt_ref, nvt_ref, qs, os_, kn, vn, *, ns, L, sb):
    i = pl.program_id(0)
    rows_q = HQ_A * L
    lane = lax.broadcasted_iota(jnp.int32, (1, LANES), 1)

    @pl.when(i == 0)
    def _():
        q = q_ref[...].astype(F32)
        for kv in range(HKV_A):
            sel = (lane < HD_A) if kv == 0 else (lane >= HD_A)
            for g in range(G_A):
                base = (kv * G_A + g) * L * ns
                qs[base:base + L * ns, :] = jnp.where(sel, q[:, g * LANES:(g + 1) * LANES], 0.0)
        kn[...] = jnp.zeros_like(kn)
        vn[...] = jnp.zeros_like(vn)
        kn[0:L * ns, :] = kvn_ref[:, 0:LANES]
        vn[0:L * ns, :] = kvn_ref[:, LANES:]

    bias_c = bias_c_ref[...]
    bias_n = bias_n_ref[...]
    sink = sink_ref[...]
    place = place_ref[...]
    zero8 = jnp.zeros((SUBLANES, LANES), F32)

    def shifted_cache(old, new8):
        hi = new8.astype(BF16).astype(F32)
        mid = (new8 - hi).astype(BF16).astype(F32)
        lo = new8 - hi - mid
        parts = jnp.concatenate([hi, mid, lo, zero8], axis=0).astype(BF16)
        placed = _dot_tn(parts, place)
        return jnp.where(lane >= WINDOW - L, placed, pltpu.roll(old, WINDOW - L, axis=1))

    def body(t, carry):
        lhs, k_new, v_new = [], [], []
        for u in range(DEC_UNROLL):
            s = t * DEC_UNROLL + u
            seq = i * sb + s
            k8 = kn[pl.ds(seq, SUBLANES, stride=ns), :]
            v8 = vn[pl.ds(seq, SUBLANES, stride=ns), :]
            nkt_ref[s] = shifted_cache(kt_ref[s], k8)
            nvt_ref[s] = shifted_cache(vt_ref[s], v8)
            lhs.append(qs[pl.ds(seq, rows_q, stride=ns), :].astype(BF16))
            k_new.append(k8.astype(BF16))
            v_new.append(v8.astype(BF16))
        lhs, k_new, v_new = jnp.stack(lhs), jnp.stack(k_new), jnp.stack(v_new)
        blk = pl.ds(pl.multiple_of(t * DEC_UNROLL, DEC_UNROLL), DEC_UNROLL)
        kt = kt_ref[blk].astype(BF16)
        vt = vt_ref[blk].astype(BF16)
        sc_c = jnp.einsum("uqd,udw->uqw", lhs, kt, preferred_element_type=F32) + bias_c
        sc_n = jnp.einsum("uqd,utd->uqt", lhs, k_new, preferred_element_type=F32) + bias_n
        m = jnp.maximum(jnp.maximum(jnp.max(sc_c, axis=-1, keepdims=True),
                                    jnp.max(sc_n, axis=-1, keepdims=True)), sink)
        p_c = jnp.exp(sc_c - m)
        p_n = jnp.exp(sc_n - m)
        den = (jnp.sum(p_c, axis=-1, keepdims=True) + jnp.sum(p_n, axis=-1, keepdims=True)
               + jnp.exp(sink - m))
        res = (jnp.einsum("uqw,udw->uqd", p_c.astype(BF16), vt, preferred_element_type=F32)
               + jnp.einsum("uqt,utd->uqd", p_n.astype(BF16), v_new,
                            preferred_element_type=F32)) / den
        half = rows_q // 2
        for u in range(DEC_UNROLL):
            seq = i * sb + t * DEC_UNROLL + u
            os_[pl.ds(seq, half, stride=ns), :] = jnp.where(lane < HD_A, res[u, 0:half],
                                                            res[u, half:])
        return carry

    lax.fori_loop(0, sb // DEC_UNROLL, body, 0)

    @pl.when(i == pl.num_programs(0) - 1)
    def _():
        for g in range(G_A):
            o_ref[:, g * LANES:(g + 1) * LANES] = os_[g * L * ns:(g + 1) * L * ns, :].astype(BF16)


def _swa_decode(q_tm, kvn_tm, cache_kt, cache_vt, sinks, *, ns, L):
    sb = min(SEQ_BLOCK, ns)
    assert sb % DEC_UNROLL == 0 and L <= SUBLANES
    bias_c, bias_n, place = _swa_decode_constants(L)
    sink_col = jnp.broadcast_to(jnp.repeat(sinks.astype(F32), L)[:, None], (HQ_A * L, 1))
    kern = functools.partial(_swa_decode_kernel, ns=ns, L=L, sb=sb)
    full = lambda a: pl.BlockSpec(a.shape, lambda i: (0,) * a.ndim)
    cache = pl.BlockSpec((sb, LANES, WINDOW), lambda i: (i, 0, 0))
    return pl.pallas_call(
        kern, grid=(ns // sb,),
        in_specs=[full(q_tm), full(kvn_tm), cache, cache, full(bias_c), full(bias_n), full(place),
                  full(sink_col)],
        out_specs=(pl.BlockSpec((L * ns, QA_W), lambda i: (0, 0)), cache, cache),
        out_shape=(jax.ShapeDtypeStruct((L * ns, QA_W), BF16),
                   jax.ShapeDtypeStruct(cache_kt.shape, F32),
                   jax.ShapeDtypeStruct(cache_vt.shape, F32)),
        scratch_shapes=[pltpu.VMEM((HQ_A * L * ns, LANES), F32),
                        pltpu.VMEM((G_A * L * ns, LANES), F32),
                        pltpu.VMEM((SUBLANES * ns, LANES), F32),
                        pltpu.VMEM((SUBLANES * ns, LANES), F32)],
        compiler_params=_cparams("arbitrary"), name="swa_decode",
    )(q_tm, kvn_tm, cache_kt, cache_vt, bias_c, bias_n, place, sink_col)


def _gla_decode_kernel(q_ref, k_ref, v_ref, g_ref, r_ref, gn_ref, ones_ref, s0_ref,
                       o_ref, s1_ref, qe2, ke2, v2, dec3, oi2, od, *, ns, L, sb):
    i = pl.program_id(0)
    lane = lax.broadcasted_iota(jnp.int32, (1, LANES), 1)
    npair = H_B // 2

    @pl.when(i == 0)
    def _():
        slab = lambda a, t: a[t * ns:(t + 1) * ns, :]
        q, k, g = q_ref[...], k_ref[...], g_ref[...]
        vf = v_ref[...].astype(F32)
        b = [slab(g, 0)]
        for t in range(1, L):
            b.append(b[-1] + slab(g, t))
        for t in range(L):
            acc = jnp.zeros((ns, VB_W), F32)
            for jj in range(t + 1):
                p = (slab(q, t) * slab(k, jj) * jnp.exp(b[t] - b[jj])).astype(BF16)
                acc = acc + _dot(p, ones_ref[...]) * slab(vf, jj)
            od[t * ns:(t + 1) * ns, :] = acc
            qe = slab(q, t) * jnp.exp(b[t])
            ke = slab(k, t) * jnp.exp(b[L - 1] - b[t])
            for par in range(2):
                sel = (lane < DK_B) if par == 0 else (lane >= DK_B)
                base = (par * L + t) * ns
                for pi in range(npair):
                    pr = slice(pi * LANES, (pi + 1) * LANES)
                    qe2[pi, base:base + ns, :] = jnp.where(sel, qe[:, pr], 0.0)
                    ke2[pi, base:base + ns, :] = jnp.where(sel, ke[:, pr], 0.0)
                    h = 2 * pi + par
                    v2[pi, base:base + ns, :] = slab(vf, t)[:, h * DV_B:(h + 1) * DV_B]
        dec3[...] = jnp.zeros_like(dec3)
        hi, mid, lo = _split3(jnp.exp(b[L - 1]))
        for pi in range(npair):
            pr = slice(pi * LANES, (pi + 1) * LANES)
            dec3[pi, 0:ns, :] = hi[:, pr].astype(F32)
            dec3[pi, ns:2 * ns, :] = mid[:, pr].astype(F32)
            dec3[pi, 2 * ns:3 * ns, :] = lo[:, pr].astype(F32)

    ones8 = jnp.ones((2 * L, LANES), BF16)

    def one_seq(s):
        seq = i * sb + s
        for pi in range(npair):
            take = lambda ref: ref[pi, pl.ds(seq, 2 * L, stride=ns), :].astype(BF16)
            st_p = s0_ref[s, pi * LANES:(pi + 1) * LANES, :]
            oi2[pi, pl.ds(seq, 2 * L, stride=ns), :] = _dot(take(qe2), st_p.astype(BF16))
            dcol = _dot_tn(take(dec3), ones8)
            upd = _dot_tn(take(ke2), take(v2))
            s1_ref[s, pi * LANES:(pi + 1) * LANES, :] = dcol * st_p + upd

    def body(t, carry):
        for u in range(DEC_UNROLL):
            one_seq(t * DEC_UNROLL + u)
        return carry

    lax.fori_loop(0, sb // DEC_UNROLL, body, 0)

    @pl.when(i == pl.num_programs(0) - 1)
    def _():
        for t in range(L):
            parts = []
            for h in range(H_B):
                base = ((h % 2) * L + t) * ns
                parts.append(oi2[h // 2, base:base + ns, :])
            o = jnp.concatenate(parts, axis=-1) + od[t * ns:(t + 1) * ns, :]
            o_ref[t * ns:(t + 1) * ns, :] = _head_norm_gate(
                o, r_ref[t * ns:(t + 1) * ns, :], gn_ref[...]).astype(BF16)


def _gla_decode(q_tm, k_tm, v_tm, g_tm, r_tm, gn, s0, *, ns, L):
    sb = min(SEQ_BLOCK, ns)
    assert 2 * L == SUBLANES
    _, ones_bd, _ = _gla_constants()
    kern = functools.partial(_gla_decode_kernel, ns=ns, L=L, sb=sb)
    full = lambda a: pl.BlockSpec(a.shape, lambda i: (0,) * a.ndim)
    st = pl.BlockSpec((sb, QB_W, DV_B), lambda i: (i, 0, 0))
    rows2 = 2 * L * ns
    return pl.pallas_call(
        kern, grid=(ns // sb,),
        in_specs=[full(q_tm), full(k_tm), full(v_tm), full(g_tm), full(r_tm), full(gn),
                  full(ones_bd), st],
        out_specs=(pl.BlockSpec((L * ns, VB_W), lambda i: (0, 0)), st),
        out_shape=(jax.ShapeDtypeStruct((L * ns, VB_W), BF16),
                   jax.ShapeDtypeStruct(s0.shape, F32)),
        scratch_shapes=[pltpu.VMEM((H_B // 2, rows2, LANES), F32) for _ in range(5)]
        + [pltpu.VMEM((L * ns, VB_W), F32)],
        compiler_params=_cparams("arbitrary"), name="gla_decode",
    )(q_tm, k_tm, v_tm, g_tm, r_tm, gn, ones_bd, s0)


def _qa_perm():
    return np.asarray([(kv * G_A + g) * HD_A + d
                       for g in range(G_A) for kv in range(HKV_A) for d in range(HD_A)])


def _prep_even(w_in, w_gate_up, b_gate, w_out):
    perm = _qa_perm()
    w_main = jnp.concatenate([w_in[:, :QA_W][:, perm], w_in[:, QA_W:MAIN_W]], axis=1).astype(BF16)
    w_g = jnp.pad(w_in[:, MAIN_W:], ((0, 0), (0, LANES - GATE_RANK))).astype(BF16)
    w_gu = jnp.pad(w_gate_up, ((0, LANES - GATE_RANK), (0, 0))).astype(BF16)
    w_o = jnp.concatenate([w_out[:QA_W][perm], w_out[QA_W:]], axis=0).astype(BF16)
    return w_main, w_g, w_gu, b_gate.reshape(1, -1), w_o


def _row(v):
    return v.reshape(1, -1)


def kernel(x_prompt, x_sample, cache_swa_k, cache_swa_v, state_gla, state_conv, state_ffn,
           norm_mix_pre, norm_mix_post, norm_ffn_pre, norm_ffn_post, w_in_even, w_gate_up, b_gate,
           attn_sinks, gla_norm, w_out_even, w_in_odd, conv_w_odd, w_out_odd, ffn_up, ffn_conv_w,
           ffn_conv_b, ffn_down):
    nb, S, _ = x_prompt.shape
    ns, L, _ = x_sample.shape
    depth = norm_mix_pre.shape[0]

    xp = x_prompt
    xs = x_sample.transpose(1, 0, 2).reshape(1, L * ns, D_MODEL)
    past_p = max(SUBLANES, 2)
    tm_p = min(TM_TOK, S)

    ks_p, vs_p, gs_p, cs_p, fs_p = [], [], [], [], []
    ks_s, vs_s, gs_s, cs_s, fs_s = [], [], [], [], []

    for l in range(depth):
        gpre, gpost = _row(norm_mix_pre[l]), _row(norm_mix_post[l])
        if l % 2 == 0:
            e = l // 2
            w_main, w_g, w_gu, b_g, w_o = _prep_even(w_in_even[e], w_gate_up[e], b_gate[e],
                                                     w_out_even[e])
            gn = _row(gla_norm[e])
            qa, kv, qb, kb, vb, rb, gb = _inproj_even(xp.reshape(nb * S, D_MODEL), gpre,
                                                      w_main, w_g, w_gu, b_g)
            r3 = lambda a: a.reshape(nb, S, a.shape[-1])
            to_bf16 = []
            if e == 0:
                to_bf16 = [ffn_up.reshape(-1, F2), ffn_down.reshape(-1, D_MODEL)]
                if w_in_odd.shape[0]:
                    to_bf16 += [w_in_odd.reshape(-1, 3 * D_MODEL), w_out_odd.reshape(-1, D_MODEL)]
            oa, ob, st_t, narrowed = _attn_prompt(r3(qa), r3(kv), attn_sinks[e], r3(qb), r3(kb),
                                                  r3(vb), r3(gb), r3(rb), gn, to_bf16)
            if e == 0:
                w_up_all = narrowed[0].reshape(ffn_up.shape)
                w_dn_all = narrowed[1].reshape(ffn_down.shape)
                if w_in_odd.shape[0]:
                    w_in_odd_bf = narrowed[2].reshape(w_in_odd.shape)
                    w_out_odd_bf = narrowed[3].reshape(w_out_odd.shape)
            mix_p = ((oa, 0), (ob, 0))
            kv_last = r3(kv)[:, S - WINDOW:, :]
            ks_p.append(kv_last[..., :LANES].reshape(nb, WINDOW, HKV_A, HD_A))
            vs_p.append(kv_last[..., LANES:].reshape(nb, WINDOW, HKV_A, HD_A))
            gs_p.append(st_t.transpose(0, 2, 1).reshape(nb, H_B, DK_B, DV_B))
            qa, kv, qb, kb, vb, rb, gb = _inproj_even(xs.reshape(L * ns, D_MODEL), gpre,
                                                      w_main, w_g, w_gu, b_g)
            feat_major = lambda c: c.transpose(0, 2, 3, 1).reshape(ns, LANES, WINDOW)
            win_major = lambda c: c.reshape(ns, HKV_A, HD_A, WINDOW).transpose(0, 3, 1, 2)
            oa, nkt, nvt = _swa_decode(qa, kv, feat_major(cache_swa_k[e]),
                                       feat_major(cache_swa_v[e]), attn_sinks[e], ns=ns, L=L)
            nk, nv = win_major(nkt), win_major(nvt)
            ob, s1 = _gla_decode(qb, kb, vb, gb, rb, gn, state_gla[e].reshape(ns, QB_W, DV_B),
                                 ns=ns, L=L)
            mix_s = ((oa.reshape(1, L * ns, QA_W), 0), (ob.reshape(1, L * ns, VB_W), 0))
            ks_s.append(nk.reshape(ns, WINDOW, HKV_A, HD_A))
            vs_s.append(nv.reshape(ns, WINDOW, HKV_A, HD_A))
            gs_s.append(s1.reshape(ns, H_B, DK_B, DV_B))
        else:
            o = l // 2
            w_in = w_in_odd_bf[o]
            w_o = w_out_odd_bf[o]
            y, st = _odd_in(xp, jnp.zeros((nb, past_p, D_MODEL), F32), gpre, w_in, conv_w_odd[o],
                            tm=min(TM_ODD, S), rs=1)
            mix_p = ((y, 0), (y, 1))
            cs_p.append(st[:, past_p - 2:, :])
            y, st = _odd_in(xs, state_conv, gpre, w_in, conv_w_odd[o], tm=L * ns, rs=ns,
                            state_index=o)
            mix_s = ((y, 0), (y, 1))
            cs_s.append(st)

        gmix = gpost
        gpre, gpost = _row(norm_ffn_pre[l]), _row(norm_ffn_post[l])
        cb = _row(ffn_conv_b[l])
        xp, st = _ffn(xp, mix_p[0], mix_p[1], w_o, gmix, jnp.zeros((nb, past_p, F2), F32), gpre,
                      w_up_all, ffn_conv_w[l], cb, w_dn_all, gpost, l, tm=tm_p, rs=1)
        fs_p.append(st[:, past_p - 2:, :])
        xs, st = _ffn(xs, mix_s[0], mix_s[1], w_o, gmix, state_ffn, gpre, w_up_all, ffn_conv_w[l],
                      cb, w_dn_all, gpost, l, tm=L * ns, rs=ns, seq_major=True)
        fs_s.append(st)

    y_sample = xs.reshape(L, ns, D_MODEL).transpose(1, 0, 2)
    return (xp, y_sample, jnp.stack(ks_p), jnp.stack(vs_p), jnp.stack(gs_p), jnp.stack(cs_p),
            jnp.stack(fs_p), jnp.stack(ks_s), jnp.stack(vs_s), jnp.stack(gs_s), jnp.stack(cs_s),
            jnp.stack(fs_s))
```
